```python
import math
import jax, jax.numpy as jnp
from jax import lax
import numpy as np

D_MODEL = 1024
BATCH = 4
SEQ = 4096
DEPTH = 2

GRID_W = 64
CTX_LEN = 256
HEAD_DIM = 64
ROPE_FREQS = HEAD_DIM // 4
ROPE_THETA = 10000.0
Q_BLOCK = 128
EPS = 1e-6
ATTN_SCALE = HEAD_DIM ** -0.5

FOURIER_GROUP_DIM = 64
FOURIER_GROUPS = (3 * D_MODEL // 8) // FOURIER_GROUP_DIM
FOURIER_WIDTH = FOURIER_GROUPS * FOURIER_GROUP_DIM
DIFF_HEADS = D_MODEL // 256
DIFF_QK_WIDTH = DIFF_HEADS * 2 * HEAD_DIM
DIFF_V_DIM = 2 * HEAD_DIM
DIFF_V_WIDTH = DIFF_HEADS * DIFF_V_DIM
GQA_Q_HEADS = D_MODEL // 128
GQA_GROUP = 4
GQA_KV_HEADS = GQA_Q_HEADS // GQA_GROUP
GQA_Q_WIDTH = GQA_Q_HEADS * HEAD_DIM
GQA_KV_WIDTH = GQA_KV_HEADS * HEAD_DIM
CONV_WIDTH = 3 * D_MODEL // 8
CONV_KERNEL = 31
N_BRANCHES = 4
D_FF = 4 * D_MODEL
N_MOD = 6

KV_SIZES = (DIFF_QK_WIDTH, DIFF_V_WIDTH, GQA_KV_WIDTH, GQA_KV_WIDTH)
REST_SIZES = (FOURIER_WIDTH, DIFF_QK_WIDTH, GQA_Q_WIDTH, 2 * CONV_WIDTH, N_BRANCHES * D_MODEL)
KV_COLS = sum(KV_SIZES)
IN_COLS = KV_COLS + sum(REST_SIZES)

kernel_name = "hybrid_parallel_dit_block"


def _split(t, sizes):
    idx = np.cumsum(sizes)[:-1].tolist()
    return jnp.split(t, idx, axis=-1)


def _rms_norm(x, g):
    xf = x.astype(jnp.float32)
    y = xf * lax.rsqrt(jnp.mean(xf * xf, axis=-1, keepdims=True) + EPS)
    return (y * g.astype(jnp.float32)).astype(x.dtype)


def _layer_norm(x, g, b):
    xf = x.astype(jnp.float32)
    mu = jnp.mean(xf, axis=-1, keepdims=True)
    var = jnp.mean(jnp.square(xf - mu), axis=-1, keepdims=True)
    y = (xf - mu) * lax.rsqrt(var + EPS) * g.astype(jnp.float32) + b.astype(jnp.float32)
    return y.astype(x.dtype)


def _modulate(x, g, shift, scale):
    return _rms_norm(x, g) * (1 + scale) + shift


def _axial_rope_tables(seq_len):
    rows = seq_len // GRID_W
    row = jnp.repeat(jnp.arange(rows, dtype=jnp.float32), GRID_W)
    col = jnp.tile(jnp.arange(GRID_W, dtype=jnp.float32), rows)
    inv_freq = ROPE_THETA ** (-jnp.arange(ROPE_FREQS, dtype=jnp.float32) / ROPE_FREQS)
    ang = jnp.stack([row[:, None] * inv_freq, col[:, None] * inv_freq], axis=1)
    return jnp.cos(ang), jnp.sin(ang)


def _apply_rope(x, cos, sin):
    xf = x.astype(jnp.float32).reshape(x.shape[:-1] + (2, 2, ROPE_FREQS))
    bshape = (cos.shape[0],) + (1,) * (x.ndim - 3) + cos.shape[1:]
    cs, sn = cos.reshape(bshape), sin.reshape(bshape)
    re, im = xf[..., 0, :], xf[..., 1, :]
    out = jnp.stack([re * cs - im * sn, im * cs + re * sn], axis=-2)
    return out.reshape(x.shape).astype(x.dtype)


def _kv_heads(kv, k_norm):
    b, l = kv.shape[:2]
    dk, dv, gk, gv = _split(kv, KV_SIZES)
    dk = dk.reshape(b, l, DIFF_HEADS, 2, HEAD_DIM)
    dv = dv.reshape(b, l, DIFF_HEADS, DIFF_V_DIM)
    gk = _rms_norm(gk.reshape(b, l, GQA_KV_HEADS, HEAD_DIM), k_norm)
    gv = gv.reshape(b, l, GQA_KV_HEADS, HEAD_DIM)
    return dk, dv, gk, gv


def _diff_attention(dq, dk, dv, lam):
    s = jnp.einsum('bqhjd,bkhjd->bhjqk', dq, dk).astype(jnp.float32) * ATTN_SCALE
    p = jax.nn.softmax(s, axis=-1)
    a = (p[:, :, 0] - lam * p[:, :, 1]).astype(dv.dtype)
    return jnp.einsum('bhqk,bkhe->bqhe', a, dv)


def _gqa_attention(gq, gk, gv):
    s = jnp.einsum('bqhgd,bkhd->bhgqk', gq, gk).astype(jnp.float32) * ATTN_SCALE
    p = jax.nn.softmax(s, axis=-1).astype(gv.dtype)
    return jnp.einsum('bhgqk,bkhd->bqhgd', p, gv)


def _sweep_query_blocks(fn, qs):
    b, l = qs[0].shape[:2]
    n = l // Q_BLOCK
    def to_blocks(q):
        return jnp.moveaxis(q.reshape((b, n, Q_BLOCK) + q.shape[2:]), 1, 0)
    out = lax.map(lambda qb: fn(*qb), tuple(to_blocks(q) for q in qs))
    def from_blocks(o):
        return jnp.moveaxis(o, 0, 1).reshape((b, l) + o.shape[3:])
    return jax.tree_util.tree_map(from_blocks, out)


def _fourier_mix(u):
    b, l = u.shape[:2]
    ug = u.astype(jnp.float32).reshape(b, l, FOURIER_GROUPS, FOURIER_GROUP_DIM)
    y = jnp.fft.fft2(ug, axes=(1, 3), norm='ortho').real
    return y.reshape(b, l, FOURIER_WIDTH).astype(u.dtype)


def _conformer_conv(u, dw, dw_bias, ln_g, ln_b):
    a, g = jnp.split(u, 2, axis=-1)
    v = a * jax.nn.sigmoid(g)
    y = lax.conv_general_dilated(
        v, dw[:, None, :], window_strides=(1,),
        padding=[(CONV_KERNEL // 2, CONV_KERNEL // 2)],
        dimension_numbers=('NWC', 'WIO', 'NWC'),
        feature_group_count=CONV_WIDTH) + dw_bias
    return jax.nn.silu(_layer_norm(y, ln_g, ln_b))


def _token_mixer(h, kv_ctx, rope, lp, lam, lam_init):
    b, l = h.shape[:2]
    w_in = lp['w_in']
    if rope is None:
        rest = h @ w_in[:, KV_COLS:]
        dk, dv, gk, gv = kv_ctx
    else:
        proj = h @ w_in
        dk, dv, gk, gv = _kv_heads(proj[..., :KV_COLS], lp['k_norm'])
        rest = proj[..., KV_COLS:]
    u_f, dq, gq, u_c, gate_logits = _split(rest, REST_SIZES)
    dq = dq.reshape(b, l, DIFF_HEADS, 2, HEAD_DIM)
    gq = _rms_norm(gq.reshape(b, l, GQA_KV_HEADS, GQA_GROUP, HEAD_DIM), lp['q_norm'])
    if rope is None:
        o_d = _diff_attention(dq, dk, dv, lam)
        o_g = _gqa_attention(gq, gk, gv)
    else:
        cos, sin = rope
        dq, dk = _apply_rope(dq, cos, sin), _apply_rope(dk, cos, sin)
        gq, gk = _apply_rope(gq, cos, sin), _apply_rope(gk, cos, sin)
        cdk, cdv, cgk, cgv = kv_ctx
        dk_all = jnp.concatenate([dk, cdk], axis=1)
        dv_all = jnp.concatenate([dv, cdv], axis=1)
        gk_all = jnp.concatenate([gk, cgk], axis=1)
        gv_all = jnp.concatenate([gv, cgv], axis=1)
        o_d, o_g = _sweep_query_blocks(
            lambda qd, qg: (_diff_attention(qd, dk_all, dv_all, lam), _gqa_attention(qg, gk_all, gv_all)),
            (dq, gq))
    o_d = (_rms_norm(o_d, lp['subln']) * (1.0 - lam_init)).reshape(b, l, DIFF_V_WIDTH)
    o_g = o_g.reshape(b, l, GQA_Q_WIDTH)
    y_f = _fourier_mix(u_f)
    y_c = _conformer_conv(u_c, lp['conv_dw'], lp['conv_dw_bias'], lp['conv_ln_g'], lp['conv_ln_b'])
    g_f, g_d, g_g, g_c = jnp.split(jax.nn.sigmoid(gate_logits), N_BRANCHES, axis=-1)
    merged = (g_f * (y_f @ lp['w_br_fourier']) + g_d * (o_d @ lp['w_br_diff'])
              + g_g * (o_g @ lp['w_br_gqa']) + g_c * (y_c @ lp['w_br_conv']))
    return merged @ lp['w_out']


def _sq_relu_mlp(h, w1, w2):
    return jnp.square(jax.nn.relu(h @ w1)) @ w2


def setup_inputs(seed: int = 0) -> dict:
    key = jax.random.key(seed)
    ks = jax.random.split(key, 32)
    f32 = jnp.float32
    def nrm(k, shape, scale):
        return jax.random.normal(k, shape, f32) * scale
    def gain(k, shape):
        return 1.0 + 0.02 * jax.random.normal(k, shape, f32)
    D = D_MODEL
    return {
        "x": nrm(ks[0], (BATCH, SEQ, D), 1.0),
        "c": nrm(ks[1], (BATCH, D), 1.0),
        "ctx": nrm(ks[2], (BATCH, CTX_LEN, D), 1.0),
        "c_ctx": nrm(ks[3], (D,), 1.0),
        "w_mod": nrm(ks[4], (DEPTH, D, N_MOD * D), 0.5 * D ** -0.5),
        "b_mod": nrm(ks[5], (DEPTH, N_MOD * D), 0.01),
        "g_pre_mix": gain(ks[6], (DEPTH, D)),
        "g_post_mix": gain(ks[7], (DEPTH, D)),
        "g_pre_mlp": gain(ks[8], (DEPTH, D)),
        "g_post_mlp": gain(ks[9], (DEPTH, D)),
        "w_in": nrm(ks[10], (DEPTH, D, IN_COLS), D ** -0.5),
        "q_norm": gain(ks[11], (DEPTH, HEAD_DIM)),
        "k_norm": gain(ks[12], (DEPTH, HEAD_DIM)),
        "diff_lambda": nrm(ks[13], (DEPTH, 4, HEAD_DIM), 0.1),
        "diff_subln": gain(ks[14], (DEPTH, DIFF_V_DIM)),
        "conv_dw": nrm(ks[15], (DEPTH, CONV_KERNEL, CONV_WIDTH), CONV_KERNEL ** -0.5),
        "conv_dw_bias": nrm(ks[16], (DEPTH, CONV_WIDTH), 0.01),
        "conv_ln_g": gain(ks[17], (DEPTH, CONV_WIDTH)),
        "conv_ln_b": nrm(ks[18], (DEPTH, CONV_WIDTH), 0.01),
        "w_br_fourier": nrm(ks[19], (DEPTH, FOURIER_WIDTH, D), FOURIER_WIDTH ** -0.5),
        "w_br_diff": nrm(ks[20], (DEPTH, DIFF_V_WIDTH, D), DIFF_V_WIDTH ** -0.5),
        "w_br_gqa": nrm(ks[21], (DEPTH, GQA_Q_WIDTH, D), GQA_Q_WIDTH ** -0.5),
        "w_br_conv": nrm(ks[22], (DEPTH, CONV_WIDTH, D), CONV_WIDTH ** -0.5),
        "w_out": nrm(ks[23], (DEPTH, D, D), D ** -0.5),
        "w_ff1": nrm(ks[24], (DEPTH, D, D_FF), D ** -0.5),
        "w_ff2": nrm(ks[25], (DEPTH, D_FF, D), D_FF ** -0.5),
    }


def reference(x, c, ctx, c_ctx, w_mod, b_mod, g_pre_mix, g_post_mix, g_pre_mlp, g_post_mlp,
              w_in, q_norm, k_norm, diff_lambda, diff_subln, conv_dw, conv_dw_bias, conv_ln_g, conv_ln_b,
              w_br_fourier, w_br_diff, w_br_gqa, w_br_conv, w_out, w_ff1, w_ff2):
    rope = _axial_rope_tables(x.shape[1])
    xc = ctx
    for l in range(DEPTH):
        lp = {
            'w_in': w_in[l], 'q_norm': q_norm[l], 'k_norm': k_norm[l], 'subln': diff_subln[l],
            'conv_dw': conv_dw[l], 'conv_dw_bias': conv_dw_bias[l],
            'conv_ln_g': conv_ln_g[l], 'conv_ln_b': conv_ln_b[l],
            'w_br_fourier': w_br_fourier[l], 'w_br_diff': w_br_diff[l],
            'w_br_gqa': w_br_gqa[l], 'w_br_conv': w_br_conv[l], 'w_out': w_out[l],
        }
        lam_init = 0.8 - 0.6 * math.exp(-0.3 * l)
        lv = diff_lambda[l].astype(jnp.float32)
        lam = jnp.exp(jnp.sum(lv[0] * lv[1])) - jnp.exp(jnp.sum(lv[2] * lv[3])) + lam_init

        mod_x = (jax.nn.silu(c) @ w_mod[l] + b_mod[l])[:, None, :]
        mod_c = (jax.nn.silu(c_ctx) @ w_mod[l] + b_mod[l])[None, None, :]
        sx1, cx1, gx1, sx2, cx2, gx2 = jnp.split(mod_x, N_MOD, axis=-1)
        sc1, cc1, gc1, sc2, cc2, gc2 = jnp.split(mod_c, N_MOD, axis=-1)

        hc = _modulate(xc, g_pre_mix[l], sc1, cc1)
        kv_ctx = _kv_heads(hc @ w_in[l][:, :KV_COLS], k_norm[l])

        h = _modulate(x, g_pre_mix[l], sx1, cx1)
        x = x + gx1 * _rms_norm(_token_mixer(h, kv_ctx, rope, lp, lam, lam_init), g_post_mix[l])
        hm = _modulate(x, g_pre_mlp[l], sx2, cx2)
        x = x + gx2 * _rms_norm(_sq_relu_mlp(hm, w_ff1[l], w_ff2[l]), g_post_mlp[l])

        if l < DEPTH - 1:
            xc = xc + gc1 * _rms_norm(_token_mixer(hc, kv_ctx, None, lp, lam, lam_init), g_post_mix[l])
            hcm = _modulate(xc, g_pre_mlp[l], sc2, cc2)
            xc = xc + gc2 * _rms_norm(_sq_relu_mlp(hcm, w_ff1[l], w_ff2[l]), g_post_mlp[l])
    return x
```

```python
import functools
import math

import jax
import jax.numpy as jnp
from jax import lax
from jax.experimental import pallas as pl
from jax.experimental.pallas import tpu as pltpu

F32 = jnp.float32
BF16 = jnp.bfloat16

GRID_W = 64
HEAD_DIM = 64
ROPE_FREQS = HEAD_DIM // 4
ROPE_THETA = 10000.0
EPS = 1e-6
ATTN_SCALE = HEAD_DIM ** -0.5
FOURIER_GROUP_DIM = 64
CONV_KERNEL = 31
CONV_HALO = 16
N_BRANCHES = 4
N_MOD = 6
MOD_ROWS = 8

V7X_VMEM_LIMIT_BYTES = 56 * 1024 * 1024


def _dims(d_model):
    fourier = 3 * d_model // 8
    diff_heads = d_model // 256
    diff_qk = diff_heads * 2 * HEAD_DIM
    diff_v = diff_heads * 2 * HEAD_DIM
    gqa_q = (d_model // 128) * HEAD_DIM
    gqa_kv = (d_model // 128 // 4) * HEAD_DIM
    conv = 3 * d_model // 8
    return dict(fourier=fourier, diff_heads=diff_heads, diff_qk=diff_qk, diff_v=diff_v,
                gqa_q=gqa_q, gqa_kv=gqa_kv, conv=conv)


def _params(semantics):
    return pltpu.CompilerParams(dimension_semantics=semantics, vmem_limit_bytes=V7X_VMEM_LIMIT_BYTES)


def _mod_kernel(c_ref, w_ref, b_ref, o_ref):
    c = c_ref[...]
    a = c * jax.nn.sigmoid(c)
    o_ref[...] = jnp.dot(a, w_ref[...], preferred_element_type=F32,
                         precision=lax.Precision.HIGHEST) + b_ref[...]


def _modulation(c_rows, w_mod, b_mod):
    depth, d, n = w_mod.shape
    tn = 768
    return pl.pallas_call(
        _mod_kernel,
        grid=(depth, n // tn),
        in_specs=[pl.BlockSpec((MOD_ROWS, d), lambda l, j: (0, 0)),
                  pl.BlockSpec((None, d, tn), lambda l, j: (l, 0, j)),
                  pl.BlockSpec((None, 1, tn), lambda l, j: (l, 0, j))],
        out_specs=pl.BlockSpec((None, MOD_ROWS, tn), lambda l, j: (l, 0, j)),
        out_shape=jax.ShapeDtypeStruct((depth, MOD_ROWS, n), F32),
        compiler_params=_params(("arbitrary", "arbitrary")),
        name="modulation",
    )(c_rows, w_mod, b_mod.reshape(depth, 1, n))


def _modulated(x, g, shift, scale):
    h = x * lax.rsqrt(jnp.mean(x * x, axis=-1, keepdims=True) + EPS) * g
    return h * (1.0 + scale) + shift


def _rms(x, g):
    return x * lax.rsqrt(jnp.mean(x * x, axis=-1, keepdims=True) + EPS) * g


def _swap_halves(t):
    w = t.shape[-1]
    lane = lax.broadcasted_iota(jnp.int32, t.shape, 1)
    from_right = pltpu.roll(t, w - ROPE_FREQS, 1)
    from_left = pltpu.roll(t, ROPE_FREQS, 1)
    return jnp.where((lane % (2 * ROPE_FREQS)) < ROPE_FREQS, from_right, from_left)


def _tile_lanes(t, width):
    reps = width // t.shape[-1]
    return t if reps == 1 else jnp.concatenate([t] * reps, axis=-1)


def _head_norm(t, ones_bd, gain):
    w = t.shape[-1]
    t2 = t * t
    hi = t2.astype(BF16)
    lo = (t2 - hi.astype(F32)).astype(BF16)
    e = ones_bd[:w, :w]
    ss = (jnp.dot(hi, e, preferred_element_type=F32) + jnp.dot(lo, e, preferred_element_type=F32))
    return t * lax.rsqrt(ss * (1.0 / HEAD_DIM) + EPS) * gain


def _inproj_kernel(x_ref, shift_ref, scale_ref, g_ref, w_ref, ones_ref, cos_ref, sin_ref, qn_ref, kn_ref,
                   dft_ref, *out_refs, dims, use_rope, kv_only):
    nq, nkv, nv, nf, nc = dims["diff_qk"], dims["gqa_kv"], dims["diff_v"], dims["fourier"], dims["conv"]
    ngq = dims["gqa_q"]
    h = _modulated(x_ref[0], g_ref[...], shift_ref[0], scale_ref[0]).astype(BF16)

    def proj(start, width):
        return jnp.dot(h, w_ref[:, start:start + width], preferred_element_type=F32)

    def rope(t):
        if not use_rope:
            return t
        w = t.shape[-1]
        return t * _tile_lanes(cos_ref[...], w) + _swap_halves(t) * _tile_lanes(sin_ref[...], w)

    ones_bd = ones_ref[...]
    off = 0
    dq = proj(off, nq); off += nq
    gq = proj(off, ngq); off += ngq
    dk = proj(off, nq); off += nq
    gk = proj(off, nkv); off += nkv
    dv = proj(off, nv); off += nv
    gv = proj(off, nkv); off += nkv
    if kv_only:
        kd_ref, kg_ref, vd_ref, vg_ref = out_refs
    else:
        qd_ref, qg_ref, kd_ref, kg_ref, vd_ref, vg_ref, xf_ref, vc_ref = out_refs
        qd_ref[0] = (rope(dq) * ATTN_SCALE).astype(BF16)
        qg_ref[0] = (rope(_head_norm(gq, ones_bd, qn_ref[...])) * ATTN_SCALE).astype(BF16)
        uf = proj(off, nf).astype(BF16); off += nf
        xf_ref[0] = jnp.dot(uf, dft_ref[...], preferred_element_type=F32).astype(BF16)
        glu = proj(off, 2 * nc)
        vc_ref[0] = glu[:, :nc] * jax.nn.sigmoid(glu[:, nc:])
    kd_ref[0] = rope(dk).astype(BF16)
    kg_ref[0] = rope(_head_norm(gk, ones_bd, kn_ref[...])).astype(BF16)
    vd_ref[0] = dv.astype(BF16)
    vg_ref[0] = gv.astype(BF16)


def _inproj(x, mod, g_pre, w_a, ones_bd, cos_t, sin_t, qn, kn, dft_ch, *, dims, use_rope, kv_only, tm):
    b, l, d = x.shape
    tm = min(tm, l)
    nq, nkv, nv, nf, nc, ngq = (dims["diff_qk"], dims["gqa_kv"], dims["diff_v"], dims["fourier"],
                                dims["conv"], dims["gqa_q"])
    tok = lambda w: pl.BlockSpec((1, tm, w), lambda bi, i: (bi, i, 0))
    const = lambda shape: pl.BlockSpec(shape, lambda bi, i: (0,) * len(shape))
    kv_out = [((b, l, nq), BF16), ((b, l, nkv), BF16), ((b, l, nv), BF16), ((b, l, nkv), BF16)]
    if kv_only:
        outs = kv_out
    else:
        outs = ([((b, l, nq), BF16), ((b, l, ngq), BF16)] + kv_out
                + [((b, l, 2 * nf), BF16), ((b, l, nc), F32)])
    return pl.pallas_call(
        functools.partial(_inproj_kernel, dims=dims, use_rope=use_rope, kv_only=kv_only),
        grid=(b, l // tm),
        in_specs=[tok(d),
                  pl.BlockSpec((1, 1, d), lambda bi, i: (bi, 0, 0)),
                  pl.BlockSpec((1, 1, d), lambda bi, i: (bi, 0, 1)),
                  const((1, d)),
                  const(w_a.shape),
                  const(ones_bd.shape),
                  pl.BlockSpec((tm, 2 * HEAD_DIM), lambda bi, i: (i, 0)),
                  pl.BlockSpec((tm, 2 * HEAD_DIM), lambda bi, i: (i, 0)),
                  const(qn.shape), const(kn.shape), const(dft_ch.shape)],
        out_specs=[tok(s[-1]) for s, _ in outs],
        out_shape=[jax.ShapeDtypeStruct(s, dt) for s, dt in outs],
        compiler_params=_params(("arbitrary", "arbitrary")),
        name="inproj_kv" if kv_only else "inproj",
    )(x, mod, mod, g_pre, w_a, ones_bd, cos_t, sin_t, qn, kn, dft_ch)


def _softmax_pv(q, k_t, v):
    s = jnp.dot(q, k_t, preferred_element_type=F32)
    m = jnp.max(s, axis=-1, keepdims=True)
    e = jnp.exp(s - m)
    denom = jnp.sum(e, axis=-1, keepdims=True)
    pv = jnp.dot(e.astype(BF16), v, preferred_element_type=F32)
    return pv / denom


def _attn_kernel(qd_ref, qg_ref, kdt_ref, vd_ref, kgt_ref, vg_ref, lam_ref, subln_ref, od_ref, og_ref,
                 *, dims, lam_init):
    lv = lam_ref[...]
    lam = (jnp.exp(jnp.sum(lv[0:1] * lv[1:2], axis=-1, keepdims=True))
           - jnp.exp(jnp.sum(lv[2:3] * lv[3:4], axis=-1, keepdims=True)) + lam_init)
    dh = HEAD_DIM
    dv = 2 * HEAD_DIM
    for hd in range(dims["diff_heads"]):
        maps = []
        for j in range(2):
            c = (2 * hd + j) * dh
            maps.append(_softmax_pv(qd_ref[0, :, c:c + dh], kdt_ref[0, c:c + dh, :],
                                    vd_ref[0, :, hd * dv:(hd + 1) * dv]))
        o = maps[0] - lam * maps[1]
        od_ref[0, :, hd * dv:(hd + 1) * dv] = (_rms(o, subln_ref[...]) * (1.0 - lam_init)).astype(BF16)
    group = dims["gqa_q"] // dims["gqa_kv"]
    heads = []
    for hq in range(dims["gqa_q"] // dh):
        kv = hq // group
        heads.append(_softmax_pv(qg_ref[0, :, hq * dh:(hq + 1) * dh], kgt_ref[0, kv * dh:(kv + 1) * dh, :],
                                 vg_ref[0, :, kv * dh:(kv + 1) * dh]))
    og_ref[0] = jnp.concatenate(heads, axis=-1).astype(BF16)


def _attention(qd, qg, kd_t, vd, kg_t, vg, lam_p, subln, *, dims, lam_init, tq):
    b, l, nq = qd.shape
    lk = vd.shape[1]
    tq = min(tq, l)
    tok = lambda w: pl.BlockSpec((1, tq, w), lambda bi, i: (bi, i, 0))
    per_batch = lambda a: pl.BlockSpec((1,) + a.shape[1:], lambda bi, i: (bi, 0, 0))
    const = lambda a: pl.BlockSpec(a.shape, lambda bi, i: (0,) * a.ndim)
    return pl.pallas_call(
        functools.partial(_attn_kernel, dims=dims, lam_init=lam_init),
        grid=(b, l // tq),
        in_specs=[tok(nq), tok(qg.shape[-1]), per_batch(kd_t), per_batch(vd), per_batch(kg_t), per_batch(vg),
                  const(lam_p), const(subln)],
        out_specs=[tok(dims["diff_v"]), tok(dims["gqa_q"])],
        out_shape=[jax.ShapeDtypeStruct((b, l, dims["diff_v"]), BF16),
                   jax.ShapeDtypeStruct((b, l, dims["gqa_q"]), BF16)],
        compiler_params=_params(("arbitrary", "arbitrary")),
        name="attention",
    )(qd, qg, kd_t, vd, kg_t, vg, lam_p, subln)


def _fourier_kernel(c_ref, s_ref, xf_ref, o_ref, *, nf, out_scale):
    y = (jnp.dot(c_ref[...], xf_ref[0, :, :nf], preferred_element_type=F32)
         - jnp.dot(s_ref[...], xf_ref[0, :, nf:], preferred_element_type=F32))
    o_ref[0] = (y * out_scale).astype(BF16)


def _fourier(dft_c, dft_s, xf, *, tr):
    b, l, nf2 = xf.shape
    nf = nf2 // 2
    tr = min(tr, l)
    return pl.pallas_call(
        functools.partial(_fourier_kernel, nf=nf, out_scale=float(l) ** -0.5),
        grid=(b, l // tr),
        in_specs=[pl.BlockSpec((tr, l), lambda bi, i: (i, 0)),
                  pl.BlockSpec((tr, l), lambda bi, i: (i, 0)),
                  pl.BlockSpec((1, l, nf2), lambda bi, i: (bi, 0, 0))],
        out_specs=pl.BlockSpec((1, tr, nf), lambda bi, i: (bi, i, 0)),
        out_shape=jax.ShapeDtypeStruct((b, l, nf), BF16),
        compiler_params=_params(("arbitrary", "arbitrary")),
        name="fourier",
    )(dft_c, dft_s, xf)


def _conv_kernel(win_ref, w_ref, b_ref, g_ref, beta_ref, o_ref, *, rows, chunk):
    first = CONV_HALO - CONV_KERNEL // 2
    for r0 in range(0, rows, chunk):
        acc = jnp.zeros((chunk, w_ref.shape[-1]), F32) + b_ref[...]
        for t in range(CONV_KERNEL):
            acc = acc + w_ref[t:t + 1, :] * win_ref[0, 0, r0 + t + first:r0 + t + first + chunk, :]
        mu = jnp.mean(acc, axis=-1, keepdims=True)
        cen = acc - mu
        var = jnp.mean(cen * cen, axis=-1, keepdims=True)
        y = cen * lax.rsqrt(var + EPS) * g_ref[...] + beta_ref[...]
        o_ref[0, r0:r0 + chunk, :] = (y * jax.nn.sigmoid(y)).astype(BF16)


def _conv(vc, w, bias, ln_g, ln_b, *, rows):
    b, l, c = vc.shape
    rows = min(rows, l)
    nt = l // rows
    padded = jnp.pad(vc, ((0, 0), (CONV_HALO, CONV_HALO), (0, 0)))
    wins = jnp.stack([padded[:, i * rows:i * rows + rows + 2 * CONV_HALO] for i in range(nt)], axis=1)
    const = lambda a: pl.BlockSpec(a.shape, lambda bi, i: (0,) * a.ndim)
    return pl.pallas_call(
        functools.partial(_conv_kernel, rows=rows, chunk=64),
        grid=(b, nt),
        in_specs=[pl.BlockSpec((1, 1, rows + 2 * CONV_HALO, c), lambda bi, i: (bi, i, 0, 0)),
                  const(w), const(bias), const(ln_g), const(ln_b)],
        out_specs=pl.BlockSpec((1, rows, c), lambda bi, i: (bi, i, 0)),
        out_shape=jax.ShapeDtypeStruct((b, l, c), BF16),
        compiler_params=_params(("arbitrary", "arbitrary")),
        name="conv",
    )(wins, w, bias, ln_g, ln_b)


def _merge_kernel(x_ref, shift_ref, scale_ref, gate_ref, g_ref, wgate_ref, yf_ref, od_ref, og_ref, yc_ref,
                  wf_ref, wd_ref, wg_ref, wc_ref, wout_ref, gpost_ref, o_ref):
    x = x_ref[0]
    d = x.shape[-1]
    h = _modulated(x, g_ref[...], shift_ref[0], scale_ref[0]).astype(BF16)
    merged = None
    for i, (y_ref, w_ref) in enumerate(((yf_ref, wf_ref), (od_ref, wd_ref), (og_ref, wg_ref), (yc_ref, wc_ref))):
        gate = jax.nn.sigmoid(jnp.dot(h, wgate_ref[:, i * d:(i + 1) * d], preferred_element_type=F32))
        term = gate * jnp.dot(y_ref[0], w_ref[...], preferred_element_type=F32)
        merged = term if merged is None else merged + term
    mix = jnp.dot(merged.astype(BF16), wout_ref[...], preferred_element_type=F32)
    o_ref[0] = x + gate_ref[0] * _rms(mix, gpost_ref[...])


def _merge(x, mod, g_pre, w_gate, yf, od, og, yc, wf, wd, wg, wc, wout, g_post, *, tm):
    b, l, d = x.shape
    tm = min(tm, l)
    tok = lambda w: pl.BlockSpec((1, tm, w), lambda bi, i: (bi, i, 0))
    modc = lambda k: pl.BlockSpec((1, 1, d), lambda bi, i: (bi, 0, k))
    const = lambda a: pl.BlockSpec(a.shape, lambda bi, i: (0,) * a.ndim)
    return pl.pallas_call(
        _merge_kernel,
        grid=(b, l // tm),
        in_specs=[tok(d), modc(0), modc(1), modc(2), const(g_pre), const(w_gate),
                  tok(yf.shape[-1]), tok(od.shape[-1]), tok(og.shape[-1]), tok(yc.shape[-1]),
                  const(wf), const(wd), const(wg), const(wc), const(wout), const(g_post)],
        out_specs=tok(d),
        out_shape=jax.ShapeDtypeStruct((b, l, d), F32),
        compiler_params=_params(("arbitrary", "arbitrary")),
        name="merge",
    )(x, mod, mod, mod, g_pre, w_gate, yf, od, og, yc, wf, wd, wg, wc, wout, g_post)


def _mlp_kernel(x_ref, shift_ref, scale_ref, gate_ref, g_ref, w1_ref, w2_ref, gpost_ref, o_ref, *, ff_chunk):
    x = x_ref[0]
    h = _modulated(x, g_ref[...], shift_ref[0], scale_ref[0]).astype(BF16)
    acc = None
    for c0 in range(0, w1_ref.shape[-1], ff_chunk):
        u = jnp.maximum(jnp.dot(h, w1_ref[:, c0:c0 + ff_chunk], preferred_element_type=F32), 0.0)
        part = jnp.dot((u * u).astype(BF16), w2_ref[c0:c0 + ff_chunk, :], preferred_element_type=F32)
        acc = part if acc is None else acc + part
    o_ref[0] = x + gate_ref[0] * _rms(acc, gpost_ref[...])


def _mlp(x, mod, g_pre, w1, w2, g_post, *, tm):
    b, l, d = x.shape
    tm = min(tm, l)
    tok = lambda w: pl.BlockSpec((1, tm, w), lambda bi, i: (bi, i, 0))
    modc = lambda k: pl.BlockSpec((1, 1, d), lambda bi, i: (bi, 0, k))
    const = lambda a: pl.BlockSpec(a.shape, lambda bi, i: (0,) * a.ndim)
    return pl.pallas_call(
        functools.partial(_mlp_kernel, ff_chunk=1024),
        grid=(b, l // tm),
        in_specs=[tok(d), modc(3), modc(4), modc(5), const(g_pre), const(w1), const(w2), const(g_post)],
        out_specs=tok(d),
        out_shape=jax.ShapeDtypeStruct((b, l, d), F32),
        compiler_params=_params(("arbitrary", "arbitrary")),
        name="mlp",
    )(x, mod, mod, mod, g_pre, w1, w2, g_post)


def _rope_tables(seq_len):
    rows = seq_len // GRID_W
    row = jnp.repeat(jnp.arange(rows, dtype=F32), GRID_W)
    col = jnp.tile(jnp.arange(GRID_W, dtype=F32), rows)
    inv_freq = ROPE_THETA ** (-jnp.arange(ROPE_FREQS, dtype=F32) / ROPE_FREQS)
    ang_r, ang_c = row[:, None] * inv_freq, col[:, None] * inv_freq
    cos64 = jnp.concatenate([jnp.cos(ang_r)] * 2 + [jnp.cos(ang_c)] * 2, axis=-1)
    sin64 = jnp.concatenate([-jnp.sin(ang_r), jnp.sin(ang_r), -jnp.sin(ang_c), jnp.sin(ang_c)], axis=-1)
    return jnp.concatenate([cos64] * 2, axis=-1), jnp.concatenate([sin64] * 2, axis=-1)


def _seq_dft(l):
    f = min(64, l)
    a_n = l // f
    k = jnp.arange(l, dtype=jnp.int32)[:, None]
    th1 = ((k * jnp.arange(a_n, dtype=jnp.int32)[None, :]) % a_n).astype(F32) * (2.0 * math.pi / a_n)
    th2 = ((k * jnp.arange(f, dtype=jnp.int32)[None, :]) % l).astype(F32) * (2.0 * math.pi / l)
    c1, s1, c2, s2 = jnp.cos(th1)[:, :, None], jnp.sin(th1)[:, :, None], jnp.cos(th2)[:, None, :], jnp.sin(th2)[:, None, :]
    c = (c1 * c2 - s1 * s2).reshape(l, l)
    s = (s1 * c2 + c1 * s2).reshape(l, l)
    return c.astype(BF16), s.astype(BF16)


def _channel_dft(n_groups):
    g = FOURIER_GROUP_DIM
    idx = jnp.arange(g, dtype=jnp.int32)
    th = ((idx[:, None] * idx[None, :]) % g).astype(F32) * (2.0 * math.pi / g)
    eye = jnp.eye(n_groups, dtype=F32)
    scale = float(g) ** -0.5
    return jnp.concatenate([jnp.kron(eye, jnp.cos(th) * scale), jnp.kron(eye, jnp.sin(th) * scale)],
                           axis=-1).astype(BF16)


def _mixer(x, mod, lw, kv_ctx, rope_tabs, dft_seq, consts, *, dims, lam_init, use_rope):
    l = x.shape[1]
    qd, qg, kd, kg, vd, vg, xf, vc = _inproj(
        x, mod, lw["g_pre_mix"], lw["w_a"], consts["ones_bd"], rope_tabs[0], rope_tabs[1], lw["qn"], lw["kn"],
        consts["dft_ch"], dims=dims, use_rope=use_rope, kv_only=False, tm=512)
    if kv_ctx is not None:
        kd, kg, vd, vg = (jnp.concatenate([a, c], axis=1) for a, c in zip((kd, kg, vd, vg), kv_ctx))
    od, og = _attention(qd, qg, jnp.swapaxes(kd, 1, 2), vd, jnp.swapaxes(kg, 1, 2), vg, lw["lam_p"], lw["subln"],
                        dims=dims, lam_init=lam_init, tq=256)
    yf = _fourier(dft_seq[0], dft_seq[1], xf, tr=512)
    yc = _conv(vc, lw["conv_w"], lw["conv_b"], lw["conv_ln_g"], lw["conv_ln_b"], rows=256)
    return _merge(x, mod, lw["g_pre_mix"], lw["w_gate"], yf, od, og, yc, lw["w_br_f"], lw["w_br_d"], lw["w_br_g"],
                  lw["w_br_c"], lw["w_out"], lw["g_post_mix"], tm=256)


def kernel(x, c, ctx, c_ctx, w_mod, b_mod, g_pre_mix, g_post_mix, g_pre_mlp, g_post_mlp, w_in, q_norm, k_norm,
           diff_lambda, diff_subln, conv_dw, conv_dw_bias, conv_ln_g, conv_ln_b, w_br_fourier, w_br_diff,
           w_br_gqa, w_br_conv, w_out, w_ff1, w_ff2):
    b, l, d = x.shape
    lc = ctx.shape[1]
    depth = w_in.shape[0]
    dims = _dims(d)
    assert b + 1 <= MOD_ROWS and l % GRID_W == 0

    rope_x = _rope_tables(l)
    rope_none = (jnp.ones((lc, 2 * HEAD_DIM), F32), jnp.zeros((lc, 2 * HEAD_DIM), F32))
    dft_x, dft_c = _seq_dft(l), _seq_dft(lc)
    n_groups = dims["fourier"] // FOURIER_GROUP_DIM
    heads = jnp.arange(dims["diff_qk"], dtype=jnp.int32) // HEAD_DIM
    consts = dict(ones_bd=(heads[:, None] == heads[None, :]).astype(BF16), dft_ch=_channel_dft(n_groups))

    c_rows = jnp.zeros((MOD_ROWS, d), F32).at[:b].set(c).at[b].set(c_ctx)
    mod_all = _modulation(c_rows, w_mod, b_mod)

    kv_sizes = (dims["diff_qk"], dims["diff_v"], dims["gqa_kv"], dims["gqa_kv"])
    rest_sizes = (dims["fourier"], dims["diff_qk"], dims["gqa_q"], 2 * dims["conv"], N_BRANCHES * d)
    offs, o = [], 0
    for s in kv_sizes + rest_sizes:
        offs.append((o, o + s)); o += s
    c_dk, c_dv, c_gk, c_gv, c_f, c_dq, c_gq, c_glu, c_gate = offs

    xc = ctx
    for li in range(depth):
        lam_init = 0.8 - 0.6 * math.exp(-0.3 * li)
        wl = w_in[li]
        cols = lambda r: wl[:, r[0]:r[1]]
        row = lambda a: a[li].reshape(1, -1)
        lw = dict(
            w_a=jnp.concatenate([cols(r) for r in (c_dq, c_gq, c_dk, c_gk, c_dv, c_gv, c_f, c_glu)],
                                axis=-1).astype(BF16),
            w_gate=cols(c_gate).astype(BF16),
            g_pre_mix=row(g_pre_mix), g_post_mix=row(g_post_mix), g_pre_mlp=row(g_pre_mlp),
            g_post_mlp=row(g_post_mlp),
            qn=jnp.tile(q_norm[li], dims["gqa_q"] // HEAD_DIM).reshape(1, -1),
            kn=jnp.tile(k_norm[li], dims["gqa_kv"] // HEAD_DIM).reshape(1, -1),
            lam_p=diff_lambda[li], subln=row(diff_subln),
            conv_w=conv_dw[li], conv_b=row(conv_dw_bias), conv_ln_g=row(conv_ln_g), conv_ln_b=row(conv_ln_b),
            w_br_f=w_br_fourier[li].astype(BF16), w_br_d=w_br_diff[li].astype(BF16),
            w_br_g=w_br_gqa[li].astype(BF16), w_br_c=w_br_conv[li].astype(BF16), w_out=w_out[li].astype(BF16),
            w_ff1=w_ff1[li].astype(BF16), w_ff2=w_ff2[li].astype(BF16))
        mod_x = mod_all[li, :b].reshape(b, 1, N_MOD * d)
        mod_c = jnp.broadcast_to(mod_all[li, b].reshape(1, 1, N_MOD * d), (b, 1, N_MOD * d))

        last = li == depth - 1
        if last:
            kv_ctx = _inproj(xc, mod_c, lw["g_pre_mix"], lw["w_a"], consts["ones_bd"], rope_none[0], rope_none[1],
                             lw["qn"], lw["kn"], consts["dft_ch"], dims=dims, use_rope=False, kv_only=True, tm=256)
        else:
            qd, qg, kd, kg, vd, vg, xf, vc = _inproj(
                xc, mod_c, lw["g_pre_mix"], lw["w_a"], consts["ones_bd"], rope_none[0], rope_none[1], lw["qn"],
                lw["kn"], consts["dft_ch"], dims=dims, use_rope=False, kv_only=False, tm=256)
            kv_ctx = (kd, kg, vd, vg)

        x = _mixer(x, mod_x, lw, kv_ctx, rope_x, dft_x, consts, dims=dims, lam_init=lam_init, use_rope=True)
        x = _mlp(x, mod_x, lw["g_pre_mlp"], lw["w_ff1"], lw["w_ff2"], lw["g_post_mlp"], tm=512)

        if not last:
            od, og = _attention(qd, qg, jnp.swapaxes(kd, 1, 2), vd, jnp.swapaxes(kg, 1, 2), vg, lw["lam_p"],
                                lw["subln"], dims=dims, lam_init=lam_init, tq=256)
            yf = _fourier(dft_c[0], dft_c[1], xf, tr=256)
            yc = _conv(vc, lw["conv_w"], lw["conv_b"], lw["conv_ln_g"], lw["conv_ln_b"], rows=256)
            xc = _merge(xc, mod_c, lw["g_pre_mix"], lw["w_gate"], yf, od, og, yc, lw["w_br_f"], lw["w_br_d"],
                        lw["w_br_g"], lw["w_br_c"], lw["w_out"], lw["g_post_mix"], tm=256)
            xc = _mlp(xc, mod_c, lw["g_pre_mlp"], lw["w_ff1"], lw["w_ff2"], lw["g_post_mlp"], tm=256)
    return x
```

```python
import functools
import math

import jax
import jax.numpy as jnp
from jax import lax
from jax.experimental import pallas as pl
from jax.experimental.pallas import tpu as pltpu

F32 = jnp.float32
BF16 = jnp.bfloat16

GRID_W = 64
HEAD_DIM = 64
ROPE_FREQS = HEAD_DIM // 4
ROPE_THETA = 10000.0
EPS = 1e-6
ATTN_SCALE = HEAD_DIM ** -0.5
Q_PRESCALE = ATTN_SCALE * math.log2(math.e)
FOURIER_GROUP_DIM = 64
CONV_KERNEL = 31
CONV_HALO = 16
N_BRANCHES = 4
N_MOD = 6
MOD_ROWS = 8

V7X_VMEM_LIMIT_BYTES = 56 * 1024 * 1024
V7X_MXU_COLUMNS = 256
SUBLANES = 8


def _dims(d_model):
    fourier = 3 * d_model // 8
    diff_heads = d_model // 256
    diff_qk = diff_heads * 2 * HEAD_DIM
    diff_v = diff_heads * 2 * HEAD_DIM
    gqa_q = (d_model // 128) * HEAD_DIM
    gqa_kv = (d_model // 128 // 4) * HEAD_DIM
    conv = 3 * d_model // 8
    return dict(fourier=fourier, diff_heads=diff_heads, diff_qk=diff_qk, diff_v=diff_v,
                gqa_q=gqa_q, gqa_kv=gqa_kv, conv=conv)


def _params(semantics):
    return pltpu.CompilerParams(dimension_semantics=semantics, vmem_limit_bytes=V7X_VMEM_LIMIT_BYTES)


def _mod_kernel(c_ref, w_ref, b_ref, o_ref):
    c = c_ref[...]
    a = c * jax.nn.sigmoid(c)
    o_ref[...] = jnp.dot(a, w_ref[...], preferred_element_type=F32,
                         precision=lax.Precision.HIGHEST) + b_ref[...]


def _modulation(c_rows, w_mod, b_mod):
    depth, d, n = w_mod.shape
    tn = 768
    return pl.pallas_call(
        _mod_kernel,
        grid=(depth, n // tn),
        in_specs=[pl.BlockSpec((MOD_ROWS, d), lambda l, j: (0, 0)),
                  pl.BlockSpec((None, d, tn), lambda l, j: (l, 0, j)),
                  pl.BlockSpec((None, 1, tn), lambda l, j: (l, 0, j))],
        out_specs=pl.BlockSpec((None, MOD_ROWS, tn), lambda l, j: (l, 0, j)),
        out_shape=jax.ShapeDtypeStruct((depth, MOD_ROWS, n), F32),
        compiler_params=_params(("arbitrary", "arbitrary")),
        name="modulation",
    )(c_rows, w_mod, b_mod.reshape(depth, 1, n))


def _modulated(x, g, shift, scale):
    h = x * lax.rsqrt(jnp.mean(x * x, axis=-1, keepdims=True) + EPS) * g
    return h * (1.0 + scale) + shift


def _rms(x, g):
    return x * lax.rsqrt(jnp.mean(x * x, axis=-1, keepdims=True) + EPS) * g


def _swap_halves(t):
    w = t.shape[-1]
    lane = lax.broadcasted_iota(jnp.int32, t.shape, 1)
    from_right = pltpu.roll(t, w - ROPE_FREQS, 1)
    from_left = pltpu.roll(t, ROPE_FREQS, 1)
    return jnp.where((lane % (2 * ROPE_FREQS)) < ROPE_FREQS, from_right, from_left)


def _tile_lanes(t, width):
    reps = width // t.shape[-1]
    return t if reps == 1 else jnp.concatenate([t] * reps, axis=-1)


def _head_norm(t, ones_bd, gain):
    w = t.shape[-1]
    t2 = t * t
    hi = t2.astype(BF16)
    lo = (t2 - hi.astype(F32)).astype(BF16)
    e = ones_bd[:w, :w]
    ss = (jnp.dot(hi, e, preferred_element_type=F32) + jnp.dot(lo, e, preferred_element_type=F32))
    return t * lax.rsqrt(ss * (1.0 / HEAD_DIM) + EPS) * gain


def _inproj_kernel(x_ref, shift_ref, scale_ref, g_ref, w_ref, ones_ref, cos_ref, sin_ref, qn_ref, kn_ref,
                   dft_ref, *rest, dims, kv_only, n_alias):
    out_refs = rest[n_alias:]
    nq, nkv, nv, nf, nc = dims["diff_qk"], dims["gqa_kv"], dims["diff_v"], dims["fourier"], dims["conv"]
    ngq = dims["gqa_q"]
    h = _modulated(x_ref[0], g_ref[...], shift_ref[0], scale_ref[0]).astype(BF16)

    def proj(start, width):
        return jnp.dot(h, w_ref[:, start:start + width], preferred_element_type=F32)

    def rope(t):
        w = t.shape[-1]
        return t * _tile_lanes(cos_ref[...], w) + _swap_halves(t) * _tile_lanes(sin_ref[...], w)

    ones_bd = ones_ref[...]
    off = 0
    dk = proj(off, nq); off += nq
    dv = proj(off, nv); off += nv
    gk = proj(off, nkv); off += nkv
    gv = proj(off, nkv); off += nkv
    kdt_ref, kgt_ref, vd_ref, vg_ref = out_refs[-4:]
    kdt_ref[0] = rope(dk).T.astype(BF16)
    kgt_ref[0] = rope(_head_norm(gk, ones_bd, kn_ref[...])).T.astype(BF16)
    vd_ref[0] = dv.astype(BF16)
    vg_ref[0] = gv.astype(BF16)
    if not kv_only:
        qd_ref, qg_ref, xf_ref, vc_ref = out_refs[:4]
        uf = proj(off, nf).astype(BF16); off += nf
        xf_ref[0] = jnp.dot(uf, dft_ref[...], preferred_element_type=F32).astype(BF16)
        dq = proj(off, nq); off += nq
        qd_ref[0] = (rope(dq) * Q_PRESCALE).astype(BF16)
        gq = proj(off, ngq); off += ngq
        qg_ref[0] = (rope(_head_norm(gq, ones_bd, qn_ref[...])) * Q_PRESCALE).astype(BF16)
        glu = proj(off, 2 * nc)
        vc_ref[0] = glu[:, :nc] * jax.nn.sigmoid(glu[:, nc:])


def _inproj(x, mod, g_pre, w_a, ones_bd, cos_t, sin_t, qn, kn, dft_ch, kv_bufs, *, dims, kv_only, row_start, lk,
            tm):
    b, l, d = x.shape
    tm = min(tm, l)
    nq, nkv, nv, nf, nc, ngq = (dims["diff_qk"], dims["gqa_kv"], dims["diff_v"], dims["fourier"],
                                dims["conv"], dims["gqa_q"])
    blk0 = row_start // tm
    assert row_start % tm == 0
    tok = lambda w: pl.BlockSpec((1, tm, w), lambda bi, i: (bi, i, 0))
    const = lambda shape: pl.BlockSpec(shape, lambda bi, i: (0,) * len(shape))
    tok_outs = [] if kv_only else [((b, l, nq), BF16), ((b, l, ngq), BF16), ((b, l, 2 * nf), BF16), ((b, l, nc), F32)]
    kv_shapes = [((b, nq, lk), BF16), ((b, nkv, lk), BF16), ((b, lk, nv), BF16), ((b, lk, nkv), BF16)]
    kv_specs = [pl.BlockSpec((1, nq, tm), lambda bi, i: (bi, 0, blk0 + i)),
                pl.BlockSpec((1, nkv, tm), lambda bi, i: (bi, 0, blk0 + i)),
                pl.BlockSpec((1, tm, nv), lambda bi, i: (bi, blk0 + i, 0)),
                pl.BlockSpec((1, tm, nkv), lambda bi, i: (bi, blk0 + i, 0))]
    n_alias = 0 if kv_bufs is None else len(kv_bufs)
    n_in = 11
    aliases = {n_in + k: len(tok_outs) + k for k in range(n_alias)}
    outs = pl.pallas_call(
        functools.partial(_inproj_kernel, dims=dims, kv_only=kv_only, n_alias=n_alias),
        grid=(b, l // tm),
        in_specs=[tok(d),
                  pl.BlockSpec((1, 1, d), lambda bi, i: (bi, 0, 0)),
                  pl.BlockSpec((1, 1, d), lambda bi, i: (bi, 0, 1)),
                  const((1, d)),
                  const(w_a.shape),
                  const(ones_bd.shape),
                  pl.BlockSpec((tm, 2 * HEAD_DIM), lambda bi, i: (i, 0)),
                  pl.BlockSpec((tm, 2 * HEAD_DIM), lambda bi, i: (i, 0)),
                  const(qn.shape), const(kn.shape), const(dft_ch.shape)]
                 + [pl.BlockSpec(memory_space=pl.ANY)] * n_alias,
        out_specs=[tok(s[-1]) for s, _ in tok_outs] + kv_specs,
        out_shape=[jax.ShapeDtypeStruct(s, dt) for s, dt in tok_outs + kv_shapes],
        input_output_aliases=aliases,
        compiler_params=_params(("arbitrary", "arbitrary")),
        name="inproj_kv" if kv_only else "inproj",
    )(x, mod, mod, g_pre, w_a, ones_bd, cos_t, sin_t, qn, kn, dft_ch, *(kv_bufs or ()))
    return outs[:len(tok_outs)], outs[len(tok_outs):]


def _softmax_pv(q, k_t, v_ext):
    s = jnp.dot(q, k_t, preferred_element_type=F32)
    e = jnp.exp2(s - jnp.max(s, axis=-1, keepdims=True))
    return jnp.dot(e.astype(BF16), v_ext, preferred_element_type=F32)


def _attn_kernel(qd_ref, qg_ref, kdt_ref, vd_ref, kgt_ref, vg_ref, lam_ref, subln_ref, od_ref, og_ref,
                 vdx_ref, vgx_ref, *, n_diff, n_gq, n_kv, lam_init):
    dh = HEAD_DIM
    dv = 2 * HEAD_DIM

    @pl.when(pl.program_id(1) == 0)
    def _():
        for hd in range(n_diff):
            vdx_ref[hd, :, :dv] = vd_ref[0, :, hd * dv:(hd + 1) * dv]
            vdx_ref[hd, :, dv:] = jnp.ones((vd_ref.shape[1], V7X_MXU_COLUMNS - dv), BF16)
        vgx_ref[:, :n_kv * dh] = vg_ref[0]
        vgx_ref[:, n_kv * dh:] = jnp.ones((vg_ref.shape[1], V7X_MXU_COLUMNS - n_kv * dh), BF16)

    lv = lam_ref[...]
    lam = (jnp.exp(jnp.sum(lv[0:1] * lv[1:2], axis=-1, keepdims=True))
           - jnp.exp(jnp.sum(lv[2:3] * lv[3:4], axis=-1, keepdims=True)) + lam_init)
    for hd in range(n_diff):
        maps = []
        for j in range(2):
            c = (2 * hd + j) * dh
            pv = _softmax_pv(qd_ref[0, :, c:c + dh], kdt_ref[0, c:c + dh, :], vdx_ref[hd])
            maps.append(pv[:, :dv] / pv[:, dv:])
        o = maps[0] - lam * maps[1]
        od_ref[0, :, hd * dv:(hd + 1) * dv] = (_rms(o, subln_ref[...]) * (1.0 - lam_init)).astype(BF16)
    group = n_gq // n_kv
    lane = lax.broadcasted_iota(jnp.int32, (qg_ref.shape[1], dv), 1)
    for pair in range(n_gq // 2):
        kv = (2 * pair) // group
        halves = []
        for hq in (2 * pair, 2 * pair + 1):
            pv = _softmax_pv(qg_ref[0, :, hq * dh:(hq + 1) * dh], kgt_ref[0, kv * dh:(kv + 1) * dh, :], vgx_ref[...])
            halves.append(pv[:, :dv] / pv[:, dv:])
        left = halves[0] if kv % 2 == 0 else pltpu.roll(halves[0], dh, 1)
        right = pltpu.roll(halves[1], dh, 1) if kv % 2 == 0 else halves[1]
        og_ref[0, :, pair * dv:(pair + 1) * dv] = jnp.where(lane < dh, left, right).astype(BF16)


def _attention(qd, qg, kv_bufs, lam_p, subln, *, dims, lam_init, tq, key_start, key_len):
    kd_t, kg_t, vd, vg = kv_bufs
    b, l, nq = qd.shape
    n_diff, n_gq, n_kv = dims["diff_heads"], dims["gqa_q"] // HEAD_DIM, dims["gqa_kv"] // HEAD_DIM
    assert n_kv * HEAD_DIM == 2 * HEAD_DIM and (n_gq // n_kv) % 2 == 0 and key_start % key_len == 0
    kb = key_start // key_len
    tq = min(tq, l)
    tok = lambda w: pl.BlockSpec((1, tq, w), lambda bi, i: (bi, i, 0))
    keys_t = lambda a: pl.BlockSpec((1, a.shape[1], key_len), lambda bi, i: (bi, 0, kb))
    vals = lambda a: pl.BlockSpec((1, key_len, a.shape[2]), lambda bi, i: (bi, kb, 0))
    const = lambda a: pl.BlockSpec(a.shape, lambda bi, i: (0,) * a.ndim)
    return pl.pallas_call(
        functools.partial(_attn_kernel, n_diff=n_diff, n_gq=n_gq, n_kv=n_kv, lam_init=lam_init),
        grid=(b, l // tq),
        in_specs=[tok(nq), tok(qg.shape[-1]), keys_t(kd_t), vals(vd), keys_t(kg_t), vals(vg),
                  const(lam_p), const(subln)],
        out_specs=[tok(dims["diff_v"]), tok(dims["gqa_q"])],
        out_shape=[jax.ShapeDtypeStruct((b, l, dims["diff_v"]), BF16),
                   jax.ShapeDtypeStruct((b, l, dims["gqa_q"]), BF16)],
        scratch_shapes=[pltpu.VMEM((n_diff, key_len, V7X_MXU_COLUMNS), BF16),
                        pltpu.VMEM((key_len, V7X_MXU_COLUMNS), BF16)],
        compiler_params=_params(("arbitrary", "arbitrary")),
        name="attention",
    )(qd, qg, kd_t, vd, kg_t, vg, lam_p, subln)


def _fourier_kernel(c_ref, s_ref, xf_ref, o_ref, *, nf, out_scale):
    y = (jnp.dot(c_ref[...], xf_ref[0, :, :nf], preferred_element_type=F32)
         - jnp.dot(s_ref[...], xf_ref[0, :, nf:], preferred_element_type=F32))
    o_ref[0] = (y * out_scale).astype(BF16)


def _fourier(dft_c, dft_s, xf, *, tr):
    b, l, nf2 = xf.shape
    nf = nf2 // 2
    tr = min(tr, l)
    return pl.pallas_call(
        functools.partial(_fourier_kernel, nf=nf, out_scale=float(l) ** -0.5),
        grid=(b, l // tr),
        in_specs=[pl.BlockSpec((tr, l), lambda bi, i: (i, 0)),
                  pl.BlockSpec((tr, l), lambda bi, i: (i, 0)),
                  pl.BlockSpec((1, l, nf2), lambda bi, i: (bi, 0, 0))],
        out_specs=pl.BlockSpec((1, tr, nf), lambda bi, i: (bi, i, 0)),
        out_shape=jax.ShapeDtypeStruct((b, l, nf), BF16),
        compiler_params=_params(("arbitrary", "arbitrary")),
        name="fourier",
    )(dft_c, dft_s, xf)


def _conv_ln_silu(win_ref, w_ref, b_ref, g_ref, beta_ref, yc_ref, *, rows, chunk):
    first = CONV_HALO - CONV_KERNEL // 2
    for r0 in range(0, rows, chunk):
        acc = jnp.zeros((chunk, w_ref.shape[-1]), F32) + b_ref[...]
        for phase in range(SUBLANES):
            part = None
            for t in range(CONV_KERNEL):
                if (t + first) % SUBLANES != phase:
                    continue
                base = r0 + t + first - phase
                term = w_ref[t:t + 1, :] * win_ref[base:base + chunk + SUBLANES, :]
                part = term if part is None else part + term
            acc = acc + part[phase:phase + chunk, :]
        mu = jnp.mean(acc, axis=-1, keepdims=True)
        cen = acc - mu
        var = jnp.mean(cen * cen, axis=-1, keepdims=True)
        y = cen * lax.rsqrt(var + EPS) * g_ref[...] + beta_ref[...]
        yc_ref[r0:r0 + chunk, :] = (y * jax.nn.sigmoid(y)).astype(BF16)


def _merge_kernel(x_ref, shift_ref, scale_ref, gate_ref, g_ref, wgate_ref, yf_ref, od_ref, og_ref,
                  vprev_ref, vcur_ref, vnext_ref, cw_ref, cb_ref, cg_ref, cbeta_ref,
                  wf_ref, wd_ref, wg_ref, wc_ref, wout_ref, gpost_ref, o_ref, win_ref, yc_ref):
    i = pl.program_id(1)
    tm = vcur_ref.shape[1]
    win_ref[:CONV_HALO, :] = jnp.where(i > 0, vprev_ref[0], 0.0)
    win_ref[CONV_HALO:CONV_HALO + tm, :] = vcur_ref[0]
    win_ref[CONV_HALO + tm:, :] = jnp.where(i < pl.num_programs(1) - 1, vnext_ref[0], 0.0)
    _conv_ln_silu(win_ref, cw_ref, cb_ref, cg_ref, cbeta_ref, yc_ref, rows=tm, chunk=32)

    x = x_ref[0]
    d = x.shape[-1]
    h = _modulated(x, g_ref[...], shift_ref[0], scale_ref[0]).astype(BF16)
    branches = ((yf_ref[0], wf_ref), (od_ref[0], wd_ref), (og_ref[0], wg_ref), (yc_ref[...], wc_ref))
    merged = None
    for k, (y, w_ref) in enumerate(branches):
        gate = jax.nn.sigmoid(jnp.dot(h, wgate_ref[:, k * d:(k + 1) * d], preferred_element_type=F32))
        term = gate * jnp.dot(y, w_ref[...], preferred_element_type=F32)
        merged = term if merged is None else merged + term
    mix = jnp.dot(merged.astype(BF16), wout_ref[...], preferred_element_type=F32)
    o_ref[0] = x + gate_ref[0] * _rms(mix, gpost_ref[...])


def _merge(x, mod, g_pre, w_gate, yf, od, og, vc, conv_w, conv_b, conv_g, conv_beta, wf, wd, wg, wc, wout, g_post,
           *, tm):
    b, l, d = x.shape
    tm = min(tm, l)
    c = vc.shape[-1]
    hb = tm // CONV_HALO
    n_hb = l // CONV_HALO
    tok = lambda w: pl.BlockSpec((1, tm, w), lambda bi, i: (bi, i, 0))
    modc = lambda k: pl.BlockSpec((1, 1, d), lambda bi, i: (bi, 0, k))
    const = lambda a: pl.BlockSpec(a.shape, lambda bi, i: (0,) * a.ndim)
    prev = pl.BlockSpec((1, CONV_HALO, c), lambda bi, i: (bi, jnp.maximum(i * hb - 1, 0), 0))
    nxt = pl.BlockSpec((1, CONV_HALO, c), lambda bi, i: (bi, jnp.minimum((i + 1) * hb, n_hb - 1), 0))
    return pl.pallas_call(
        _merge_kernel,
        grid=(b, l // tm),
        in_specs=[tok(d), modc(0), modc(1), modc(2), const(g_pre), const(w_gate),
                  tok(yf.shape[-1]), tok(od.shape[-1]), tok(og.shape[-1]), prev, tok(c), nxt,
                  const(conv_w), const(conv_b), const(conv_g), const(conv_beta),
                  const(wf), const(wd), const(wg), const(wc), const(wout), const(g_post)],
        out_specs=tok(d),
        out_shape=jax.ShapeDtypeStruct((b, l, d), F32),
        scratch_shapes=[pltpu.VMEM((tm + 2 * CONV_HALO, c), F32), pltpu.VMEM((tm, c), BF16)],
        compiler_params=_params(("arbitrary", "arbitrary")),
        name="merge",
    )(x, mod, mod, mod, g_pre, w_gate, yf, od, og, vc, vc, vc, conv_w, conv_b, conv_g, conv_beta,
      wf, wd, wg, wc, wout, g_post)


def _mlp_kernel(x_ref, shift_ref, scale_ref, gate_ref, g_ref, w1_ref, w2_ref, gpost_ref, o_ref, *, ff_chunk):
    x = x_ref[0]
    h = _modulated(x, g_ref[...], shift_ref[0], scale_ref[0]).astype(BF16)
    acc = None
    for c0 in range(0, w1_ref.shape[-1], ff_chunk):
        u = jnp.maximum(jnp.dot(h, w1_ref[:, c0:c0 + ff_chunk], preferred_element_type=F32), 0.0)
        part = jnp.dot((u * u).astype(BF16), w2_ref[c0:c0 + ff_chunk, :], preferred_element_type=F32)
        acc = part if acc is None else acc + part
    o_ref[0] = x + gate_ref[0] * _rms(acc, gpost_ref[...])


def _mlp(x, mod, g_pre, w1, w2, g_post, *, tm):
    b, l, d = x.shape
    tm = min(tm, l)
    tok = lambda w: pl.BlockSpec((1, tm, w), lambda bi, i: (bi, i, 0))
    modc = lambda k: pl.BlockSpec((1, 1, d), lambda bi, i: (bi, 0, k))
    const = lambda a: pl.BlockSpec(a.shape, lambda bi, i: (0,) * a.ndim)
    return pl.pallas_call(
        functools.partial(_mlp_kernel, ff_chunk=1024),
        grid=(b, l // tm),
        in_specs=[tok(d), modc(3), modc(4), modc(5), const(g_pre), const(w1), const(w2), const(g_post)],
        out_specs=tok(d),
        out_shape=jax.ShapeDtypeStruct((b, l, d), F32),
        compiler_params=_params(("arbitrary", "arbitrary")),
        name="mlp",
    )(x, mod, mod, mod, g_pre, w1, w2, g_post)


def _rope_tables(seq_len):
    rows = seq_len // GRID_W
    row = jnp.repeat(jnp.arange(rows, dtype=F32), GRID_W)
    col = jnp.tile(jnp.arange(GRID_W, dtype=F32), rows)
    inv_freq = ROPE_THETA ** (-jnp.arange(ROPE_FREQS, dtype=F32) / ROPE_FREQS)
    ang_r, ang_c = row[:, None] * inv_freq, col[:, None] * inv_freq
    cos64 = jnp.concatenate([jnp.cos(ang_r)] * 2 + [jnp.cos(ang_c)] * 2, axis=-1)
    sin64 = jnp.concatenate([-jnp.sin(ang_r), jnp.sin(ang_r), -jnp.sin(ang_c), jnp.sin(ang_c)], axis=-1)
    return jnp.concatenate([cos64] * 2, axis=-1), jnp.concatenate([sin64] * 2, axis=-1)


def _seq_dft(l):
    f = min(64, l)
    a_n = l // f
    k = jnp.arange(l, dtype=jnp.int32)[:, None]
    th1 = ((k * jnp.arange(a_n, dtype=jnp.int32)[None, :]) % a_n).astype(F32) * (2.0 * math.pi / a_n)
    th2 = ((k * jnp.arange(f, dtype=jnp.int32)[None, :]) % l).astype(F32) * (2.0 * math.pi / l)
    c1, s1, c2, s2 = jnp.cos(th1)[:, :, None], jnp.sin(th1)[:, :, None], jnp.cos(th2)[:, None, :], jnp.sin(th2)[:, None, :]
    c = (c1 * c2 - s1 * s2).reshape(l, l)
    s = (s1 * c2 + c1 * s2).reshape(l, l)
    return c.astype(BF16), s.astype(BF16)


def _channel_dft(n_groups):
    g = FOURIER_GROUP_DIM
    idx = jnp.arange(g, dtype=jnp.int32)
    th = ((idx[:, None] * idx[None, :]) % g).astype(F32) * (2.0 * math.pi / g)
    eye = jnp.eye(n_groups, dtype=F32)
    scale = float(g) ** -0.5
    return jnp.concatenate([jnp.kron(eye, jnp.cos(th) * scale), jnp.kron(eye, jnp.sin(th) * scale)],
                           axis=-1).astype(BF16)


def _mix_and_mlp(x, mod, lw, tok, kv_bufs, dft_seq, *, dims, lam_init, key_start, key_len, tm_mlp):
    qd, qg, xf, vc = tok
    od, og = _attention(qd, qg, kv_bufs, lw["lam_p"], lw["subln"], dims=dims, lam_init=lam_init, tq=256,
                        key_start=key_start, key_len=key_len)
    yf = _fourier(dft_seq[0], dft_seq[1], xf, tr=512)
    x = _merge(x, mod, lw["g_pre_mix"], lw["w_gate"], yf, od, og, vc, lw["conv_w"], lw["conv_b"], lw["conv_ln_g"],
               lw["conv_ln_b"], lw["w_br_f"], lw["w_br_d"], lw["w_br_g"], lw["w_br_c"], lw["w_out"],
               lw["g_post_mix"], tm=256)
    return _mlp(x, mod, lw["g_pre_mlp"], lw["w_ff1"], lw["w_ff2"], lw["g_post_mlp"], tm=tm_mlp)


def kernel(x, c, ctx, c_ctx, w_mod, b_mod, g_pre_mix, g_post_mix, g_pre_mlp, g_post_mlp, w_in, q_norm, k_norm,
           diff_lambda, diff_subln, conv_dw, conv_dw_bias, conv_ln_g, conv_ln_b, w_br_fourier, w_br_diff,
           w_br_gqa, w_br_conv, w_out, w_ff1, w_ff2):
    b, l, d = x.shape
    lc = ctx.shape[1]
    lk = l + lc
    depth = w_in.shape[0]
    dims = _dims(d)
    assert b + 1 <= MOD_ROWS and l % GRID_W == 0 and l % lc == 0

    rope_x = _rope_tables(l)
    rope_none = (jnp.ones((lc, 2 * HEAD_DIM), F32), jnp.zeros((lc, 2 * HEAD_DIM), F32))
    dft_x, dft_c = _seq_dft(l), _seq_dft(lc)
    n_groups = dims["fourier"] // FOURIER_GROUP_DIM
    heads = jnp.arange(dims["diff_qk"], dtype=jnp.int32) // HEAD_DIM
    ones_bd = (heads[:, None] == heads[None, :]).astype(BF16)
    dft_ch = _channel_dft(n_groups)

    c_rows = jnp.zeros((MOD_ROWS, d), F32).at[:b].set(c).at[b].set(c_ctx)
    mod_all = _modulation(c_rows, w_mod, b_mod)

    n_tok_cols = w_in.shape[-1] - N_BRANCHES * d

    xc = ctx
    for li in range(depth):
        lam_init = 0.8 - 0.6 * math.exp(-0.3 * li)
        row = lambda a: a[li].reshape(1, -1)
        lw = dict(
            w_a=w_in[li, :, :n_tok_cols].astype(BF16), w_gate=w_in[li, :, n_tok_cols:].astype(BF16),
            g_pre_mix=row(g_pre_mix), g_post_mix=row(g_post_mix), g_pre_mlp=row(g_pre_mlp),
            g_post_mlp=row(g_post_mlp),
            qn=jnp.tile(q_norm[li], dims["gqa_q"] // HEAD_DIM).reshape(1, -1),
            kn=jnp.tile(k_norm[li], dims["gqa_kv"] // HEAD_DIM).reshape(1, -1),
            lam_p=diff_lambda[li], subln=row(diff_subln),
            conv_w=conv_dw[li], conv_b=row(conv_dw_bias), conv_ln_g=row(conv_ln_g), conv_ln_b=row(conv_ln_b),
            w_br_f=w_br_fourier[li].astype(BF16), w_br_d=w_br_diff[li].astype(BF16),
            w_br_g=w_br_gqa[li].astype(BF16), w_br_c=w_br_conv[li].astype(BF16), w_out=w_out[li].astype(BF16),
            w_ff1=w_ff1[li].astype(BF16), w_ff2=w_ff2[li].astype(BF16))
        mod_x = mod_all[li, :b].reshape(b, 1, N_MOD * d)
        mod_c = jnp.broadcast_to(mod_all[li, b].reshape(1, 1, N_MOD * d), (b, 1, N_MOD * d))
        last = li == depth - 1

        inproj = functools.partial(_inproj, g_pre=lw["g_pre_mix"], w_a=lw["w_a"], ones_bd=ones_bd, qn=lw["qn"],
                                   kn=lw["kn"], dft_ch=dft_ch, dims=dims, lk=lk)
        tok_c, kv_bufs = inproj(xc, mod_c, cos_t=rope_none[0], sin_t=rope_none[1], kv_bufs=None, kv_only=last,
                                row_start=l, tm=lc)
        tok_x, kv_bufs = inproj(x, mod_x, cos_t=rope_x[0], sin_t=rope_x[1], kv_bufs=kv_bufs, kv_only=False,
                                row_start=0, tm=512)
        mix = functools.partial(_mix_and_mlp, lw=lw, kv_bufs=kv_bufs, dims=dims, lam_init=lam_init)
        x = mix(x, mod_x, tok=tok_x, dft_seq=dft_x, key_start=0, key_len=lk, tm_mlp=512)
        if not last:
            xc = mix(xc, mod_c, tok=tok_c, dft_seq=dft_c, key_start=l, key_len=lc, tm_mlp=lc)
    return x
```

```python
import functools
import math

import jax
import jax.numpy as jnp
from jax import lax
from jax.experimental import pallas as pl
from jax.experimental.pallas import tpu as pltpu

F32 = jnp.float32
BF16 = jnp.bfloat16

GRID_W = 64
HEAD_DIM = 64
ROPE_FREQS = HEAD_DIM // 4
ROPE_THETA = 10000.0
EPS = 1e-6
ATTN_SCALE = HEAD_DIM ** -0.5
Q_PRESCALE = ATTN_SCALE * math.log2(math.e)
FOURIER_GROUP_DIM = 64
CONV_KERNEL = 31
CONV_HALO = 16
N_BRANCHES = 4
N_MOD = 6
MOD_ROWS = 8

V7X_VMEM_LIMIT_BYTES = 56 * 1024 * 1024
V7X_MXU_COLUMNS = 256
SUBLANES = 8
BF16_SUBLANES = 16
FOURIER_STAGE_LEN = 64


def _dims(d_model):
    fourier = 3 * d_model // 8
    diff_heads = d_model // 256
    diff_qk = diff_heads * 2 * HEAD_DIM
    diff_v = diff_heads * 2 * HEAD_DIM
    gqa_q = (d_model // 128) * HEAD_DIM
    gqa_kv = (d_model // 128 // 4) * HEAD_DIM
    conv = 3 * d_model // 8
    return dict(fourier=fourier, diff_heads=diff_heads, diff_qk=diff_qk, diff_v=diff_v,
                gqa_q=gqa_q, gqa_kv=gqa_kv, conv=conv)


def _params(semantics):
    return pltpu.CompilerParams(dimension_semantics=semantics, vmem_limit_bytes=V7X_VMEM_LIMIT_BYTES)


def _mod_kernel(c_ref, w_ref, b_ref, o_ref):
    c = c_ref[...]
    a = c * jax.nn.sigmoid(c)
    o_ref[...] = jnp.dot(a, w_ref[...], preferred_element_type=F32,
                         precision=lax.Precision.HIGHEST) + b_ref[...]


def _modulation(c_rows, w_mod, b_mod):
    depth, d, n = w_mod.shape
    tn = 768
    return pl.pallas_call(
        _mod_kernel,
        grid=(depth, n // tn),
        in_specs=[pl.BlockSpec((MOD_ROWS, d), lambda l, j: (0, 0)),
                  pl.BlockSpec((None, d, tn), lambda l, j: (l, 0, j)),
                  pl.BlockSpec((None, 1, tn), lambda l, j: (l, 0, j))],
        out_specs=pl.BlockSpec((None, MOD_ROWS, tn), lambda l, j: (l, 0, j)),
        out_shape=jax.ShapeDtypeStruct((depth, MOD_ROWS, n), F32),
        compiler_params=_params(("arbitrary", "arbitrary")),
        name="modulation",
    )(c_rows, w_mod, b_mod.reshape(depth, 1, n))


def _modulated(x, g, shift, scale):
    h = x * lax.rsqrt(jnp.mean(x * x, axis=-1, keepdims=True) + EPS) * g
    return h * (1.0 + scale) + shift


def _rms(x, g):
    return x * lax.rsqrt(jnp.mean(x * x, axis=-1, keepdims=True) + EPS) * g


def _swap_halves(t):
    w = t.shape[-1]
    lane = lax.broadcasted_iota(jnp.int32, t.shape, 1)
    from_right = pltpu.roll(t, w - ROPE_FREQS, 1)
    from_left = pltpu.roll(t, ROPE_FREQS, 1)
    return jnp.where((lane % (2 * ROPE_FREQS)) < ROPE_FREQS, from_right, from_left)


def _tile_lanes(t, width):
    reps = width // t.shape[-1]
    return t if reps == 1 else jnp.concatenate([t] * reps, axis=-1)


def _head_norm(t, ones_bd, gain):
    w = t.shape[-1]
    t2 = t * t
    hi = t2.astype(BF16)
    lo = (t2 - hi.astype(F32)).astype(BF16)
    e = ones_bd[:w, :w]
    ss = (jnp.dot(hi, e, preferred_element_type=F32) + jnp.dot(lo, e, preferred_element_type=F32))
    return t * lax.rsqrt(ss * (1.0 / HEAD_DIM) + EPS) * gain


def _inproj_kernel(x_ref, shift_ref, scale_ref, g_ref, w_ref, ones_ref, cos_ref, sin_ref, qn_ref, kn_ref,
                   dft_ref, *rest, dims, kv_only, n_alias):
    out_refs = rest[n_alias:]
    nq, nkv, nv, nf, nc = dims["diff_qk"], dims["gqa_kv"], dims["diff_v"], dims["fourier"], dims["conv"]
    ngq = dims["gqa_q"]
    h = _modulated(x_ref[0], g_ref[...], shift_ref[0], scale_ref[0]).astype(BF16)

    def proj(start, width):
        return jnp.dot(h, w_ref[:, start:start + width], preferred_element_type=F32)

    def rope(t):
        w = t.shape[-1]
        return t * _tile_lanes(cos_ref[...], w) + _swap_halves(t) * _tile_lanes(sin_ref[...], w)

    ones_bd = ones_ref[...]
    off = 0
    dk = proj(off, nq); off += nq
    dv = proj(off, nv); off += nv
    gk = proj(off, nkv); off += nkv
    gv = proj(off, nkv); off += nkv
    kdt_ref, kgt_ref, vd_ref, vg_ref = out_refs[-4:]
    kdt_ref[0] = rope(dk).T.astype(BF16)
    kgt_ref[0] = rope(_head_norm(gk, ones_bd, kn_ref[...])).T.astype(BF16)
    vd_ref[0] = dv.astype(BF16)
    vg_ref[0] = gv.astype(BF16)
    if not kv_only:
        qd_ref, qg_ref, xf_ref, vc_ref = out_refs[:4]
        uf = proj(off, nf).astype(BF16); off += nf
        xf_ref[0] = jnp.dot(uf, dft_ref[...], preferred_element_type=F32).astype(BF16)
        dq = proj(off, nq); off += nq
        qd_ref[0] = (rope(dq) * Q_PRESCALE).astype(BF16)
        gq = proj(off, ngq); off += ngq
        qg_ref[0] = (rope(_head_norm(gq, ones_bd, qn_ref[...])) * Q_PRESCALE).astype(BF16)
        glu = proj(off, 2 * nc)
        vc_ref[0] = glu[:, :nc] * jax.nn.sigmoid(glu[:, nc:])


def _inproj(x, mod, g_pre, w_a, ones_bd, cos_t, sin_t, qn, kn, dft_ch, kv_bufs, *, dims, kv_only, row_start, lk,
            tm):
    b, l, d = x.shape
    tm = min(tm, l)
    nq, nkv, nv, nf, nc, ngq = (dims["diff_qk"], dims["gqa_kv"], dims["diff_v"], dims["fourier"],
                                dims["conv"], dims["gqa_q"])
    blk0 = row_start // tm
    assert row_start % tm == 0
    tok = lambda w: pl.BlockSpec((1, tm, w), lambda bi, i: (bi, i, 0))
    const = lambda shape: pl.BlockSpec(shape, lambda bi, i: (0,) * len(shape))
    tok_outs = [] if kv_only else [((b, l, nq), BF16), ((b, l, ngq), BF16), ((b, l, 2 * nf), BF16), ((b, l, nc), F32)]
    kv_shapes = [(a.shape, a.dtype) for a in kv_bufs]
    kv_specs = [pl.BlockSpec((1, nq, tm), lambda bi, i: (bi, 0, blk0 + i)),
                pl.BlockSpec((1, nkv, tm), lambda bi, i: (bi, 0, blk0 + i)),
                pl.BlockSpec((1, tm, nv), lambda bi, i: (bi, blk0 + i, 0)),
                pl.BlockSpec((1, tm, nkv), lambda bi, i: (bi, blk0 + i, 0))]
    n_alias = len(kv_bufs)
    n_in = 11
    aliases = {n_in + k: len(tok_outs) + k for k in range(n_alias)}
    outs = pl.pallas_call(
        functools.partial(_inproj_kernel, dims=dims, kv_only=kv_only, n_alias=n_alias),
        grid=(b, l // tm),
        in_specs=[tok(d),
                  pl.BlockSpec((1, 1, d), lambda bi, i: (bi, 0, 0)),
                  pl.BlockSpec((1, 1, d), lambda bi, i: (bi, 0, 1)),
                  const((1, d)),
                  const(w_a.shape),
                  const(ones_bd.shape),
                  pl.BlockSpec((tm, 2 * HEAD_DIM), lambda bi, i: (i, 0)),
                  pl.BlockSpec((tm, 2 * HEAD_DIM), lambda bi, i: (i, 0)),
                  const(qn.shape), const(kn.shape), const(dft_ch.shape)]
                 + [pl.BlockSpec(memory_space=pl.ANY)] * n_alias,
        out_specs=[tok(s[-1]) for s, _ in tok_outs] + kv_specs,
        out_shape=[jax.ShapeDtypeStruct(s, dt) for s, dt in tok_outs + kv_shapes],
        input_output_aliases=aliases,
        compiler_params=_params(("arbitrary", "arbitrary")),
        name="inproj_kv" if kv_only else "inproj",
    )(x, mod, mod, g_pre, w_a, ones_bd, cos_t, sin_t, qn, kn, dft_ch, *kv_bufs)
    return outs[:len(tok_outs)], outs[len(tok_outs):]


def _softmax_pv(q, k_t, v_ext):
    s = jnp.dot(q, k_t, preferred_element_type=F32)
    e = jnp.exp2(s - jnp.max(s, axis=-1, keepdims=True))
    return jnp.dot(e.astype(BF16), v_ext, preferred_element_type=F32)


def _attn_kernel(qd_ref, qg_ref, kdt_ref, vd_ref, kgt_ref, vg_ref, lam_ref, subln_ref, od_ref, og_ref,
                 vdx_ref, vgx_ref, *, n_diff, n_gq, n_kv, lam_init):
    dh = HEAD_DIM
    dv = 2 * HEAD_DIM

    @pl.when(pl.program_id(1) == 0)
    def _():
        for hd in range(n_diff):
            vdx_ref[hd, :, :dv] = vd_ref[0, :, hd * dv:(hd + 1) * dv]
            vdx_ref[hd, :, dv:] = jnp.ones((vd_ref.shape[1], V7X_MXU_COLUMNS - dv), BF16)
        vgx_ref[:, :n_kv * dh] = vg_ref[0]
        vgx_ref[:, n_kv * dh:] = jnp.ones((vg_ref.shape[1], V7X_MXU_COLUMNS - n_kv * dh), BF16)

    lv = lam_ref[...]
    lam = (jnp.exp(jnp.sum(lv[0:1] * lv[1:2], axis=-1, keepdims=True))
           - jnp.exp(jnp.sum(lv[2:3] * lv[3:4], axis=-1, keepdims=True)) + lam_init)
    for hd in range(n_diff):
        maps = []
        for j in range(2):
            c = (2 * hd + j) * dh
            pv = _softmax_pv(qd_ref[0, :, c:c + dh], kdt_ref[0, c:c + dh, :], vdx_ref[hd])
            maps.append(pv[:, :dv] / pv[:, dv:])
        o = maps[0] - lam * maps[1]
        od_ref[0, :, hd * dv:(hd + 1) * dv] = (_rms(o, subln_ref[...]) * (1.0 - lam_init)).astype(BF16)
    group = n_gq // n_kv
    lane = lax.broadcasted_iota(jnp.int32, (qg_ref.shape[1], dv), 1)
    for pair in range(n_gq // 2):
        kv = (2 * pair) // group
        halves = []
        for hq in (2 * pair, 2 * pair + 1):
            pv = _softmax_pv(qg_ref[0, :, hq * dh:(hq + 1) * dh], kgt_ref[0, kv * dh:(kv + 1) * dh, :], vgx_ref[...])
            halves.append(pv[:, :dv] / pv[:, dv:])
        left = halves[0] if kv % 2 == 0 else pltpu.roll(halves[0], dh, 1)
        right = pltpu.roll(halves[1], dh, 1) if kv % 2 == 0 else halves[1]
        og_ref[0, :, pair * dv:(pair + 1) * dv] = jnp.where(lane < dh, left, right).astype(BF16)


def _attention(qd, qg, kv_bufs, lam_p, subln, *, dims, lam_init, tq, key_start, key_len):
    kd_t, kg_t, vd, vg = kv_bufs
    b, l, nq = qd.shape
    n_diff, n_gq, n_kv = dims["diff_heads"], dims["gqa_q"] // HEAD_DIM, dims["gqa_kv"] // HEAD_DIM
    assert n_kv * HEAD_DIM == 2 * HEAD_DIM and (n_gq // n_kv) % 2 == 0 and key_start % key_len == 0
    kb = key_start // key_len
    tq = min(tq, l)
    tok = lambda w: pl.BlockSpec((1, tq, w), lambda bi, i: (bi, i, 0))
    keys_t = lambda a: pl.BlockSpec((1, a.shape[1], key_len), lambda bi, i: (bi, 0, kb))
    vals = lambda a: pl.BlockSpec((1, key_len, a.shape[2]), lambda bi, i: (bi, kb, 0))
    const = lambda a: pl.BlockSpec(a.shape, lambda bi, i: (0,) * a.ndim)
    return pl.pallas_call(
        functools.partial(_attn_kernel, n_diff=n_diff, n_gq=n_gq, n_kv=n_kv, lam_init=lam_init),
        grid=(b, l // tq),
        in_specs=[tok(nq), tok(qg.shape[-1]), keys_t(kd_t), vals(vd), keys_t(kg_t), vals(vg),
                  const(lam_p), const(subln)],
        out_specs=[tok(dims["diff_v"]), tok(dims["gqa_q"])],
        out_shape=[jax.ShapeDtypeStruct((b, l, dims["diff_v"]), BF16),
                   jax.ShapeDtypeStruct((b, l, dims["gqa_q"]), BF16)],
        scratch_shapes=[pltpu.VMEM((n_diff, key_len, V7X_MXU_COLUMNS), BF16),
                        pltpu.VMEM((key_len, V7X_MXU_COLUMNS), BF16)],
        compiler_params=_params(("arbitrary", "arbitrary")),
        name="attention",
    )(qd, qg, kd_t, vd, kg_t, vg, lam_p, subln)


def _fourier_kernel(c_ref, s_ref, xf_ref, o_ref, *, nf, out_scale):
    y = (jnp.dot(c_ref[...], xf_ref[0, :, :nf], preferred_element_type=F32)
         - jnp.dot(s_ref[...], xf_ref[0, :, nf:], preferred_element_type=F32))
    o_ref[0] = (y * out_scale).astype(BF16)


def _fourier_dense(tables, xf, *, tr=512):
    dft_c, dft_s = tables
    b, l, nf2 = xf.shape
    nf = nf2 // 2
    tr = min(tr, l)
    return pl.pallas_call(
        functools.partial(_fourier_kernel, nf=nf, out_scale=float(l) ** -0.5),
        grid=(b, l // tr),
        in_specs=[pl.BlockSpec((tr, l), lambda bi, i: (i, 0)),
                  pl.BlockSpec((tr, l), lambda bi, i: (i, 0)),
                  pl.BlockSpec((1, l, nf2), lambda bi, i: (bi, 0, 0))],
        out_specs=pl.BlockSpec((1, tr, nf), lambda bi, i: (bi, i, 0)),
        out_shape=jax.ShapeDtypeStruct((b, l, nf), BF16),
        compiler_params=_params(("arbitrary", "arbitrary")),
        name="fourier",
    )(dft_c, dft_s, xf)


def _fft_a_kernel(z_ref, w_ref, cos_ref, sin_ref, o_ref, *, nf, slabs):
    n1 = z_ref.shape[1]
    r = jnp.dot(w_ref[...], z_ref[0], preferred_element_type=F32)
    for j in range(slabs):
        cz = r[:n1, j * 2 * nf:(j + 1) * 2 * nf]
        sz = r[n1:, j * 2 * nf:(j + 1) * 2 * nf]
        ar = cz[:, :nf] - sz[:, nf:]
        ai = -(sz[:, :nf] + cz[:, nf:])
        ct = _tile_lanes(cos_ref[j], nf)
        st = _tile_lanes(sin_ref[j], nf)
        o_ref[0, j, :, :nf] = (ar * ct + ai * st).astype(BF16)
        o_ref[0, j, :, nf:] = (ai * ct - ar * st).astype(BF16)


def _fft_c_kernel(b_ref, wc_ref, ws_ref, o_ref, *, nf, cols, out_scale):
    for k in range(cols):
        y = (jnp.dot(wc_ref[...], b_ref[0, :, k * 2 * nf:k * 2 * nf + nf], preferred_element_type=F32)
             + jnp.dot(ws_ref[...], b_ref[0, :, k * 2 * nf + nf:(k + 1) * 2 * nf], preferred_element_type=F32))
        o_ref[0, :, k * nf:(k + 1) * nf] = (y * out_scale).astype(BF16)


def _fourier_two_stage(tables, xf, *, slabs=8, cols=8):
    w_a, tw_cos, tw_sin, w_c_cos, w_c_sin = tables
    b, l, nf2 = xf.shape
    nf = nf2 // 2
    n2 = w_c_cos.shape[0]
    n1 = l // n2
    staged = pl.pallas_call(
        functools.partial(_fft_a_kernel, nf=nf, slabs=slabs),
        grid=(b, n2 // slabs),
        in_specs=[pl.BlockSpec((1, n1, slabs * nf2), lambda bi, s: (bi, 0, s)),
                  pl.BlockSpec(w_a.shape, lambda bi, s: (0, 0)),
                  pl.BlockSpec((slabs, n1, 2 * HEAD_DIM), lambda bi, s: (s, 0, 0)),
                  pl.BlockSpec((slabs, n1, 2 * HEAD_DIM), lambda bi, s: (s, 0, 0))],
        out_specs=pl.BlockSpec((1, slabs, n1, nf2), lambda bi, s: (bi, s, 0, 0)),
        out_shape=jax.ShapeDtypeStruct((b, n2, n1, nf2), BF16),
        compiler_params=_params(("arbitrary", "arbitrary")),
        name="fourier_a",
    )(xf.reshape(b, n1, n2 * nf2), w_a, tw_cos, tw_sin)
    out = pl.pallas_call(
        functools.partial(_fft_c_kernel, nf=nf, cols=cols, out_scale=float(l) ** -0.5),
        grid=(b, n1 // cols),
        in_specs=[pl.BlockSpec((1, n2, cols * nf2), lambda bi, s: (bi, 0, s)),
                  pl.BlockSpec(w_c_cos.shape, lambda bi, s: (0, 0)),
                  pl.BlockSpec(w_c_sin.shape, lambda bi, s: (0, 0))],
        out_specs=pl.BlockSpec((1, n2, cols * nf), lambda bi, s: (bi, 0, s)),
        out_shape=jax.ShapeDtypeStruct((b, n2, n1 * nf), BF16),
        compiler_params=_params(("arbitrary", "arbitrary")),
        name="fourier_c",
    )(staged.reshape(b, n2, n1 * nf2), w_c_cos, w_c_sin)
    return out.reshape(b, l, nf)


def _conv_ln_silu(win_ref, w_ref, b_ref, g_ref, beta_ref, yc_ref, *, rows, chunk):
    first = CONV_HALO - CONV_KERNEL // 2
    for r0 in range(0, rows, chunk):
        acc = jnp.zeros((chunk, w_ref.shape[-1]), F32) + b_ref[...]
        for phase in range(SUBLANES):
            part = None
            for t in range(CONV_KERNEL):
                if (t + first) % SUBLANES != phase:
                    continue
                base = r0 + t + first - phase
                term = w_ref[t:t + 1, :] * win_ref[base:base + chunk + SUBLANES, :]
                part = term if part is None else part + term
            acc = acc + part[phase:phase + chunk, :]
        mu = jnp.mean(acc, axis=-1, keepdims=True)
        cen = acc - mu
        var = jnp.mean(cen * cen, axis=-1, keepdims=True)
        y = cen * lax.rsqrt(var + EPS) * g_ref[...] + beta_ref[...]
        yc_ref[r0:r0 + chunk, :] = (y * jax.nn.sigmoid(y)).astype(BF16)


def _merge_kernel(x_ref, shift_ref, scale_ref, gate_ref, g_ref, wgate_ref, yf_ref, od_ref, og_ref,
                  vprev_ref, vcur_ref, vnext_ref, cw_ref, cb_ref, cg_ref, cbeta_ref,
                  wf_ref, wd_ref, wg_ref, wc_ref, wout_ref, gpost_ref, o_ref, win_ref, yc_ref):
    i = pl.program_id(1)
    tm = vcur_ref.shape[1]
    win_ref[:CONV_HALO, :] = jnp.where(i > 0, vprev_ref[0], 0.0)
    win_ref[CONV_HALO:CONV_HALO + tm, :] = vcur_ref[0]
    win_ref[CONV_HALO + tm:, :] = jnp.where(i < pl.num_programs(1) - 1, vnext_ref[0], 0.0)
    _conv_ln_silu(win_ref, cw_ref, cb_ref, cg_ref, cbeta_ref, yc_ref, rows=tm, chunk=32)

    x = x_ref[0]
    d = x.shape[-1]
    h = _modulated(x, g_ref[...], shift_ref[0], scale_ref[0]).astype(BF16)
    branches = ((yf_ref[0], wf_ref), (od_ref[0], wd_ref), (og_ref[0], wg_ref), (yc_ref[...], wc_ref))
    merged = None
    for k, (y, w_ref) in enumerate(branches):
        gate = jax.nn.sigmoid(jnp.dot(h, wgate_ref[:, k * d:(k + 1) * d], preferred_element_type=F32))
        term = gate * jnp.dot(y, w_ref[...], preferred_element_type=F32)
        merged = term if merged is None else merged + term
    mix = jnp.dot(merged.astype(BF16), wout_ref[...], preferred_element_type=F32)
    o_ref[0] = x + gate_ref[0] * _rms(mix, gpost_ref[...])


def _merge(x, mod, g_pre, w_gate, yf, od, og, vc, conv_w, conv_b, conv_g, conv_beta, wf, wd, wg, wc, wout, g_post,
           *, tm):
    b, l, d = x.shape
    tm = min(tm, l)
    c = vc.shape[-1]
    hb = tm // CONV_HALO
    n_hb = l // CONV_HALO
    tok = lambda w: pl.BlockSpec((1, tm, w), lambda bi, i: (bi, i, 0))
    modc = lambda k: pl.BlockSpec((1, 1, d), lambda bi, i: (bi, 0, k))
    const = lambda a: pl.BlockSpec(a.shape, lambda bi, i: (0,) * a.ndim)
    prev = pl.BlockSpec((1, CONV_HALO, c), lambda bi, i: (bi, jnp.maximum(i * hb - 1, 0), 0))
    nxt = pl.BlockSpec((1, CONV_HALO, c), lambda bi, i: (bi, jnp.minimum((i + 1) * hb, n_hb - 1), 0))
    return pl.pallas_call(
        _merge_kernel,
        grid=(b, l // tm),
        in_specs=[tok(d), modc(0), modc(1), modc(2), const(g_pre), const(w_gate),
                  tok(yf.shape[-1]), tok(od.shape[-1]), tok(og.shape[-1]), prev, tok(c), nxt,
                  const(conv_w), const(conv_b), const(conv_g), const(conv_beta),
                  const(wf), const(wd), const(wg), const(wc), const(wout), const(g_post)],
        out_specs=tok(d),
        out_shape=jax.ShapeDtypeStruct((b, l, d), F32),
        scratch_shapes=[pltpu.VMEM((tm + 2 * CONV_HALO, c), F32), pltpu.VMEM((tm, c), BF16)],
        compiler_params=_params(("arbitrary", "arbitrary")),
        name="merge",
    )(x, mod, mod, mod, g_pre, w_gate, yf, od, og, vc, vc, vc, conv_w, conv_b, conv_g, conv_beta,
      wf, wd, wg, wc, wout, g_post)


def _mlp_kernel(x_ref, shift_ref, scale_ref, gate_ref, g_ref, w1_ref, w2_ref, gpost_ref, o_ref, *, ff_chunk):
    x = x_ref[0]
    h = _modulated(x, g_ref[...], shift_ref[0], scale_ref[0]).astype(BF16)
    acc = None
    for c0 in range(0, w1_ref.shape[-1], ff_chunk):
        u = jnp.maximum(jnp.dot(h, w1_ref[:, c0:c0 + ff_chunk], preferred_element_type=F32), 0.0)
        part = jnp.dot((u * u).astype(BF16), w2_ref[c0:c0 + ff_chunk, :], preferred_element_type=F32)
        acc = part if acc is None else acc + part
    o_ref[0] = x + gate_ref[0] * _rms(acc, gpost_ref[...])


def _mlp(x, mod, g_pre, w1, w2, g_post, *, tm):
    b, l, d = x.shape
    tm = min(tm, l)
    tok = lambda w: pl.BlockSpec((1, tm, w), lambda bi, i: (bi, i, 0))
    modc = lambda k: pl.BlockSpec((1, 1, d), lambda bi, i: (bi, 0, k))
    const = lambda a: pl.BlockSpec(a.shape, lambda bi, i: (0,) * a.ndim)
    return pl.pallas_call(
        functools.partial(_mlp_kernel, ff_chunk=1024),
        grid=(b, l // tm),
        in_specs=[tok(d), modc(3), modc(4), modc(5), const(g_pre), const(w1), const(w2), const(g_post)],
        out_specs=tok(d),
        out_shape=jax.ShapeDtypeStruct((b, l, d), F32),
        compiler_params=_params(("arbitrary", "arbitrary")),
        name="mlp",
    )(x, mod, mod, mod, g_pre, w1, w2, g_post)


def _rope_tables(seq_len):
    rows = seq_len // GRID_W
    row = jnp.repeat(jnp.arange(rows, dtype=F32), GRID_W)
    col = jnp.tile(jnp.arange(GRID_W, dtype=F32), rows)
    inv_freq = ROPE_THETA ** (-jnp.arange(ROPE_FREQS, dtype=F32) / ROPE_FREQS)
    ang_r, ang_c = row[:, None] * inv_freq, col[:, None] * inv_freq
    cos64 = jnp.concatenate([jnp.cos(ang_r)] * 2 + [jnp.cos(ang_c)] * 2, axis=-1)
    sin64 = jnp.concatenate([-jnp.sin(ang_r), jnp.sin(ang_r), -jnp.sin(ang_c), jnp.sin(ang_c)], axis=-1)
    return jnp.concatenate([cos64] * 2, axis=-1), jnp.concatenate([sin64] * 2, axis=-1)


def _seq_dft(l):
    f = min(64, l)
    a_n = l // f
    k = jnp.arange(l, dtype=jnp.int32)[:, None]
    th1 = ((k * jnp.arange(a_n, dtype=jnp.int32)[None, :]) % a_n).astype(F32) * (2.0 * math.pi / a_n)
    th2 = ((k * jnp.arange(f, dtype=jnp.int32)[None, :]) % l).astype(F32) * (2.0 * math.pi / l)
    c1, s1, c2, s2 = jnp.cos(th1)[:, :, None], jnp.sin(th1)[:, :, None], jnp.cos(th2)[:, None, :], jnp.sin(th2)[:, None, :]
    c = (c1 * c2 - s1 * s2).reshape(l, l)
    s = (s1 * c2 + c1 * s2).reshape(l, l)
    return c.astype(BF16), s.astype(BF16)


def _two_stage_dft(l, n2):
    n1 = l // n2
    angle = lambda num, den: (num % den).astype(F32) * (2.0 * math.pi / den)
    i1 = jnp.arange(n1, dtype=jnp.int32)
    i2 = jnp.arange(n2, dtype=jnp.int32)
    th_a = angle(i1[:, None] * i1[None, :], n1)
    th_t = angle(i2[:, None] * i1[None, :], l)
    th_c = angle(i2[:, None] * i2[None, :], n2)
    lanes = lambda t: jnp.broadcast_to(t[:, :, None], t.shape + (2 * HEAD_DIM,))
    return (jnp.concatenate([jnp.cos(th_a), jnp.sin(th_a)], axis=0).astype(BF16),
            lanes(jnp.cos(th_t)), lanes(jnp.sin(th_t)), jnp.cos(th_c).astype(BF16), jnp.sin(th_c).astype(BF16))


def _channel_dft(n_groups):
    g = FOURIER_GROUP_DIM
    idx = jnp.arange(g, dtype=jnp.int32)
    th = ((idx[:, None] * idx[None, :]) % g).astype(F32) * (2.0 * math.pi / g)
    eye = jnp.eye(n_groups, dtype=F32)
    scale = float(g) ** -0.5
    return jnp.concatenate([jnp.kron(eye, jnp.cos(th) * scale), jnp.kron(eye, jnp.sin(th) * scale)],
                           axis=-1).astype(BF16)


def _seq_fourier(l):
    n2 = FOURIER_STAGE_LEN
    if l % n2 == 0 and (l // n2) % BF16_SUBLANES == 0:
        return functools.partial(_fourier_two_stage, _two_stage_dft(l, n2))
    return functools.partial(_fourier_dense, _seq_dft(l))


def _mix_and_mlp(x, mod, lw, tok, kv_bufs, fourier, *, dims, lam_init, key_start, key_len, tm_mlp):
    qd, qg, xf, vc = tok
    od, og = _attention(qd, qg, kv_bufs, lw["lam_p"], lw["subln"], dims=dims, lam_init=lam_init, tq=256,
                        key_start=key_start, key_len=key_len)
    yf = fourier(xf)
    x = _merge(x, mod, lw["g_pre_mix"], lw["w_gate"], yf, od, og, vc, lw["conv_w"], lw["conv_b"], lw["conv_ln_g"],
               lw["conv_ln_b"], lw["w_br_f"], lw["w_br_d"], lw["w_br_g"], lw["w_br_c"], lw["w_out"],
               lw["g_post_mix"], tm=256)
    return _mlp(x, mod, lw["g_pre_mlp"], lw["w_ff1"], lw["w_ff2"], lw["g_post_mlp"], tm=tm_mlp)


def kernel(x, c, ctx, c_ctx, w_mod, b_mod, g_pre_mix, g_post_mix, g_pre_mlp, g_post_mlp, w_in, q_norm, k_norm,
           diff_lambda, diff_subln, conv_dw, conv_dw_bias, conv_ln_g, conv_ln_b, w_br_fourier, w_br_diff,
           w_br_gqa, w_br_conv, w_out, w_ff1, w_ff2):
    b, l, d = x.shape
    lc = ctx.shape[1]
    lk = l + lc
    depth = w_in.shape[0]
    dims = _dims(d)
    assert b + 1 <= MOD_ROWS and l % GRID_W == 0 and l % lc == 0

    rope_x = _rope_tables(l)
    rope_none = (jnp.ones((lc, 2 * HEAD_DIM), F32), jnp.zeros((lc, 2 * HEAD_DIM), F32))
    fourier_x, fourier_c = _seq_fourier(l), _seq_fourier(lc)
    n_groups = dims["fourier"] // FOURIER_GROUP_DIM
    heads = jnp.arange(dims["diff_qk"], dtype=jnp.int32) // HEAD_DIM
    ones_bd = (heads[:, None] == heads[None, :]).astype(BF16)
    dft_ch = _channel_dft(n_groups)

    c_rows = jnp.zeros((MOD_ROWS, d), F32).at[:b].set(c).at[b].set(c_ctx)
    mod_all = _modulation(c_rows, w_mod, b_mod)

    n_tok_cols = w_in.shape[-1] - N_BRANCHES * d

    xc = ctx
    for li in range(depth):
        lam_init = 0.8 - 0.6 * math.exp(-0.3 * li)
        row = lambda a: a[li].reshape(1, -1)
        lw = dict(
            w_a=w_in[li, :, :n_tok_cols].astype(BF16), w_gate=w_in[li, :, n_tok_cols:].astype(BF16),
            g_pre_mix=row(g_pre_mix), g_post_mix=row(g_post_mix), g_pre_mlp=row(g_pre_mlp),
            g_post_mlp=row(g_post_mlp),
            qn=jnp.tile(q_norm[li], dims["gqa_q"] // HEAD_DIM).reshape(1, -1),
            kn=jnp.tile(k_norm[li], dims["gqa_kv"] // HEAD_DIM).reshape(1, -1),
            lam_p=diff_lambda[li], subln=row(diff_subln),
            conv_w=conv_dw[li], conv_b=row(conv_dw_bias), conv_ln_g=row(conv_ln_g), conv_ln_b=row(conv_ln_b),
            w_br_f=w_br_fourier[li].astype(BF16), w_br_d=w_br_diff[li].astype(BF16),
            w_br_g=w_br_gqa[li].astype(BF16), w_br_c=w_br_conv[li].astype(BF16), w_out=w_out[li].astype(BF16),
            w_ff1=w_ff1[li].astype(BF16), w_ff2=w_ff2[li].astype(BF16))
        mod_x = mod_all[li, :b].reshape(b, 1, N_MOD * d)
        mod_c = jnp.broadcast_to(mod_all[li, b].reshape(1, 1, N_MOD * d), (b, 1, N_MOD * d))
        last = li == depth - 1

        inproj = functools.partial(_inproj, g_pre=lw["g_pre_mix"], w_a=lw["w_a"], ones_bd=ones_bd, qn=lw["qn"],
                                   kn=lw["kn"], dft_ch=dft_ch, dims=dims, lk=lk)
        kv_bufs = (jnp.zeros((b, dims["diff_qk"], lk), BF16), jnp.zeros((b, dims["gqa_kv"], lk), BF16),
                   jnp.zeros((b, lk, dims["diff_v"]), BF16), jnp.zeros((b, lk, dims["gqa_kv"]), BF16))
        tok_c, kv_bufs = inproj(xc, mod_c, cos_t=rope_none[0], sin_t=rope_none[1], kv_bufs=kv_bufs, kv_only=last,
                                row_start=l, tm=lc)
        tok_x, kv_bufs = inproj(x, mod_x, cos_t=rope_x[0], sin_t=rope_x[1], kv_bufs=kv_bufs, kv_only=False,
                                row_start=0, tm=512)
        mix = functools.partial(_mix_and_mlp, lw=lw, kv_bufs=kv_bufs, dims=dims, lam_init=lam_init)
        x = mix(x, mod_x, tok=tok_x, fourier=fourier_x, key_start=0, key_len=lk, tm_mlp=512)
        if not last:
            xc = mix(xc, mod_c, tok=tok_c, fourier=fourier_c, key_start=l, key_len=lc, tm_mlp=lc)
    return x
```

```python
import functools
import math

import jax
import jax.numpy as jnp
from jax import lax
from jax.experimental import pallas as pl
from jax.experimental.pallas import tpu as pltpu

F32 = jnp.float32
BF16 = jnp.bfloat16

GRID_W = 64
HEAD_DIM = 64
ROPE_FREQS = HEAD_DIM // 4
ROPE_THETA = 10000.0
EPS = 1e-6
ATTN_SCALE = HEAD_DIM ** -0.5
Q_PRESCALE = ATTN_SCALE * math.log2(math.e)
FOURIER_GROUP_DIM = 64
CONV_KERNEL = 31
CONV_HALO = 16
N_BRANCHES = 4
N_MOD = 6
MOD_ROWS = 8

V7X_VMEM_LIMIT_BYTES = 56 * 1024 * 1024
V7X_MXU_COLUMNS = 256
SUBLANES = 8
BF16_SUBLANES = 16
FOURIER_STAGE_LEN = 64


def _dims(d_model):
    fourier = 3 * d_model // 8
    diff_heads = d_model // 256
    diff_qk = diff_heads * 2 * HEAD_DIM
    diff_v = diff_heads * 2 * HEAD_DIM
    gqa_q = (d_model // 128) * HEAD_DIM
    gqa_kv = (d_model // 128 // 4) * HEAD_DIM
    conv = 3 * d_model // 8
    return dict(fourier=fourier, diff_heads=diff_heads, diff_qk=diff_qk, diff_v=diff_v,
                gqa_q=gqa_q, gqa_kv=gqa_kv, conv=conv)


def _params(semantics):
    return pltpu.CompilerParams(dimension_semantics=semantics, vmem_limit_bytes=V7X_VMEM_LIMIT_BYTES)


def _mod_kernel(c_ref, w_ref, b_ref, o_ref):
    c = c_ref[...]
    a = c * jax.nn.sigmoid(c)
    o_ref[...] = jnp.dot(a, w_ref[...], preferred_element_type=F32,
                         precision=lax.Precision.HIGHEST) + b_ref[...]


def _modulation(c_rows, w_mod, b_mod):
    depth, d, n = w_mod.shape
    tn = 768
    return pl.pallas_call(
        _mod_kernel,
        grid=(depth, n // tn),
        in_specs=[pl.BlockSpec((MOD_ROWS, d), lambda l, j: (0, 0)),
                  pl.BlockSpec((None, d, tn), lambda l, j: (l, 0, j)),
                  pl.BlockSpec((None, 1, tn), lambda l, j: (l, 0, j))],
        out_specs=pl.BlockSpec((None, MOD_ROWS, tn), lambda l, j: (l, 0, j)),
        out_shape=jax.ShapeDtypeStruct((depth, MOD_ROWS, n), F32),
        compiler_params=_params(("arbitrary", "arbitrary")),
        name="modulation",
    )(c_rows, w_mod, b_mod.reshape(depth, 1, n))


def _modulated(x, g, shift, scale):
    h = x * lax.rsqrt(jnp.mean(x * x, axis=-1, keepdims=True) + EPS) * g
    return h * (1.0 + scale) + shift


def _rms(x, g):
    return x * lax.rsqrt(jnp.mean(x * x, axis=-1, keepdims=True) + EPS) * g


def _swap_halves(t):
    w = t.shape[-1]
    lane = lax.broadcasted_iota(jnp.int32, t.shape, 1)
    from_right = pltpu.roll(t, w - ROPE_FREQS, 1)
    from_left = pltpu.roll(t, ROPE_FREQS, 1)
    return jnp.where((lane % (2 * ROPE_FREQS)) < ROPE_FREQS, from_right, from_left)


def _tile_lanes(t, width):
    reps = width // t.shape[-1]
    return t if reps == 1 else jnp.concatenate([t] * reps, axis=-1)


def _head_norm(t, ones_bd, gain):
    w = t.shape[-1]
    t2 = t * t
    hi = t2.astype(BF16)
    lo = (t2 - hi.astype(F32)).astype(BF16)
    e = ones_bd[:w, :w]
    ss = (jnp.dot(hi, e, preferred_element_type=F32) + jnp.dot(lo, e, preferred_element_type=F32))
    return t * lax.rsqrt(ss * (1.0 / HEAD_DIM) + EPS) * gain


def _inproj_kernel(x_ref, shift_ref, scale_ref, g_ref, w_ref, ones_ref, cos_ref, sin_ref, qn_ref, kn_ref,
                   dft_ref, *rest, dims, kv_only, n_alias):
    out_refs = rest[n_alias:]
    nq, nkv, nv, nf, nc = dims["diff_qk"], dims["gqa_kv"], dims["diff_v"], dims["fourier"], dims["conv"]
    ngq = dims["gqa_q"]
    h = _modulated(x_ref[0], g_ref[...], shift_ref[0], scale_ref[0]).astype(BF16)

    def proj(start, width):
        return jnp.dot(h, w_ref[:, start:start + width], preferred_element_type=F32)

    def rope(t):
        w = t.shape[-1]
        return t * _tile_lanes(cos_ref[...], w) + _swap_halves(t) * _tile_lanes(sin_ref[...], w)

    ones_bd = ones_ref[...]
    off = 0
    dk = proj(off, nq); off += nq
    dv = proj(off, nv); off += nv
    gk = proj(off, nkv); off += nkv
    gv = proj(off, nkv); off += nkv
    kdt_ref, kgt_ref, vd_ref, vg_ref = out_refs[-4:]
    kdt_ref[0] = rope(dk).T.astype(BF16)
    kgt_ref[0] = rope(_head_norm(gk, ones_bd, kn_ref[...])).T.astype(BF16)
    vd_ref[0] = dv.astype(BF16)
    vg_ref[0] = gv.astype(BF16)
    if not kv_only:
        qd_ref, qg_ref, xf_ref, vc_ref = out_refs[:4]
        uf = proj(off, nf).astype(BF16); off += nf
        xf_ref[0] = jnp.dot(uf, dft_ref[...], preferred_element_type=F32).astype(BF16)
        dq = proj(off, nq); off += nq
        qd_ref[0] = (rope(dq) * Q_PRESCALE).astype(BF16)
        gq = proj(off, ngq); off += ngq
        qg_ref[0] = (rope(_head_norm(gq, ones_bd, qn_ref[...])) * Q_PRESCALE).astype(BF16)
        glu = proj(off, 2 * nc)
        vc_ref[0] = glu[:, :nc] * jax.nn.sigmoid(glu[:, nc:])


def _inproj(x, mod, g_pre, w_a, ones_bd, cos_t, sin_t, qn, kn, dft_ch, kv_bufs, *, dims, kv_only, row_start, lk,
            tm):
    b, l, d = x.shape
    tm = min(tm, l)
    nq, nkv, nv, nf, nc, ngq = (dims["diff_qk"], dims["gqa_kv"], dims["diff_v"], dims["fourier"],
                                dims["conv"], dims["gqa_q"])
    blk0 = row_start // tm
    assert row_start % tm == 0
    tok = lambda w: pl.BlockSpec((1, tm, w), lambda bi, i: (bi, i, 0))
    const = lambda shape: pl.BlockSpec(shape, lambda bi, i: (0,) * len(shape))
    tok_outs = [] if kv_only else [((b, l, nq), BF16), ((b, l, ngq), BF16), ((b, l, 2 * nf), BF16), ((b, l, nc), F32)]
    kv_shapes = [(a.shape, a.dtype) for a in kv_bufs]
    kv_specs = [pl.BlockSpec((1, nq, tm), lambda bi, i: (bi, 0, blk0 + i)),
                pl.BlockSpec((1, nkv, tm), lambda bi, i: (bi, 0, blk0 + i)),
                pl.BlockSpec((1, tm, nv), lambda bi, i: (bi, blk0 + i, 0)),
                pl.BlockSpec((1, tm, nkv), lambda bi, i: (bi, blk0 + i, 0))]
    n_alias = len(kv_bufs)
    n_in = 11
    aliases = {n_in + k: len(tok_outs) + k for k in range(n_alias)}
    outs = pl.pallas_call(
        functools.partial(_inproj_kernel, dims=dims, kv_only=kv_only, n_alias=n_alias),
        grid=(b, l // tm),
        in_specs=[tok(d),
                  pl.BlockSpec((1, 1, d), lambda bi, i: (bi, 0, 0)),
                  pl.BlockSpec((1, 1, d), lambda bi, i: (bi, 0, 1)),
                  const((1, d)),
                  const(w_a.shape),
                  const(ones_bd.shape),
                  pl.BlockSpec((tm, 2 * HEAD_DIM), lambda bi, i: (i, 0)),
                  pl.BlockSpec((tm, 2 * HEAD_DIM), lambda bi, i: (i, 0)),
                  const(qn.shape), const(kn.shape), const(dft_ch.shape)]
                 + [pl.BlockSpec(memory_space=pl.ANY)] * n_alias,
        out_specs=[tok(s[-1]) for s, _ in tok_outs] + kv_specs,
        out_shape=[jax.ShapeDtypeStruct(s, dt) for s, dt in tok_outs + kv_shapes],
        input_output_aliases=aliases,
        compiler_params=_params(("arbitrary", "arbitrary")),
        name="inproj_kv" if kv_only else "inproj",
    )(x, mod, mod, g_pre, w_a, ones_bd, cos_t, sin_t, qn, kn, dft_ch, *kv_bufs)
    return outs[:len(tok_outs)], outs[len(tok_outs):]


def _conv_ln_silu(win_ref, w_ref, b_ref, g_ref, beta_ref, yc_ref, *, rows, chunk):
    first = CONV_HALO - CONV_KERNEL // 2
    for r0 in range(0, rows, chunk):
        acc = jnp.zeros((chunk, w_ref.shape[-1]), F32) + b_ref[...]
        for phase in range(SUBLANES):
            part = None
            for t in range(CONV_KERNEL):
                if (t + first) % SUBLANES != phase:
                    continue
                base = r0 + t + first - phase
                term = w_ref[t:t + 1, :] * win_ref[base:base + chunk + SUBLANES, :]
                part = term if part is None else part + term
            acc = acc + part[phase:phase + chunk, :]
        mu = jnp.mean(acc, axis=-1, keepdims=True)
        cen = acc - mu
        var = jnp.mean(cen * cen, axis=-1, keepdims=True)
        y = cen * lax.rsqrt(var + EPS) * g_ref[...] + beta_ref[...]
        yc_ref[r0:r0 + chunk, :] = (y * jax.nn.sigmoid(y)).astype(BF16)


def _softmax_pv(q, k_t, v_ext):
    s = jnp.dot(q, k_t, preferred_element_type=F32)
    e = jnp.exp2(s - jnp.max(s, axis=-1, keepdims=True))
    return jnp.dot(e.astype(BF16), v_ext, preferred_element_type=F32)


def _conv_kernel(vprev_ref, vcur_ref, vnext_ref, cw_ref, cb_ref, cg_ref, cbeta_ref, yc_ref, win_ref):
    i = pl.program_id(1)
    rows = vcur_ref.shape[1]
    win_ref[:CONV_HALO, :] = jnp.where(i > 0, vprev_ref[0], 0.0)
    win_ref[CONV_HALO:CONV_HALO + rows, :] = vcur_ref[0]
    win_ref[CONV_HALO + rows:, :] = jnp.where(i < pl.num_programs(1) - 1, vnext_ref[0], 0.0)
    _conv_ln_silu(win_ref, cw_ref, cb_ref, cg_ref, cbeta_ref, yc_ref.at[0], rows=rows, chunk=32)


def _conv(vc, conv_w, conv_b, conv_g, conv_beta, *, rows):
    b, l, c = vc.shape
    rows = min(rows, l)
    hb = rows // CONV_HALO
    n_hb = l // CONV_HALO
    const = lambda a: pl.BlockSpec(a.shape, lambda bi, i: (0,) * a.ndim)
    prev = pl.BlockSpec((1, CONV_HALO, c), lambda bi, i: (bi, jnp.maximum(i * hb - 1, 0), 0))
    nxt = pl.BlockSpec((1, CONV_HALO, c), lambda bi, i: (bi, jnp.minimum((i + 1) * hb, n_hb - 1), 0))
    tile = pl.BlockSpec((1, rows, c), lambda bi, i: (bi, i, 0))
    return pl.pallas_call(
        _conv_kernel,
        grid=(b, l // rows),
        in_specs=[prev, tile, nxt, const(conv_w), const(conv_b), const(conv_g), const(conv_beta)],
        out_specs=tile,
        out_shape=jax.ShapeDtypeStruct((b, l, c), BF16),
        scratch_shapes=[pltpu.VMEM((rows + 2 * CONV_HALO, c), F32)],
        compiler_params=_params(("arbitrary", "arbitrary")),
        name="conv",
    )(vc, vc, vc, conv_w, conv_b, conv_g, conv_beta)


def _attn_kernel(qd_ref, qg_ref, kdt_ref, vd_ref, kgt_ref, vg_ref, lam_ref, subln_ref, od_ref, og_ref,
                 vdx_ref, vgx_ref, *, n_diff, n_gq, n_kv, lam_init):
    dh = HEAD_DIM
    dv = 2 * HEAD_DIM

    @pl.when(pl.program_id(1) == 0)
    def _():
        for hd in range(n_diff):
            vdx_ref[hd, :, :dv] = vd_ref[0, :, hd * dv:(hd + 1) * dv]
            vdx_ref[hd, :, dv:] = jnp.ones((vd_ref.shape[1], V7X_MXU_COLUMNS - dv), BF16)
        vgx_ref[:, :n_kv * dh] = vg_ref[0]
        vgx_ref[:, n_kv * dh:] = jnp.ones((vg_ref.shape[1], V7X_MXU_COLUMNS - n_kv * dh), BF16)

    lv = lam_ref[...]
    lam = (jnp.exp(jnp.sum(lv[0:1] * lv[1:2], axis=-1, keepdims=True))
           - jnp.exp(jnp.sum(lv[2:3] * lv[3:4], axis=-1, keepdims=True)) + lam_init)
    for hd in range(n_diff):
        maps = []
        for j in range(2):
            c = (2 * hd + j) * dh
            pv = _softmax_pv(qd_ref[0, :, c:c + dh], kdt_ref[0, c:c + dh, :], vdx_ref[hd])
            maps.append(pv[:, :dv] / pv[:, dv:])
        o = maps[0] - lam * maps[1]
        od_ref[0, :, hd * dv:(hd + 1) * dv] = (_rms(o, subln_ref[...]) * (1.0 - lam_init)).astype(BF16)
    group = n_gq // n_kv
    lane = lax.broadcasted_iota(jnp.int32, (qg_ref.shape[1], dv), 1)
    for pair in range(n_gq // 2):
        kv = (2 * pair) // group
        halves = []
        for hq in (2 * pair, 2 * pair + 1):
            pv = _softmax_pv(qg_ref[0, :, hq * dh:(hq + 1) * dh], kgt_ref[0, kv * dh:(kv + 1) * dh, :], vgx_ref[...])
            halves.append(pv[:, :dv] / pv[:, dv:])
        left = halves[0] if kv % 2 == 0 else pltpu.roll(halves[0], dh, 1)
        right = pltpu.roll(halves[1], dh, 1) if kv % 2 == 0 else halves[1]
        og_ref[0, :, pair * dv:(pair + 1) * dv] = jnp.where(lane < dh, left, right).astype(BF16)


def _attention(qd, qg, kv_bufs, lam_p, subln, *, dims, lam_init, tq, key_start, key_len):
    kd_t, kg_t, vd, vg = kv_bufs
    b, l, nq = qd.shape
    n_diff, n_gq, n_kv = dims["diff_heads"], dims["gqa_q"] // HEAD_DIM, dims["gqa_kv"] // HEAD_DIM
    assert n_kv * HEAD_DIM == 2 * HEAD_DIM and (n_gq // n_kv) % 2 == 0 and key_start % key_len == 0
    kb = key_start // key_len
    tq = min(tq, l)
    tok = lambda w: pl.BlockSpec((1, tq, w), lambda bi, i: (bi, i, 0))
    keys_t = lambda a: pl.BlockSpec((1, a.shape[1], key_len), lambda bi, i: (bi, 0, kb))
    vals = lambda a: pl.BlockSpec((1, key_len, a.shape[2]), lambda bi, i: (bi, kb, 0))
    const = lambda a: pl.BlockSpec(a.shape, lambda bi, i: (0,) * a.ndim)
    return pl.pallas_call(
        functools.partial(_attn_kernel, n_diff=n_diff, n_gq=n_gq, n_kv=n_kv, lam_init=lam_init),
        grid=(b, l // tq),
        in_specs=[tok(nq), tok(qg.shape[-1]), keys_t(kd_t), vals(vd), keys_t(kg_t), vals(vg),
                  const(lam_p), const(subln)],
        out_specs=[tok(dims["diff_v"]), tok(dims["gqa_q"])],
        out_shape=[jax.ShapeDtypeStruct((b, l, dims["diff_v"]), BF16),
                   jax.ShapeDtypeStruct((b, l, dims["gqa_q"]), BF16)],
        scratch_shapes=[pltpu.VMEM((n_diff, key_len, V7X_MXU_COLUMNS), BF16),
                        pltpu.VMEM((key_len, V7X_MXU_COLUMNS), BF16)],
        compiler_params=_params(("arbitrary", "arbitrary")),
        name="attention",
    )(qd, qg, kd_t, vd, kg_t, vg, lam_p, subln)


def _fourier_kernel(c_ref, s_ref, xf_ref, o_ref, *, nf, out_scale):
    y = (jnp.dot(c_ref[...], xf_ref[0, :, :nf], preferred_element_type=F32)
         - jnp.dot(s_ref[...], xf_ref[0, :, nf:], preferred_element_type=F32))
    o_ref[0] = (y * out_scale).astype(BF16)


def _fourier_dense(tables, xf, *, tr=512):
    dft_c, dft_s = tables
    b, l, nf2 = xf.shape
    nf = nf2 // 2
    tr = min(tr, l)
    return pl.pallas_call(
        functools.partial(_fourier_kernel, nf=nf, out_scale=float(l) ** -0.5),
        grid=(b, l // tr),
        in_specs=[pl.BlockSpec((tr, l), lambda bi, i: (i, 0)),
                  pl.BlockSpec((tr, l), lambda bi, i: (i, 0)),
                  pl.BlockSpec((1, l, nf2), lambda bi, i: (bi, 0, 0))],
        out_specs=pl.BlockSpec((1, tr, nf), lambda bi, i: (bi, i, 0)),
        out_shape=jax.ShapeDtypeStruct((b, l, nf), BF16),
        compiler_params=_params(("arbitrary", "arbitrary")),
        name="fourier",
    )(dft_c, dft_s, xf)


def _fft_a_kernel(z_ref, m_ref, cos_ref, sin_ref, o_ref, *, nf):
    n1, r = z_ref.shape[1], z_ref.shape[2]
    rows = n1 * r
    z = z_ref[0].reshape(rows, 2 * nf)
    t = jnp.dot(m_ref[...], z, preferred_element_type=F32)
    cz, sz = t[:rows], t[rows:]
    ar = cz[:, :nf] - sz[:, nf:]
    ai = -(sz[:, :nf] + cz[:, nf:])
    ct = _tile_lanes(cos_ref[0], nf)
    st = _tile_lanes(sin_ref[0], nf)
    o_ref[0, :, :, :nf] = (ar * ct + ai * st).reshape(n1, r, nf).astype(BF16)
    o_ref[0, :, :, nf:] = (ai * ct - ar * st).reshape(n1, r, nf).astype(BF16)


def _fft_c_kernel(b_ref, mc_ref, ms_ref, o_ref, *, nf, out_scale):
    r, n2 = b_ref.shape[1], b_ref.shape[2]
    blk = b_ref[0].reshape(r * n2, 2 * nf)
    y = (jnp.dot(mc_ref[...], blk[:, :nf], preferred_element_type=F32)
         + jnp.dot(ms_ref[...], blk[:, nf:], preferred_element_type=F32))
    o_ref[0] = (y * out_scale).reshape(n2, r, nf).astype(BF16)


def _fourier_two_stage(tables, xf):
    m_a, tw_cos, tw_sin, m_c_cos, m_c_sin = tables
    b, l, nf2 = xf.shape
    nf = nf2 // 2
    r = BF16_SUBLANES
    n2 = m_c_cos.shape[0] // r
    n1 = l // n2
    tile = lambda w: pl.BlockSpec((1, n1, r, w), lambda bi, s: (bi, 0, s, 0))
    const = lambda a: pl.BlockSpec(a.shape, lambda bi, s: (0,) * a.ndim)
    twiddle = pl.BlockSpec((1, n1 * r, 2 * HEAD_DIM), lambda bi, s: (s, 0, 0))
    staged = pl.pallas_call(
        functools.partial(_fft_a_kernel, nf=nf),
        grid=(b, n2 // r),
        in_specs=[tile(nf2), const(m_a), twiddle, twiddle],
        out_specs=tile(nf2),
        out_shape=jax.ShapeDtypeStruct((b, n1, n2, nf2), BF16),
        compiler_params=_params(("arbitrary", "arbitrary")),
        name="fourier_a",
    )(xf.reshape(b, n1, n2, nf2), m_a, tw_cos, tw_sin)
    out = pl.pallas_call(
        functools.partial(_fft_c_kernel, nf=nf, out_scale=float(l) ** -0.5),
        grid=(b, n1 // r),
        in_specs=[pl.BlockSpec((1, r, n2, nf2), lambda bi, s: (bi, s, 0, 0)), const(m_c_cos), const(m_c_sin)],
        out_specs=pl.BlockSpec((1, n2, r, nf), lambda bi, s: (bi, 0, s, 0)),
        out_shape=jax.ShapeDtypeStruct((b, n2, n1, nf), BF16),
        compiler_params=_params(("arbitrary", "arbitrary")),
        name="fourier_c",
    )(staged, m_c_cos, m_c_sin)
    return out.reshape(b, l, nf)


def _merge_kernel(x_ref, shift_ref, scale_ref, gate_ref, g_ref, wgate_ref, yf_ref, od_ref, og_ref, yc_ref,
                  wf_ref, wd_ref, wg_ref, wc_ref, wout_ref, gpost_ref, o_ref):
    x = x_ref[0]
    d = x.shape[-1]
    h = _modulated(x, g_ref[...], shift_ref[0], scale_ref[0]).astype(BF16)
    branches = ((yf_ref, wf_ref), (od_ref, wd_ref), (og_ref, wg_ref), (yc_ref, wc_ref))
    merged = None
    for k, (y_ref, w_ref) in enumerate(branches):
        gate = jax.nn.sigmoid(jnp.dot(h, wgate_ref[:, k * d:(k + 1) * d], preferred_element_type=F32))
        term = gate * jnp.dot(y_ref[0], w_ref[...], preferred_element_type=F32)
        merged = term if merged is None else merged + term
    mix = jnp.dot(merged.astype(BF16), wout_ref[...], preferred_element_type=F32)
    o_ref[0] = x + gate_ref[0] * _rms(mix, gpost_ref[...])


def _merge(x, mod, g_pre, w_gate, yf, od, og, yc, wf, wd, wg, wc, wout, g_post, *, tm):
    b, l, d = x.shape
    tm = min(tm, l)
    tok = lambda w: pl.BlockSpec((1, tm, w), lambda bi, i: (bi, i, 0))
    modc = lambda k: pl.BlockSpec((1, 1, d), lambda bi, i: (bi, 0, k))
    const = lambda a: pl.BlockSpec(a.shape, lambda bi, i: (0,) * a.ndim)
    return pl.pallas_call(
        _merge_kernel,
        grid=(b, l // tm),
        in_specs=[tok(d), modc(0), modc(1), modc(2), const(g_pre), const(w_gate),
                  tok(yf.shape[-1]), tok(od.shape[-1]), tok(og.shape[-1]), tok(yc.shape[-1]),
                  const(wf), const(wd), const(wg), const(wc), const(wout), const(g_post)],
        out_specs=tok(d),
        out_shape=jax.ShapeDtypeStruct((b, l, d), F32),
        compiler_params=_params(("arbitrary", "arbitrary")),
        name="merge",
    )(x, mod, mod, mod, g_pre, w_gate, yf, od, og, yc, wf, wd, wg, wc, wout, g_post)


def _mlp_kernel(x_ref, shift_ref, scale_ref, gate_ref, g_ref, w1_ref, w2_ref, gpost_ref, o_ref, *, ff_chunk):
    x = x_ref[0]
    h = _modulated(x, g_ref[...], shift_ref[0], scale_ref[0]).astype(BF16)
    acc = None
    for c0 in range(0, w1_ref.shape[-1], ff_chunk):
        u = jnp.maximum(jnp.dot(h, w1_ref[:, c0:c0 + ff_chunk], preferred_element_type=F32), 0.0)
        part = jnp.dot((u * u).astype(BF16), w2_ref[c0:c0 + ff_chunk, :], preferred_element_type=F32)
        acc = part if acc is None else acc + part
    o_ref[0] = x + gate_ref[0] * _rms(acc, gpost_ref[...])


def _mlp(x, mod, g_pre, w1, w2, g_post, *, tm):
    b, l, d = x.shape
    tm = min(tm, l)
    tok = lambda w: pl.BlockSpec((1, tm, w), lambda bi, i: (bi, i, 0))
    modc = lambda k: pl.BlockSpec((1, 1, d), lambda bi, i: (bi, 0, k))
    const = lambda a: pl.BlockSpec(a.shape, lambda bi, i: (0,) * a.ndim)
    return pl.pallas_call(
        functools.partial(_mlp_kernel, ff_chunk=1024),
        grid=(b, l // tm),
        in_specs=[tok(d), modc(3), modc(4), modc(5), const(g_pre), const(w1), const(w2), const(g_post)],
        out_specs=tok(d),
        out_shape=jax.ShapeDtypeStruct((b, l, d), F32),
        compiler_params=_params(("arbitrary", "arbitrary")),
        name="mlp",
    )(x, mod, mod, mod, g_pre, w1, w2, g_post)


def _rope_tables(seq_len):
    rows = seq_len // GRID_W
    row = jnp.repeat(jnp.arange(rows, dtype=F32), GRID_W)
    col = jnp.tile(jnp.arange(GRID_W, dtype=F32), rows)
    inv_freq = ROPE_THETA ** (-jnp.arange(ROPE_FREQS, dtype=F32) / ROPE_FREQS)
    ang_r, ang_c = row[:, None] * inv_freq, col[:, None] * inv_freq
    cos64 = jnp.concatenate([jnp.cos(ang_r)] * 2 + [jnp.cos(ang_c)] * 2, axis=-1)
    sin64 = jnp.concatenate([-jnp.sin(ang_r), jnp.sin(ang_r), -jnp.sin(ang_c), jnp.sin(ang_c)], axis=-1)
    return jnp.concatenate([cos64] * 2, axis=-1), jnp.concatenate([sin64] * 2, axis=-1)


def _seq_dft(l):
    f = min(64, l)
    a_n = l // f
    k = jnp.arange(l, dtype=jnp.int32)[:, None]
    th1 = ((k * jnp.arange(a_n, dtype=jnp.int32)[None, :]) % a_n).astype(F32) * (2.0 * math.pi / a_n)
    th2 = ((k * jnp.arange(f, dtype=jnp.int32)[None, :]) % l).astype(F32) * (2.0 * math.pi / l)
    c1, s1, c2, s2 = jnp.cos(th1)[:, :, None], jnp.sin(th1)[:, :, None], jnp.cos(th2)[:, None, :], jnp.sin(th2)[:, None, :]
    c = (c1 * c2 - s1 * s2).reshape(l, l)
    s = (s1 * c2 + c1 * s2).reshape(l, l)
    return c.astype(BF16), s.astype(BF16)


def _two_stage_dft(l, n2):
    n1 = l // n2
    r = BF16_SUBLANES
    angle = lambda num, den: (num % den).astype(F32) * (2.0 * math.pi / den)
    i1 = jnp.arange(n1, dtype=jnp.int32)
    i2 = jnp.arange(n2, dtype=jnp.int32)
    eye = jnp.eye(r, dtype=F32)
    th_a = angle(i1[:, None] * i1[None, :], n1)
    m_a = jnp.concatenate([jnp.kron(jnp.cos(th_a), eye), jnp.kron(jnp.sin(th_a), eye)], axis=0).astype(BF16)
    th_t = angle(i1[:, None] * i2[None, :], l).reshape(n1, n2 // r, r)
    th_t = jnp.transpose(th_t, (1, 0, 2)).reshape(n2 // r, n1 * r)
    lanes = lambda t: jnp.broadcast_to(t[:, :, None], t.shape + (2 * HEAD_DIM,))
    th_c = angle(i2[:, None] * i2[None, :], n2)
    perm = lambda w: jnp.einsum('kb,ij->kijb', w, eye).reshape(n2 * r, r * n2).astype(BF16)
    return m_a, lanes(jnp.cos(th_t)), lanes(jnp.sin(th_t)), perm(jnp.cos(th_c)), perm(jnp.sin(th_c))


def _channel_dft(n_groups):
    g = FOURIER_GROUP_DIM
    idx = jnp.arange(g, dtype=jnp.int32)
    th = ((idx[:, None] * idx[None, :]) % g).astype(F32) * (2.0 * math.pi / g)
    eye = jnp.eye(n_groups, dtype=F32)
    scale = float(g) ** -0.5
    return jnp.concatenate([jnp.kron(eye, jnp.cos(th) * scale), jnp.kron(eye, jnp.sin(th) * scale)],
                           axis=-1).astype(BF16)


def _seq_fourier(l):
    n2 = FOURIER_STAGE_LEN
    if l % n2 == 0 and (l // n2) % BF16_SUBLANES == 0:
        return functools.partial(_fourier_two_stage, _two_stage_dft(l, n2))
    return functools.partial(_fourier_dense, _seq_dft(l))


def _mix_and_mlp(x, mod, lw, tok, kv_bufs, fourier, *, dims, lam_init, key_start, key_len, tm_mlp):
    qd, qg, xf, vc = tok
    od, og = _attention(qd, qg, kv_bufs, lw["lam_p"], lw["subln"], dims=dims, lam_init=lam_init, tq=256,
                        key_start=key_start, key_len=key_len)
    yf = fourier(xf)
    yc = _conv(vc, lw["conv_w"], lw["conv_b"], lw["conv_ln_g"], lw["conv_ln_b"], rows=512)
    x = _merge(x, mod, lw["g_pre_mix"], lw["w_gate"], yf, od, og, yc, lw["w_br_f"], lw["w_br_d"], lw["w_br_g"],
               lw["w_br_c"], lw["w_out"], lw["g_post_mix"], tm=256)
    return _mlp(x, mod, lw["g_pre_mlp"], lw["w_ff1"], lw["w_ff2"], lw["g_post_mlp"], tm=tm_mlp)


def kernel(x, c, ctx, c_ctx, w_mod, b_mod, g_pre_mix, g_post_mix, g_pre_mlp, g_post_mlp, w_in, q_norm, k_norm,
           diff_lambda, diff_subln, conv_dw, conv_dw_bias, conv_ln_g, conv_ln_b, w_br_fourier, w_br_diff,
           w_br_gqa, w_br_conv, w_out, w_ff1, w_ff2):
    b, l, d = x.shape
    lc = ctx.shape[1]
    lk = l + lc
    depth = w_in.shape[0]
    dims = _dims(d)
    assert b + 1 <= MOD_ROWS and l % GRID_W == 0 and l % lc == 0

    rope_x = _rope_tables(l)
    rope_none = (jnp.ones((lc, 2 * HEAD_DIM), F32), jnp.zeros((lc, 2 * HEAD_DIM), F32))
    fourier_x, fourier_c = _seq_fourier(l), _seq_fourier(lc)
    n_groups = dims["fourier"] // FOURIER_GROUP_DIM
    heads = jnp.arange(dims["diff_qk"], dtype=jnp.int32) // HEAD_DIM
    ones_bd = (heads[:, None] == heads[None, :]).astype(BF16)
    dft_ch = _channel_dft(n_groups)

    c_rows = jnp.zeros((MOD_ROWS, d), F32).at[:b].set(c).at[b].set(c_ctx)
    mod_all = _modulation(c_rows, w_mod, b_mod)

    n_tok_cols = w_in.shape[-1] - N_BRANCHES * d

    xc = ctx
    for li in range(depth):
        lam_init = 0.8 - 0.6 * math.exp(-0.3 * li)
        row = lambda a: a[li].reshape(1, -1)
        lw = dict(
            w_a=w_in[li, :, :n_tok_cols].astype(BF16), w_gate=w_in[li, :, n_tok_cols:].astype(BF16),
            g_pre_mix=row(g_pre_mix), g_post_mix=row(g_post_mix), g_pre_mlp=row(g_pre_mlp),
            g_post_mlp=row(g_post_mlp),
            qn=jnp.tile(q_norm[li], dims["gqa_q"] // HEAD_DIM).reshape(1, -1),
            kn=jnp.tile(k_norm[li], dims["gqa_kv"] // HEAD_DIM).reshape(1, -1),
            lam_p=diff_lambda[li], subln=row(diff_subln),
            conv_w=conv_dw[li], conv_b=row(conv_dw_bias), conv_ln_g=row(conv_ln_g), conv_ln_b=row(conv_ln_b),
            w_br_f=w_br_fourier[li].astype(BF16), w_br_d=w_br_diff[li].astype(BF16),
            w_br_g=w_br_gqa[li].astype(BF16), w_br_c=w_br_conv[li].astype(BF16), w_out=w_out[li].astype(BF16),
            w_ff1=w_ff1[li].astype(BF16), w_ff2=w_ff2[li].astype(BF16))
        mod_x = mod_all[li, :b].reshape(b, 1, N_MOD * d)
        mod_c = jnp.broadcast_to(mod_all[li, b].reshape(1, 1, N_MOD * d), (b, 1, N_MOD * d))
        last = li == depth - 1

        inproj = functools.partial(_inproj, g_pre=lw["g_pre_mix"], w_a=lw["w_a"], ones_bd=ones_bd, qn=lw["qn"],
                                   kn=lw["kn"], dft_ch=dft_ch, dims=dims, lk=lk)
        kv_bufs = (jnp.zeros((b, dims["diff_qk"], lk), BF16), jnp.zeros((b, dims["gqa_kv"], lk), BF16),
                   jnp.zeros((b, lk, dims["diff_v"]), BF16), jnp.zeros((b, lk, dims["gqa_kv"]), BF16))
        tok_c, kv_bufs = inproj(xc, mod_c, cos_t=rope_none[0], sin_t=rope_none[1], kv_bufs=kv_bufs, kv_only=last,
                                row_start=l, tm=lc)
        tok_x, kv_bufs = inproj(x, mod_x, cos_t=rope_x[0], sin_t=rope_x[1], kv_bufs=kv_bufs, kv_only=False,
                                row_start=0, tm=512)
        mix = functools.partial(_mix_and_mlp, lw=lw, kv_bufs=kv_bufs, dims=dims, lam_init=lam_init)
        x = mix(x, mod_x, tok=tok_x, fourier=fourier_x, key_start=0, key_len=lk, tm_mlp=512)
        if not last:
            xc = mix(xc, mod_c, tok=tok_c, fourier=fourier_c, key_start=l, key_len=lc, tm_mlp=lc)
    return x
```

```python
import functools
import math

import jax
import jax.numpy as jnp
import numpy as np
from jax import lax
from jax.experimental import pallas as pl
from jax.experimental.pallas import tpu as pltpu

F32 = jnp.float32
BF16 = jnp.bfloat16

GRID_W = 64
HEAD_DIM = 64
ROPE_FREQS = HEAD_DIM // 4
ROPE_THETA = 10000.0
EPS = 1e-6
ATTN_SCALE = HEAD_DIM ** -0.5
Q_PRESCALE = ATTN_SCALE * math.log2(math.e)
FOURIER_GROUP_DIM = 64
CONV_KERNEL = 31
CONV_HALO = 16
N_BRANCHES = 4
N_MOD = 6
MOD_ROWS = 8

V7X_VMEM_LIMIT_BYTES = 56 * 1024 * 1024
V7X_MXU_COLUMNS = 256
SUBLANES = 8
BF16_SUBLANES = 16
FOURIER_STAGE_LEN = 64


def _dims(d_model):
    fourier = 3 * d_model // 8
    diff_heads = d_model // 256
    diff_qk = diff_heads * 2 * HEAD_DIM
    diff_v = diff_heads * 2 * HEAD_DIM
    gqa_q = (d_model // 128) * HEAD_DIM
    gqa_kv = (d_model // 128 // 4) * HEAD_DIM
    conv = 3 * d_model // 8
    return dict(fourier=fourier, diff_heads=diff_heads, diff_qk=diff_qk, diff_v=diff_v,
                gqa_q=gqa_q, gqa_kv=gqa_kv, conv=conv)


def _params(semantics):
    return pltpu.CompilerParams(dimension_semantics=semantics, vmem_limit_bytes=V7X_VMEM_LIMIT_BYTES)


def _mod_kernel(c_ref, w_ref, b_ref, o_ref):
    c = c_ref[...]
    a = c * jax.nn.sigmoid(c)
    o_ref[...] = jnp.dot(a, w_ref[...], preferred_element_type=F32,
                         precision=lax.Precision.HIGHEST) + b_ref[...]


def _modulation(c_rows, w_mod, b_mod):
    depth, d, n = w_mod.shape
    tn = 768
    return pl.pallas_call(
        _mod_kernel,
        grid=(depth, n // tn),
        in_specs=[pl.BlockSpec((MOD_ROWS, d), lambda l, j: (0, 0)),
                  pl.BlockSpec((None, d, tn), lambda l, j: (l, 0, j)),
                  pl.BlockSpec((None, 1, tn), lambda l, j: (l, 0, j))],
        out_specs=pl.BlockSpec((None, MOD_ROWS, tn), lambda l, j: (l, 0, j)),
        out_shape=jax.ShapeDtypeStruct((depth, MOD_ROWS, n), F32),
        compiler_params=_params(("arbitrary", "arbitrary")),
        name="modulation",
    )(c_rows, w_mod, b_mod.reshape(depth, 1, n))


def _modulated(x, g, shift, scale):
    h = x * lax.rsqrt(jnp.mean(x * x, axis=-1, keepdims=True) + EPS) * g
    return h * (1.0 + scale) + shift


def _rms(x, g):
    return x * lax.rsqrt(jnp.mean(x * x, axis=-1, keepdims=True) + EPS) * g


def _swap_halves(t):
    w = t.shape[-1]
    lane = lax.broadcasted_iota(jnp.int32, t.shape, 1)
    from_right = pltpu.roll(t, w - ROPE_FREQS, 1)
    from_left = pltpu.roll(t, ROPE_FREQS, 1)
    return jnp.where((lane % (2 * ROPE_FREQS)) < ROPE_FREQS, from_right, from_left)


def _tile_lanes(t, width):
    reps = width // t.shape[-1]
    return t if reps == 1 else jnp.concatenate([t] * reps, axis=-1)


def _head_norm(t, ones_bd, gain):
    w = t.shape[-1]
    t2 = t * t
    hi = t2.astype(BF16)
    lo = (t2 - hi.astype(F32)).astype(BF16)
    e = ones_bd[:w, :w]
    ss = (jnp.dot(hi, e, preferred_element_type=F32) + jnp.dot(lo, e, preferred_element_type=F32))
    return t * lax.rsqrt(ss * (1.0 / HEAD_DIM) + EPS) * gain


def _inproj_kernel(x_ref, shift_ref, scale_ref, g_ref, w_ref, ones_ref, cos_ref, sin_ref, qn_ref, kn_ref,
                   dft_ref, *rest, dims, kv_only, n_alias):
    out_refs = rest[n_alias:]
    nq, nkv, nv, nf, nc = dims["diff_qk"], dims["gqa_kv"], dims["diff_v"], dims["fourier"], dims["conv"]
    ngq = dims["gqa_q"]
    h = _modulated(x_ref[0], g_ref[...], shift_ref[0], scale_ref[0]).astype(BF16)

    def proj(start, width):
        return jnp.dot(h, w_ref[:, start:start + width].astype(BF16), preferred_element_type=F32)

    def rope(t):
        w = t.shape[-1]
        return t * _tile_lanes(cos_ref[...], w) + _swap_halves(t) * _tile_lanes(sin_ref[...], w)

    ones_bd = ones_ref[...]
    off = 0
    dk = proj(off, nq); off += nq
    dv = proj(off, nv); off += nv
    gk = proj(off, nkv); off += nkv
    gv = proj(off, nkv); off += nkv
    kdt_ref, kgt_ref, vd_ref, vg_ref = out_refs[-4:]
    kdt_ref[0] = rope(dk).T.astype(BF16)
    kgt_ref[0] = rope(_head_norm(gk, ones_bd, kn_ref[...])).T.astype(BF16)
    vd_ref[0] = dv.astype(BF16)
    vg_ref[0] = gv.astype(BF16)
    if not kv_only:
        qd_ref, qg_ref, xf_ref, vc_ref = out_refs[:4]
        uf = proj(off, nf).astype(BF16); off += nf
        xf_ref[0] = jnp.dot(uf, dft_ref[...], preferred_element_type=F32).astype(BF16)
        dq = proj(off, nq); off += nq
        qd_ref[0] = (rope(dq) * Q_PRESCALE).astype(BF16)
        gq = proj(off, ngq); off += ngq
        qg_ref[0] = (rope(_head_norm(gq, ones_bd, qn_ref[...])) * Q_PRESCALE).astype(BF16)
        glu = proj(off, 2 * nc)
        vc_ref[0] = glu[:, :nc] * jax.nn.sigmoid(glu[:, nc:])


def _inproj(x, mod, g_pre, w_in, ones_bd, cos_t, sin_t, qn, kn, dft_ch, kv_bufs, *, dims, layer, n_tok_cols,
            kv_only, row_start, tm):
    b, l, d = x.shape
    tm = min(tm, l)
    nq, nkv, nv, nf, nc, ngq = (dims["diff_qk"], dims["gqa_kv"], dims["diff_v"], dims["fourier"],
                                dims["conv"], dims["gqa_q"])
    blk0 = row_start // tm
    assert row_start % tm == 0
    tok = lambda w: pl.BlockSpec((1, tm, w), lambda bi, i: (bi, i, 0))
    const = lambda shape: pl.BlockSpec(shape, lambda bi, i: (0,) * len(shape))
    tok_outs = [] if kv_only else [((b, l, nq), BF16), ((b, l, ngq), BF16), ((b, l, 2 * nf), BF16), ((b, l, nc), F32)]
    kv_shapes = [(a.shape, a.dtype) for a in kv_bufs]
    kv_specs = [pl.BlockSpec((1, nq, tm), lambda bi, i: (bi, 0, blk0 + i)),
                pl.BlockSpec((1, nkv, tm), lambda bi, i: (bi, 0, blk0 + i)),
                pl.BlockSpec((1, tm, nv), lambda bi, i: (bi, blk0 + i, 0)),
                pl.BlockSpec((1, tm, nkv), lambda bi, i: (bi, blk0 + i, 0))]
    n_alias = len(kv_bufs)
    n_in = 11
    aliases = {n_in + k: len(tok_outs) + k for k in range(n_alias)}
    outs = pl.pallas_call(
        functools.partial(_inproj_kernel, dims=dims, kv_only=kv_only, n_alias=n_alias),
        grid=(b, l // tm),
        in_specs=[tok(d),
                  pl.BlockSpec((1, 1, d), lambda bi, i: (bi, 0, 0)),
                  pl.BlockSpec((1, 1, d), lambda bi, i: (bi, 0, 1)),
                  const((1, d)),
                  pl.BlockSpec((None, d, n_tok_cols), lambda bi, i: (layer, 0, 0)),
                  const(ones_bd.shape),
                  pl.BlockSpec((tm, 2 * HEAD_DIM), lambda bi, i: (i, 0)),
                  pl.BlockSpec((tm, 2 * HEAD_DIM), lambda bi, i: (i, 0)),
                  const(qn.shape), const(kn.shape), const(dft_ch.shape)]
                 + [pl.BlockSpec(memory_space=pl.ANY)] * n_alias,
        out_specs=[tok(s[-1]) for s, _ in tok_outs] + kv_specs,
        out_shape=[jax.ShapeDtypeStruct(s, dt) for s, dt in tok_outs + kv_shapes],
        input_output_aliases=aliases,
        compiler_params=_params(("arbitrary", "arbitrary")),
        name="inproj_kv" if kv_only else "inproj",
    )(x, mod, mod, g_pre, w_in, ones_bd, cos_t, sin_t, qn, kn, dft_ch, *kv_bufs)
    return outs[:len(tok_outs)], outs[len(tok_outs):]


def _conv_ln_silu(win_ref, w_ref, b_ref, g_ref, beta_ref, yc_ref, *, rows, chunk):
    first = CONV_HALO - CONV_KERNEL // 2
    for r0 in range(0, rows, chunk):
        acc = jnp.zeros((chunk, w_ref.shape[-1]), F32) + b_ref[...]
        for phase in range(SUBLANES):
            part = None
            for t in range(CONV_KERNEL):
                if (t + first) % SUBLANES != phase:
                    continue
                base = r0 + t + first - phase
                term = w_ref[t:t + 1, :] * win_ref[base:base + chunk + SUBLANES, :]
                part = term if part is None else part + term
            acc = acc + part[phase:phase + chunk, :]
        mu = jnp.mean(acc, axis=-1, keepdims=True)
        cen = acc - mu
        var = jnp.mean(cen * cen, axis=-1, keepdims=True)
        y = cen * lax.rsqrt(var + EPS) * g_ref[...] + beta_ref[...]
        yc_ref[r0:r0 + chunk, :] = (y * jax.nn.sigmoid(y)).astype(BF16)


def _softmax_pv(q, k_t, v_ext):
    s = jnp.dot(q, k_t, preferred_element_type=F32)
    e = jnp.exp2(s - jnp.max(s, axis=-1, keepdims=True))
    return jnp.dot(e.astype(BF16), v_ext, preferred_element_type=F32)


def _conv_kernel(vprev_ref, vcur_ref, vnext_ref, cw_ref, cb_ref, cg_ref, cbeta_ref, yc_ref, win_ref):
    i = pl.program_id(1)
    rows = vcur_ref.shape[1]
    win_ref[:CONV_HALO, :] = jnp.where(i > 0, vprev_ref[0], 0.0)
    win_ref[CONV_HALO:CONV_HALO + rows, :] = vcur_ref[0]
    win_ref[CONV_HALO + rows:, :] = jnp.where(i < pl.num_programs(1) - 1, vnext_ref[0], 0.0)
    _conv_ln_silu(win_ref, cw_ref, cb_ref, cg_ref, cbeta_ref, yc_ref.at[0], rows=rows, chunk=32)


def _conv(vc, conv_w, conv_b, conv_g, conv_beta, *, rows):
    b, l, c = vc.shape
    rows = min(rows, l)
    hb = rows // CONV_HALO
    n_hb = l // CONV_HALO
    const = lambda a: pl.BlockSpec(a.shape, lambda bi, i: (0,) * a.ndim)
    prev = pl.BlockSpec((1, CONV_HALO, c), lambda bi, i: (bi, jnp.maximum(i * hb - 1, 0), 0))
    nxt = pl.BlockSpec((1, CONV_HALO, c), lambda bi, i: (bi, jnp.minimum((i + 1) * hb, n_hb - 1), 0))
    tile = pl.BlockSpec((1, rows, c), lambda bi, i: (bi, i, 0))
    return pl.pallas_call(
        _conv_kernel,
        grid=(b, l // rows),
        in_specs=[prev, tile, nxt, const(conv_w), const(conv_b), const(conv_g), const(conv_beta)],
        out_specs=tile,
        out_shape=jax.ShapeDtypeStruct((b, l, c), BF16),
        scratch_shapes=[pltpu.VMEM((rows + 2 * CONV_HALO, c), F32)],
        compiler_params=_params(("arbitrary", "arbitrary")),
        name="conv",
    )(vc, vc, vc, conv_w, conv_b, conv_g, conv_beta)


def _attn_kernel(qd_ref, qg_ref, kdt_ref, vd_ref, kgt_ref, vg_ref, lam_ref, subln_ref, od_ref, og_ref,
                 vdx_ref, vgx_ref, *, n_diff, n_gq, n_kv, lam_init):
    dh = HEAD_DIM
    dv = 2 * HEAD_DIM

    @pl.when(pl.program_id(1) == 0)
    def _():
        for hd in range(n_diff):
            vdx_ref[hd, :, :dv] = vd_ref[0, :, hd * dv:(hd + 1) * dv]
            vdx_ref[hd, :, dv:] = jnp.ones((vd_ref.shape[1], V7X_MXU_COLUMNS - dv), BF16)
        vgx_ref[:, :n_kv * dh] = vg_ref[0]
        vgx_ref[:, n_kv * dh:] = jnp.ones((vg_ref.shape[1], V7X_MXU_COLUMNS - n_kv * dh), BF16)

    lv = lam_ref[...]
    lam = (jnp.exp(jnp.sum(lv[0:1] * lv[1:2], axis=-1, keepdims=True))
           - jnp.exp(jnp.sum(lv[2:3] * lv[3:4], axis=-1, keepdims=True)) + lam_init)
    for hd in range(n_diff):
        maps = []
        for j in range(2):
            c = (2 * hd + j) * dh
            pv = _softmax_pv(qd_ref[0, :, c:c + dh], kdt_ref[0, c:c + dh, :], vdx_ref[hd])
            maps.append(pv[:, :dv] / pv[:, dv:])
        o = maps[0] - lam * maps[1]
        od_ref[0, :, hd * dv:(hd + 1) * dv] = (_rms(o, subln_ref[...]) * (1.0 - lam_init)).astype(BF16)
    group = n_gq // n_kv
    lane = lax.broadcasted_iota(jnp.int32, (qg_ref.shape[1], dv), 1)
    for pair in range(n_gq // 2):
        kv = (2 * pair) // group
        halves = []
        for hq in (2 * pair, 2 * pair + 1):
            pv = _softmax_pv(qg_ref[0, :, hq * dh:(hq + 1) * dh], kgt_ref[0, kv * dh:(kv + 1) * dh, :], vgx_ref[...])
            halves.append(pv[:, :dv] / pv[:, dv:])
        left = halves[0] if kv % 2 == 0 else pltpu.roll(halves[0], dh, 1)
        right = pltpu.roll(halves[1], dh, 1) if kv % 2 == 0 else halves[1]
        og_ref[0, :, pair * dv:(pair + 1) * dv] = jnp.where(lane < dh, left, right).astype(BF16)


def _attention(qd, qg, kv_bufs, lam_p, subln, *, dims, lam_init, tq, key_start, key_len):
    kd_t, kg_t, vd, vg = kv_bufs
    b, l, nq = qd.shape
    n_diff, n_gq, n_kv = dims["diff_heads"], dims["gqa_q"] // HEAD_DIM, dims["gqa_kv"] // HEAD_DIM
    assert n_kv * HEAD_DIM == 2 * HEAD_DIM and (n_gq // n_kv) % 2 == 0 and key_start % key_len == 0
    kb = key_start // key_len
    tq = min(tq, l)
    tok = lambda w: pl.BlockSpec((1, tq, w), lambda bi, i: (bi, i, 0))
    keys_t = lambda a: pl.BlockSpec((1, a.shape[1], key_len), lambda bi, i: (bi, 0, kb))
    vals = lambda a: pl.BlockSpec((1, key_len, a.shape[2]), lambda bi, i: (bi, kb, 0))
    const = lambda a: pl.BlockSpec(a.shape, lambda bi, i: (0,) * a.ndim)
    return pl.pallas_call(
        functools.partial(_attn_kernel, n_diff=n_diff, n_gq=n_gq, n_kv=n_kv, lam_init=lam_init),
        grid=(b, l // tq),
        in_specs=[tok(nq), tok(qg.shape[-1]), keys_t(kd_t), vals(vd), keys_t(kg_t), vals(vg),
                  const(lam_p), const(subln)],
        out_specs=[tok(dims["diff_v"]), tok(dims["gqa_q"])],
        out_shape=[jax.ShapeDtypeStruct((b, l, dims["diff_v"]), BF16),
                   jax.ShapeDtypeStruct((b, l, dims["gqa_q"]), BF16)],
        scratch_shapes=[pltpu.VMEM((n_diff, key_len, V7X_MXU_COLUMNS), BF16),
                        pltpu.VMEM((key_len, V7X_MXU_COLUMNS), BF16)],
        compiler_params=_params(("arbitrary", "arbitrary")),
        name="attention",
    )(qd, qg, kd_t, vd, kg_t, vg, lam_p, subln)


def _fourier_kernel(c_ref, s_ref, xf_ref, o_ref, *, nf, out_scale):
    y = (jnp.dot(c_ref[...], xf_ref[0, :, :nf], preferred_element_type=F32)
         - jnp.dot(s_ref[...], xf_ref[0, :, nf:], preferred_element_type=F32))
    o_ref[0] = (y * out_scale).astype(BF16)


def _fourier_dense(tables, xf, *, tr=512):
    dft_c, dft_s = tables
    b, l, nf2 = xf.shape
    nf = nf2 // 2
    tr = min(tr, l)
    return pl.pallas_call(
        functools.partial(_fourier_kernel, nf=nf, out_scale=float(l) ** -0.5),
        grid=(b, l // tr),
        in_specs=[pl.BlockSpec((tr, l), lambda bi, i: (i, 0)),
                  pl.BlockSpec((tr, l), lambda bi, i: (i, 0)),
                  pl.BlockSpec((1, l, nf2), lambda bi, i: (bi, 0, 0))],
        out_specs=pl.BlockSpec((1, tr, nf), lambda bi, i: (bi, i, 0)),
        out_shape=jax.ShapeDtypeStruct((b, l, nf), BF16),
        compiler_params=_params(("arbitrary", "arbitrary")),
        name="fourier",
    )(dft_c, dft_s, xf)


def _fft_a_kernel(z_ref, m_ref, cos_ref, sin_ref, o_ref, *, nf):
    n1, r = z_ref.shape[1], z_ref.shape[2]
    rows = n1 * r
    z = z_ref[0].reshape(rows, 2 * nf)
    t = jnp.dot(m_ref[...], z, preferred_element_type=F32)
    cz, sz = t[:rows], t[rows:]
    ar = cz[:, :nf] - sz[:, nf:]
    ai = -(sz[:, :nf] + cz[:, nf:])
    ct = _tile_lanes(cos_ref[0], nf)
    st = _tile_lanes(sin_ref[0], nf)
    o_ref[0, :, :, :nf] = (ar * ct + ai * st).reshape(n1, r, nf).astype(BF16)
    o_ref[0, :, :, nf:] = (ai * ct - ar * st).reshape(n1, r, nf).astype(BF16)


def _fft_c_kernel(b_ref, mc_ref, ms_ref, o_ref, *, nf, out_scale):
    r, n2 = b_ref.shape[1], b_ref.shape[2]
    blk = b_ref[0].reshape(r * n2, 2 * nf)
    y = (jnp.dot(mc_ref[...], blk[:, :nf], preferred_element_type=F32)
         + jnp.dot(ms_ref[...], blk[:, nf:], preferred_element_type=F32))
    o_ref[0] = (y * out_scale).reshape(n2, r, nf).astype(BF16)


def _fourier_two_stage(tables, xf):
    m_a, tw_cos, tw_sin, m_c_cos, m_c_sin = tables
    b, l, nf2 = xf.shape
    nf = nf2 // 2
    r = BF16_SUBLANES
    n2 = m_c_cos.shape[0] // r
    n1 = l // n2
    tile = lambda w: pl.BlockSpec((1, n1, r, w), lambda bi, s: (bi, 0, s, 0))
    const = lambda a: pl.BlockSpec(a.shape, lambda bi, s: (0,) * a.ndim)
    twiddle = pl.BlockSpec((1, n1 * r, 2 * HEAD_DIM), lambda bi, s: (s, 0, 0))
    staged = pl.pallas_call(
        functools.partial(_fft_a_kernel, nf=nf),
        grid=(b, n2 // r),
        in_specs=[tile(nf2), const(m_a), twiddle, twiddle],
        out_specs=tile(nf2),
        out_shape=jax.ShapeDtypeStruct((b, n1, n2, nf2), BF16),
        compiler_params=_params(("arbitrary", "arbitrary")),
        name="fourier_a",
    )(xf.reshape(b, n1, n2, nf2), m_a, tw_cos, tw_sin)
    out = pl.pallas_call(
        functools.partial(_fft_c_kernel, nf=nf, out_scale=float(l) ** -0.5),
        grid=(b, n1 // r),
        in_specs=[pl.BlockSpec((1, r, n2, nf2), lambda bi, s: (bi, s, 0, 0)), const(m_c_cos), const(m_c_sin)],
        out_specs=pl.BlockSpec((1, n2, r, nf), lambda bi, s: (bi, 0, s, 0)),
        out_shape=jax.ShapeDtypeStruct((b, n2, n1, nf), BF16),
        compiler_params=_params(("arbitrary", "arbitrary")),
        name="fourier_c",
    )(staged, m_c_cos, m_c_sin)
    return out.reshape(b, l, nf)


def _merge_kernel(x_ref, shift_ref, scale_ref, gate_ref, g_ref, wgate_ref, yf_ref, od_ref, og_ref, yc_ref,
                  wf_ref, wd_ref, wg_ref, wc_ref, wout_ref, gpost_ref, o_ref):
    x = x_ref[0]
    d = x.shape[-1]
    h = _modulated(x, g_ref[...], shift_ref[0], scale_ref[0]).astype(BF16)
    branches = ((yf_ref, wf_ref), (od_ref, wd_ref), (og_ref, wg_ref), (yc_ref, wc_ref))
    merged = None
    for k, (y_ref, w_ref) in enumerate(branches):
        gate = jax.nn.sigmoid(jnp.dot(h, wgate_ref[:, k * d:(k + 1) * d], preferred_element_type=F32))
        term = gate * jnp.dot(y_ref[0], w_ref[...], preferred_element_type=F32)
        merged = term if merged is None else merged + term
    mix = jnp.dot(merged.astype(BF16), wout_ref[...], preferred_element_type=F32)
    o_ref[0] = x + gate_ref[0] * _rms(mix, gpost_ref[...])


def _merge(x, mod, g_pre, w_gate, yf, od, og, yc, wf, wd, wg, wc, wout, g_post, *, tm):
    b, l, d = x.shape
    tm = min(tm, l)
    tok = lambda w: pl.BlockSpec((1, tm, w), lambda bi, i: (bi, i, 0))
    modc = lambda k: pl.BlockSpec((1, 1, d), lambda bi, i: (bi, 0, k))
    const = lambda a: pl.BlockSpec(a.shape, lambda bi, i: (0,) * a.ndim)
    return pl.pallas_call(
        _merge_kernel,
        grid=(b, l // tm),
        in_specs=[tok(d), modc(0), modc(1), modc(2), const(g_pre), const(w_gate),
                  tok(yf.shape[-1]), tok(od.shape[-1]), tok(og.shape[-1]), tok(yc.shape[-1]),
                  const(wf), const(wd), const(wg), const(wc), const(wout), const(g_post)],
        out_specs=tok(d),
        out_shape=jax.ShapeDtypeStruct((b, l, d), F32),
        compiler_params=_params(("arbitrary", "arbitrary")),
        name="merge",
    )(x, mod, mod, mod, g_pre, w_gate, yf, od, og, yc, wf, wd, wg, wc, wout, g_post)


def _mlp_kernel(x_ref, shift_ref, scale_ref, gate_ref, g_ref, w1_ref, w2_ref, gpost_ref, o_ref, *, ff_chunk):
    x = x_ref[0]
    h = _modulated(x, g_ref[...], shift_ref[0], scale_ref[0]).astype(BF16)
    acc = None
    for c0 in range(0, w1_ref.shape[-1], ff_chunk):
        u = jnp.maximum(jnp.dot(h, w1_ref[:, c0:c0 + ff_chunk], preferred_element_type=F32), 0.0)
        part = jnp.dot((u * u).astype(BF16), w2_ref[c0:c0 + ff_chunk, :], preferred_element_type=F32)
        acc = part if acc is None else acc + part
    o_ref[0] = x + gate_ref[0] * _rms(acc, gpost_ref[...])


def _mlp(x, mod, g_pre, w1, w2, g_post, *, tm):
    b, l, d = x.shape
    tm = min(tm, l)
    tok = lambda w: pl.BlockSpec((1, tm, w), lambda bi, i: (bi, i, 0))
    modc = lambda k: pl.BlockSpec((1, 1, d), lambda bi, i: (bi, 0, k))
    const = lambda a: pl.BlockSpec(a.shape, lambda bi, i: (0,) * a.ndim)
    return pl.pallas_call(
        functools.partial(_mlp_kernel, ff_chunk=1024),
        grid=(b, l // tm),
        in_specs=[tok(d), modc(3), modc(4), modc(5), const(g_pre), const(w1), const(w2), const(g_post)],
        out_specs=tok(d),
        out_shape=jax.ShapeDtypeStruct((b, l, d), F32),
        compiler_params=_params(("arbitrary", "arbitrary")),
        name="mlp",
    )(x, mod, mod, mod, g_pre, w1, w2, g_post)


def _angles(i, j, period):
    return (np.outer(i, j) % period).astype(np.float64) * (2.0 * np.pi / period)


def _mxu_table(values):
    return jnp.asarray(values, F32).astype(BF16)


def _rope_tables(seq_len):
    rows = seq_len // GRID_W
    row = np.repeat(np.arange(rows, dtype=np.float64), GRID_W)
    col = np.tile(np.arange(GRID_W, dtype=np.float64), rows)
    inv_freq = ROPE_THETA ** (-np.arange(ROPE_FREQS, dtype=np.float64) / ROPE_FREQS)
    ang_r, ang_c = row[:, None] * inv_freq, col[:, None] * inv_freq
    cos64 = np.concatenate([np.cos(ang_r)] * 2 + [np.cos(ang_c)] * 2, axis=-1)
    sin64 = np.concatenate([-np.sin(ang_r), np.sin(ang_r), -np.sin(ang_c), np.sin(ang_c)], axis=-1)
    return (jnp.asarray(np.concatenate([cos64] * 2, axis=-1), F32),
            jnp.asarray(np.concatenate([sin64] * 2, axis=-1), F32))


def _seq_dft(l):
    th = _angles(np.arange(l), np.arange(l), l)
    return _mxu_table(np.cos(th)), _mxu_table(np.sin(th))


def _two_stage_dft(l, n2):
    n1 = l // n2
    r = BF16_SUBLANES
    i1, i2 = np.arange(n1), np.arange(n2)
    eye = np.eye(r)
    th_a = _angles(i1, i1, n1)
    m_a = np.concatenate([np.kron(np.cos(th_a), eye), np.kron(np.sin(th_a), eye)], axis=0)
    th_t = _angles(i1, i2, l).reshape(n1, n2 // r, r)
    th_t = np.transpose(th_t, (1, 0, 2)).reshape(n2 // r, n1 * r)
    lanes = lambda t: jnp.asarray(np.broadcast_to(t[:, :, None], t.shape + (2 * HEAD_DIM,)), F32)
    th_c = _angles(i2, i2, n2)
    perm = lambda w: _mxu_table(np.einsum('kb,ij->kijb', w, eye).reshape(n2 * r, r * n2))
    return _mxu_table(m_a), lanes(np.cos(th_t)), lanes(np.sin(th_t)), perm(np.cos(th_c)), perm(np.sin(th_c))


def _channel_dft(n_groups):
    g = FOURIER_GROUP_DIM
    th = _angles(np.arange(g), np.arange(g), g)
    eye = np.eye(n_groups)
    scale = float(g) ** -0.5
    return _mxu_table(np.concatenate([np.kron(eye, np.cos(th) * scale), np.kron(eye, np.sin(th) * scale)], axis=-1))


def _seq_fourier(l):
    n2 = FOURIER_STAGE_LEN
    if l % n2 == 0 and (l // n2) % BF16_SUBLANES == 0:
        return functools.partial(_fourier_two_stage, _two_stage_dft(l, n2))
    return functools.partial(_fourier_dense, _seq_dft(l))


def _mix_and_mlp(x, mod, lw, tok, kv_bufs, fourier, *, dims, lam_init, key_start, key_len, tm_mlp):
    qd, qg, xf, vc = tok
    od, og = _attention(qd, qg, kv_bufs, lw["lam_p"], lw["subln"], dims=dims, lam_init=lam_init, tq=256,
                        key_start=key_start, key_len=key_len)
    yf = fourier(xf)
    yc = _conv(vc, lw["conv_w"], lw["conv_b"], lw["conv_ln_g"], lw["conv_ln_b"], rows=512)
    x = _merge(x, mod, lw["g_pre_mix"], lw["w_gate"], yf, od, og, yc, lw["w_br_f"], lw["w_br_d"], lw["w_br_g"],
               lw["w_br_c"], lw["w_out"], lw["g_post_mix"], tm=512)
    return _mlp(x, mod, lw["g_pre_mlp"], lw["w_ff1"], lw["w_ff2"], lw["g_post_mlp"], tm=tm_mlp)


def kernel(x, c, ctx, c_ctx, w_mod, b_mod, g_pre_mix, g_post_mix, g_pre_mlp, g_post_mlp, w_in, q_norm, k_norm,
           diff_lambda, diff_subln, conv_dw, conv_dw_bias, conv_ln_g, conv_ln_b, w_br_fourier, w_br_diff,
           w_br_gqa, w_br_conv, w_out, w_ff1, w_ff2):
    b, l, d = x.shape
    lc = ctx.shape[1]
    lk = l + lc
    depth = w_in.shape[0]
    dims = _dims(d)
    assert b + 1 <= MOD_ROWS and l % GRID_W == 0 and l % lc == 0

    rope_x = _rope_tables(l)
    rope_none = (jnp.ones((lc, 2 * HEAD_DIM), F32), jnp.zeros((lc, 2 * HEAD_DIM), F32))
    fourier_x, fourier_c = _seq_fourier(l), _seq_fourier(lc)
    n_groups = dims["fourier"] // FOURIER_GROUP_DIM
    heads = np.arange(dims["diff_qk"]) // HEAD_DIM
    ones_bd = jnp.asarray((heads[:, None] == heads[None, :]).astype(BF16))
    dft_ch = _channel_dft(n_groups)

    c_rows = jnp.zeros((MOD_ROWS, d), F32).at[:b].set(c).at[b].set(c_ctx)
    mod_all = _modulation(c_rows, w_mod, b_mod)

    n_tok_cols = w_in.shape[-1] - N_BRANCHES * d

    xc = ctx
    for li in range(depth):
        lam_init = 0.8 - 0.6 * math.exp(-0.3 * li)
        row = lambda a: a[li].reshape(1, -1)
        lw = dict(
            w_gate=w_in[li, :, n_tok_cols:].astype(BF16),
            g_pre_mix=row(g_pre_mix), g_post_mix=row(g_post_mix), g_pre_mlp=row(g_pre_mlp),
            g_post_mlp=row(g_post_mlp),
            qn=jnp.tile(q_norm[li], dims["gqa_q"] // HEAD_DIM).reshape(1, -1),
            kn=jnp.tile(k_norm[li], dims["gqa_kv"] // HEAD_DIM).reshape(1, -1),
            lam_p=diff_lambda[li], subln=row(diff_subln),
            conv_w=conv_dw[li], conv_b=row(conv_dw_bias), conv_ln_g=row(conv_ln_g), conv_ln_b=row(conv_ln_b),
            w_br_f=w_br_fourier[li].astype(BF16), w_br_d=w_br_diff[li].astype(BF16),
            w_br_g=w_br_gqa[li].astype(BF16), w_br_c=w_br_conv[li].astype(BF16), w_out=w_out[li].astype(BF16),
            w_ff1=w_ff1[li].astype(BF16), w_ff2=w_ff2[li].astype(BF16))
        mod_x = mod_all[li, :b].reshape(b, 1, N_MOD * d)
        mod_c = jnp.broadcast_to(mod_all[li, b].reshape(1, 1, N_MOD * d), (b, 1, N_MOD * d))
        last = li == depth - 1

        inproj = functools.partial(_inproj, g_pre=lw["g_pre_mix"], w_in=w_in, ones_bd=ones_bd, qn=lw["qn"],
                                   kn=lw["kn"], dft_ch=dft_ch, dims=dims, layer=li, n_tok_cols=n_tok_cols)
        kv_bufs = (jnp.zeros((b, dims["diff_qk"], lk), BF16), jnp.zeros((b, dims["gqa_kv"], lk), BF16),
                   jnp.zeros((b, lk, dims["diff_v"]), BF16), jnp.zeros((b, lk, dims["gqa_kv"]), BF16))
        tok_c, kv_bufs = inproj(xc, mod_c, cos_t=rope_none[0], sin_t=rope_none[1], kv_bufs=kv_bufs, kv_only=last,
                                row_start=l, tm=lc)
        tok_x, kv_bufs = inproj(x, mod_x, cos_t=rope_x[0], sin_t=rope_x[1], kv_bufs=kv_bufs, kv_only=False,
                                row_start=0, tm=512)
        mix = functools.partial(_mix_and_mlp, lw=lw, kv_bufs=kv_bufs, dims=dims, lam_init=lam_init)
        x = mix(x, mod_x, tok=tok_x, fourier=fourier_x, key_start=0, key_len=lk, tm_mlp=512)
        if not last:
            xc = mix(xc, mod_c, tok=tok_c, fourier=fourier_c, key_start=l, key_len=lc, tm_mlp=lc)
    return x
```

```python
import functools
import math

import jax
import jax.numpy as jnp
import numpy as np
from jax import lax
from jax.experimental import pallas as pl
from jax.experimental.pallas import tpu as pltpu

F32 = jnp.float32
BF16 = jnp.bfloat16

GRID_W = 64
HEAD_DIM = 64
ROPE_FREQS = HEAD_DIM // 4
ROPE_THETA = 10000.0
EPS = 1e-6
ATTN_SCALE = HEAD_DIM ** -0.5
Q_PRESCALE = ATTN_SCALE * math.log2(math.e)
FOURIER_GROUP_DIM = 64
CONV_KERNEL = 31
CONV_HALO = 16
N_BRANCHES = 4
N_MOD = 6
MOD_ROWS = 8

V7X_VMEM_LIMIT_BYTES = 56 * 1024 * 1024
V7X_MXU_COLUMNS = 256
SUBLANES = 8
BF16_SUBLANES = 16
FOURIER_STAGE_LEN = 64


def _dims(d_model):
    fourier = 3 * d_model // 8
    diff_heads = d_model // 256
    diff_qk = diff_heads * 2 * HEAD_DIM
    diff_v = diff_heads * 2 * HEAD_DIM
    gqa_q = (d_model // 128) * HEAD_DIM
    gqa_kv = (d_model // 128 // 4) * HEAD_DIM
    conv = 3 * d_model // 8
    return dict(fourier=fourier, diff_heads=diff_heads, diff_qk=diff_qk, diff_v=diff_v,
                gqa_q=gqa_q, gqa_kv=gqa_kv, conv=conv)


def _params(semantics):
    return pltpu.CompilerParams(dimension_semantics=semantics, vmem_limit_bytes=V7X_VMEM_LIMIT_BYTES)


def _mod_kernel(c_ref, w_ref, b_ref, o_ref):
    c = c_ref[...]
    a = c * jax.nn.sigmoid(c)
    o_ref[...] = jnp.dot(a, w_ref[...], preferred_element_type=F32,
                         precision=lax.Precision.HIGHEST) + b_ref[...]


def _modulation(c_rows, w_mod, b_mod):
    depth, d, n = w_mod.shape
    tn = 2 * d
    return pl.pallas_call(
        _mod_kernel,
        grid=(depth, n // tn),
        in_specs=[pl.BlockSpec((MOD_ROWS, d), lambda l, j: (0, 0)),
                  pl.BlockSpec((None, d, tn), lambda l, j: (l, 0, j)),
                  pl.BlockSpec((None, 1, tn), lambda l, j: (l, 0, j))],
        out_specs=pl.BlockSpec((None, MOD_ROWS, tn), lambda l, j: (l, 0, j)),
        out_shape=jax.ShapeDtypeStruct((depth, MOD_ROWS, n), F32),
        compiler_params=_params(("arbitrary", "arbitrary")),
        name="modulation",
    )(c_rows, w_mod, b_mod.reshape(depth, 1, n))


def _modulated(x, g, shift, scale):
    h = x * lax.rsqrt(jnp.mean(x * x, axis=-1, keepdims=True) + EPS) * g
    return h * (1.0 + scale) + shift


def _rms(x, g):
    return x * lax.rsqrt(jnp.mean(x * x, axis=-1, keepdims=True) + EPS) * g


def _swap_halves(t):
    w = t.shape[-1]
    lane = lax.broadcasted_iota(jnp.int32, t.shape, 1)
    from_right = pltpu.roll(t, w - ROPE_FREQS, 1)
    from_left = pltpu.roll(t, ROPE_FREQS, 1)
    return jnp.where((lane % (2 * ROPE_FREQS)) < ROPE_FREQS, from_right, from_left)


def _tile_lanes(t, width):
    reps = width // t.shape[-1]
    return t if reps == 1 else jnp.concatenate([t] * reps, axis=-1)


def _head_norm(t, ones_bd, gain):
    w = t.shape[-1]
    t2 = t * t
    hi = t2.astype(BF16)
    lo = (t2 - hi.astype(F32)).astype(BF16)
    e = ones_bd[:w, :w]
    ss = (jnp.dot(hi, e, preferred_element_type=F32) + jnp.dot(lo, e, preferred_element_type=F32))
    return t * lax.rsqrt(ss * (1.0 / HEAD_DIM) + EPS) * gain


def _inproj_kernel(x_ref, shift_ref, scale_ref, g_ref, w_ref, ones_ref, cos_ref, sin_ref, qn_ref, kn_ref,
                   dft_ref, *rest, dims, kv_only, n_alias):
    out_refs = rest[n_alias:]
    nq, nkv, nv, nf, nc = dims["diff_qk"], dims["gqa_kv"], dims["diff_v"], dims["fourier"], dims["conv"]
    ngq = dims["gqa_q"]
    h = _modulated(x_ref[0], g_ref[...], shift_ref[0], scale_ref[0]).astype(BF16)

    def proj(start, width):
        return jnp.dot(h, w_ref[:, start:start + width].astype(BF16), preferred_element_type=F32)

    def rope(t):
        w = t.shape[-1]
        return t * _tile_lanes(cos_ref[...], w) + _swap_halves(t) * _tile_lanes(sin_ref[...], w)

    ones_bd = ones_ref[...]
    off = 0
    dk = proj(off, nq); off += nq
    dv = proj(off, nv); off += nv
    gk = proj(off, nkv); off += nkv
    gv = proj(off, nkv); off += nkv
    kdt_ref, kgt_ref, vd_ref, vg_ref = out_refs[-4:]
    kdt_ref[0] = rope(dk).T.astype(BF16)
    kgt_ref[0] = rope(_head_norm(gk, ones_bd, kn_ref[...])).T.astype(BF16)
    vd_ref[0] = dv.astype(BF16)
    vg_ref[0] = gv.astype(BF16)
    if not kv_only:
        qd_ref, qg_ref, xf_ref, vc_ref = out_refs[:4]
        uf = proj(off, nf).astype(BF16); off += nf
        xf_ref[0] = jnp.dot(uf, dft_ref[...], preferred_element_type=F32).astype(BF16)
        dq = proj(off, nq); off += nq
        qd_ref[0] = (rope(dq) * Q_PRESCALE).astype(BF16)
        gq = proj(off, ngq); off += ngq
        qg_ref[0] = (rope(_head_norm(gq, ones_bd, qn_ref[...])) * Q_PRESCALE).astype(BF16)
        glu = proj(off, 2 * nc)
        vc_ref[0] = glu[:, :nc] * jax.nn.sigmoid(glu[:, nc:])


def _inproj(x, mod, g_pre, w_in, ones_bd, cos_t, sin_t, qn, kn, dft_ch, kv_bufs, *, dims, layer, n_tok_cols,
            kv_only, row_start, tm):
    b, l, d = x.shape
    tm = min(tm, l)
    nq, nkv, nv, nf, nc, ngq = (dims["diff_qk"], dims["gqa_kv"], dims["diff_v"], dims["fourier"],
                                dims["conv"], dims["gqa_q"])
    blk0 = row_start // tm
    assert row_start % tm == 0
    tok = lambda w: pl.BlockSpec((1, tm, w), lambda bi, i: (bi, i, 0))
    const = lambda shape: pl.BlockSpec(shape, lambda bi, i: (0,) * len(shape))
    tok_outs = [] if kv_only else [((b, l, nq), BF16), ((b, l, ngq), BF16), ((b, l, 2 * nf), BF16), ((b, l, nc), F32)]
    kv_shapes = [(a.shape, a.dtype) for a in kv_bufs]
    kv_specs = [pl.BlockSpec((1, nq, tm), lambda bi, i: (bi, 0, blk0 + i)),
                pl.BlockSpec((1, nkv, tm), lambda bi, i: (bi, 0, blk0 + i)),
                pl.BlockSpec((1, tm, nv), lambda bi, i: (bi, blk0 + i, 0)),
                pl.BlockSpec((1, tm, nkv), lambda bi, i: (bi, blk0 + i, 0))]
    n_alias = len(kv_bufs)
    n_in = 11
    aliases = {n_in + k: len(tok_outs) + k for k in range(n_alias)}
    outs = pl.pallas_call(
        functools.partial(_inproj_kernel, dims=dims, kv_only=kv_only, n_alias=n_alias),
        grid=(b, l // tm),
        in_specs=[tok(d),
                  pl.BlockSpec((1, 1, d), lambda bi, i: (bi, 0, 0)),
                  pl.BlockSpec((1, 1, d), lambda bi, i: (bi, 0, 1)),
                  const((1, d)),
                  pl.BlockSpec((None, d, n_tok_cols), lambda bi, i: (layer, 0, 0)),
                  const(ones_bd.shape),
                  pl.BlockSpec((tm, 2 * HEAD_DIM), lambda bi, i: (i, 0)),
                  pl.BlockSpec((tm, 2 * HEAD_DIM), lambda bi, i: (i, 0)),
                  const(qn.shape), const(kn.shape), const(dft_ch.shape)]
                 + [pl.BlockSpec(memory_space=pl.ANY)] * n_alias,
        out_specs=[tok(s[-1]) for s, _ in tok_outs] + kv_specs,
        out_shape=[jax.ShapeDtypeStruct(s, dt) for s, dt in tok_outs + kv_shapes],
        input_output_aliases=aliases,
        compiler_params=_params(("arbitrary", "arbitrary")),
        name="inproj_kv" if kv_only else "inproj",
    )(x, mod, mod, g_pre, w_in, ones_bd, cos_t, sin_t, qn, kn, dft_ch, *kv_bufs)
    return outs[:len(tok_outs)], outs[len(tok_outs):]


def _conv_ln_silu(win_ref, w_ref, b_ref, g_ref, beta_ref, yc_ref, *, rows, chunk):
    first = CONV_HALO - CONV_KERNEL // 2
    for r0 in range(0, rows, chunk):
        acc = jnp.zeros((chunk, w_ref.shape[-1]), F32) + b_ref[...]
        for phase in range(SUBLANES):
            part = None
            for t in range(CONV_KERNEL):
                if (t + first) % SUBLANES != phase:
                    continue
                base = r0 + t + first - phase
                term = w_ref[t:t + 1, :] * win_ref[base:base + chunk + SUBLANES, :]
                part = term if part is None else part + term
            acc = acc + part[phase:phase + chunk, :]
        mu = jnp.mean(acc, axis=-1, keepdims=True)
        cen = acc - mu
        var = jnp.mean(cen * cen, axis=-1, keepdims=True)
        y = cen * lax.rsqrt(var + EPS) * g_ref[...] + beta_ref[...]
        yc_ref[r0:r0 + chunk, :] = (y * jax.nn.sigmoid(y)).astype(BF16)


def _softmax_pv(q, k_t, v_ext):
    s = jnp.dot(q, k_t, preferred_element_type=F32)
    e = jnp.exp2(s - jnp.max(s, axis=-1, keepdims=True))
    return jnp.dot(e.astype(BF16), v_ext, preferred_element_type=F32)


def _conv_kernel(vprev_ref, vcur_ref, vnext_ref, cw_ref, cb_ref, cg_ref, cbeta_ref, yc_ref, win_ref):
    i = pl.program_id(1)
    rows = vcur_ref.shape[1]
    win_ref[:CONV_HALO, :] = jnp.where(i > 0, vprev_ref[0], 0.0)
    win_ref[CONV_HALO:CONV_HALO + rows, :] = vcur_ref[0]
    win_ref[CONV_HALO + rows:, :] = jnp.where(i < pl.num_programs(1) - 1, vnext_ref[0], 0.0)
    _conv_ln_silu(win_ref, cw_ref, cb_ref, cg_ref, cbeta_ref, yc_ref.at[0], rows=rows, chunk=32)


def _conv(vc, conv_w, conv_b, conv_g, conv_beta, *, rows):
    b, l, c = vc.shape
    rows = min(rows, l)
    hb = rows // CONV_HALO
    n_hb = l // CONV_HALO
    const = lambda a: pl.BlockSpec(a.shape, lambda bi, i: (0,) * a.ndim)
    prev = pl.BlockSpec((1, CONV_HALO, c), lambda bi, i: (bi, jnp.maximum(i * hb - 1, 0), 0))
    nxt = pl.BlockSpec((1, CONV_HALO, c), lambda bi, i: (bi, jnp.minimum((i + 1) * hb, n_hb - 1), 0))
    tile = pl.BlockSpec((1, rows, c), lambda bi, i: (bi, i, 0))
    return pl.pallas_call(
        _conv_kernel,
        grid=(b, l // rows),
        in_specs=[prev, tile, nxt, const(conv_w), const(conv_b), const(conv_g), const(conv_beta)],
        out_specs=tile,
        out_shape=jax.ShapeDtypeStruct((b, l, c), BF16),
        scratch_shapes=[pltpu.VMEM((rows + 2 * CONV_HALO, c), F32)],
        compiler_params=_params(("arbitrary", "arbitrary")),
        name="conv",
    )(vc, vc, vc, conv_w, conv_b, conv_g, conv_beta)


def _attn_kernel(qd_ref, qg_ref, kdt_ref, vd_ref, kgt_ref, vg_ref, lam_ref, subln_ref, od_ref, og_ref,
                 vdx_ref, vgx_ref, *, n_diff, n_gq, n_kv, lam_init):
    dh = HEAD_DIM
    dv = 2 * HEAD_DIM

    @pl.when(pl.program_id(1) == 0)
    def _():
        for hd in range(n_diff):
            vdx_ref[hd, :, :dv] = vd_ref[0, :, hd * dv:(hd + 1) * dv]
            vdx_ref[hd, :, dv:] = jnp.ones((vd_ref.shape[1], V7X_MXU_COLUMNS - dv), BF16)
        vgx_ref[:, :n_kv * dh] = vg_ref[0]
        vgx_ref[:, n_kv * dh:] = jnp.ones((vg_ref.shape[1], V7X_MXU_COLUMNS - n_kv * dh), BF16)

    lv = lam_ref[...]
    lam = (jnp.exp(jnp.sum(lv[0:1] * lv[1:2], axis=-1, keepdims=True))
           - jnp.exp(jnp.sum(lv[2:3] * lv[3:4], axis=-1, keepdims=True)) + lam_init)
    for hd in range(n_diff):
        maps = []
        for j in range(2):
            c = (2 * hd + j) * dh
            pv = _softmax_pv(qd_ref[0, :, c:c + dh], kdt_ref[0, c:c + dh, :], vdx_ref[hd])
            maps.append(pv[:, :dv] / pv[:, dv:])
        o = maps[0] - lam * maps[1]
        od_ref[0, :, hd * dv:(hd + 1) * dv] = (_rms(o, subln_ref[...]) * (1.0 - lam_init)).astype(BF16)
    group = n_gq // n_kv
    lane = lax.broadcasted_iota(jnp.int32, (qg_ref.shape[1], dv), 1)
    for pair in range(n_gq // 2):
        kv = (2 * pair) // group
        halves = []
        for hq in (2 * pair, 2 * pair + 1):
            pv = _softmax_pv(qg_ref[0, :, hq * dh:(hq + 1) * dh], kgt_ref[0, kv * dh:(kv + 1) * dh, :], vgx_ref[...])
            halves.append(pv[:, :dv] / pv[:, dv:])
        left = halves[0] if kv % 2 == 0 else pltpu.roll(halves[0], dh, 1)
        right = pltpu.roll(halves[1], dh, 1) if kv % 2 == 0 else halves[1]
        og_ref[0, :, pair * dv:(pair + 1) * dv] = jnp.where(lane < dh, left, right).astype(BF16)


def _attention(qd, qg, kv_bufs, lam_p, subln, *, dims, lam_init, tq, key_start, key_len):
    kd_t, kg_t, vd, vg = kv_bufs
    b, l, nq = qd.shape
    n_diff, n_gq, n_kv = dims["diff_heads"], dims["gqa_q"] // HEAD_DIM, dims["gqa_kv"] // HEAD_DIM
    assert n_kv * HEAD_DIM == 2 * HEAD_DIM and (n_gq // n_kv) % 2 == 0 and key_start % key_len == 0
    kb = key_start // key_len
    tq = min(tq, l)
    tok = lambda w: pl.BlockSpec((1, tq, w), lambda bi, i: (bi, i, 0))
    keys_t = lambda a: pl.BlockSpec((1, a.shape[1], key_len), lambda bi, i: (bi, 0, kb))
    vals = lambda a: pl.BlockSpec((1, key_len, a.shape[2]), lambda bi, i: (bi, kb, 0))
    const = lambda a: pl.BlockSpec(a.shape, lambda bi, i: (0,) * a.ndim)
    return pl.pallas_call(
        functools.partial(_attn_kernel, n_diff=n_diff, n_gq=n_gq, n_kv=n_kv, lam_init=lam_init),
        grid=(b, l // tq),
        in_specs=[tok(nq), tok(qg.shape[-1]), keys_t(kd_t), vals(vd), keys_t(kg_t), vals(vg),
                  const(lam_p), const(subln)],
        out_specs=[tok(dims["diff_v"]), tok(dims["gqa_q"])],
        out_shape=[jax.ShapeDtypeStruct((b, l, dims["diff_v"]), BF16),
                   jax.ShapeDtypeStruct((b, l, dims["gqa_q"]), BF16)],
        scratch_shapes=[pltpu.VMEM((n_diff, key_len, V7X_MXU_COLUMNS), BF16),
                        pltpu.VMEM((key_len, V7X_MXU_COLUMNS), BF16)],
        compiler_params=_params(("arbitrary", "arbitrary")),
        name="attention",
    )(qd, qg, kd_t, vd, kg_t, vg, lam_p, subln)


def _fourier_kernel(c_ref, s_ref, xf_ref, o_ref, *, nf, out_scale):
    y = (jnp.dot(c_ref[...], xf_ref[0, :, :nf], preferred_element_type=F32)
         - jnp.dot(s_ref[...], xf_ref[0, :, nf:], preferred_element_type=F32))
    o_ref[0] = (y * out_scale).astype(BF16)


def _fourier_dense(tables, xf, *, tr=512):
    dft_c, dft_s = tables
    b, l, nf2 = xf.shape
    nf = nf2 // 2
    tr = min(tr, l)
    return pl.pallas_call(
        functools.partial(_fourier_kernel, nf=nf, out_scale=float(l) ** -0.5),
        grid=(b, l // tr),
        in_specs=[pl.BlockSpec((tr, l), lambda bi, i: (i, 0)),
                  pl.BlockSpec((tr, l), lambda bi, i: (i, 0)),
                  pl.BlockSpec((1, l, nf2), lambda bi, i: (bi, 0, 0))],
        out_specs=pl.BlockSpec((1, tr, nf), lambda bi, i: (bi, i, 0)),
        out_shape=jax.ShapeDtypeStruct((b, l, nf), BF16),
        compiler_params=_params(("arbitrary", "arbitrary")),
        name="fourier",
    )(dft_c, dft_s, xf)


def _fft_a_kernel(z_ref, m_ref, cos_ref, sin_ref, o_ref, *, nf):
    n1, r = z_ref.shape[1], z_ref.shape[2]
    rows = n1 * r
    z = z_ref[0].reshape(rows, 2 * nf)
    t = jnp.dot(m_ref[...], z, preferred_element_type=F32)
    cz, sz = t[:rows], t[rows:]
    ar = cz[:, :nf] - sz[:, nf:]
    ai = -(sz[:, :nf] + cz[:, nf:])
    ct = _tile_lanes(cos_ref[0], nf)
    st = _tile_lanes(sin_ref[0], nf)
    o_ref[0, :, :, :nf] = (ar * ct + ai * st).reshape(n1, r, nf).astype(BF16)
    o_ref[0, :, :, nf:] = (ai * ct - ar * st).reshape(n1, r, nf).astype(BF16)


def _fft_c_kernel(b_ref, mc_ref, ms_ref, o_ref, *, nf, out_scale):
    r, n2 = b_ref.shape[1], b_ref.shape[2]
    blk = b_ref[0].reshape(r * n2, 2 * nf)
    y = (jnp.dot(mc_ref[...], blk[:, :nf], preferred_element_type=F32)
         + jnp.dot(ms_ref[...], blk[:, nf:], preferred_element_type=F32))
    o_ref[0] = (y * out_scale).reshape(n2, r, nf).astype(BF16)


def _fourier_two_stage(tables, xf):
    m_a, tw_cos, tw_sin, m_c_cos, m_c_sin = tables
    b, l, nf2 = xf.shape
    nf = nf2 // 2
    r = BF16_SUBLANES
    n2 = m_c_cos.shape[0] // r
    n1 = l // n2
    tile = lambda w: pl.BlockSpec((1, n1, r, w), lambda bi, s: (bi, 0, s, 0))
    const = lambda a: pl.BlockSpec(a.shape, lambda bi, s: (0,) * a.ndim)
    twiddle = pl.BlockSpec((1, n1 * r, 2 * HEAD_DIM), lambda bi, s: (s, 0, 0))
    staged = pl.pallas_call(
        functools.partial(_fft_a_kernel, nf=nf),
        grid=(b, n2 // r),
        in_specs=[tile(nf2), const(m_a), twiddle, twiddle],
        out_specs=tile(nf2),
        out_shape=jax.ShapeDtypeStruct((b, n1, n2, nf2), BF16),
        compiler_params=_params(("arbitrary", "arbitrary")),
        name="fourier_a",
    )(xf.reshape(b, n1, n2, nf2), m_a, tw_cos, tw_sin)
    out = pl.pallas_call(
        functools.partial(_fft_c_kernel, nf=nf, out_scale=float(l) ** -0.5),
        grid=(b, n1 // r),
        in_specs=[pl.BlockSpec((1, r, n2, nf2), lambda bi, s: (bi, s, 0, 0)), const(m_c_cos), const(m_c_sin)],
        out_specs=pl.BlockSpec((1, n2, r, nf), lambda bi, s: (bi, 0, s, 0)),
        out_shape=jax.ShapeDtypeStruct((b, n2, n1, nf), BF16),
        compiler_params=_params(("arbitrary", "arbitrary")),
        name="fourier_c",
    )(staged, m_c_cos, m_c_sin)
    return out.reshape(b, l, nf)


def _merge_kernel(x_ref, shift_ref, scale_ref, gate_ref, g_ref, wgate_ref, yf_ref, od_ref, og_ref, yc_ref,
                  wf_ref, wd_ref, wg_ref, wc_ref, wout_ref, gpost_ref, o_ref):
    x = x_ref[0]
    d = x.shape[-1]
    h = _modulated(x, g_ref[...], shift_ref[0], scale_ref[0]).astype(BF16)
    branches = ((yf_ref, wf_ref), (od_ref, wd_ref), (og_ref, wg_ref), (yc_ref, wc_ref))
    merged = None
    for k, (y_ref, w_ref) in enumerate(branches):
        gate = jax.nn.sigmoid(jnp.dot(h, wgate_ref[:, k * d:(k + 1) * d], preferred_element_type=F32))
        term = gate * jnp.dot(y_ref[0], w_ref[...], preferred_element_type=F32)
        merged = term if merged is None else merged + term
    mix = jnp.dot(merged.astype(BF16), wout_ref[...], preferred_element_type=F32)
    o_ref[0] = x + gate_ref[0] * _rms(mix, gpost_ref[...])


def _merge(x, mod, g_pre, w_gate, yf, od, og, yc, wf, wd, wg, wc, wout, g_post, *, tm):
    b, l, d = x.shape
    tm = min(tm, l)
    tok = lambda w: pl.BlockSpec((1, tm, w), lambda bi, i: (bi, i, 0))
    modc = lambda k: pl.BlockSpec((1, 1, d), lambda bi, i: (bi, 0, k))
    const = lambda a: pl.BlockSpec(a.shape, lambda bi, i: (0,) * a.ndim)
    return pl.pallas_call(
        _merge_kernel,
        grid=(b, l // tm),
        in_specs=[tok(d), modc(0), modc(1), modc(2), const(g_pre), const(w_gate),
                  tok(yf.shape[-1]), tok(od.shape[-1]), tok(og.shape[-1]), tok(yc.shape[-1]),
                  const(wf), const(wd), const(wg), const(wc), const(wout), const(g_post)],
        out_specs=tok(d),
        out_shape=jax.ShapeDtypeStruct((b, l, d), F32),
        compiler_params=_params(("arbitrary", "arbitrary")),
        name="merge",
    )(x, mod, mod, mod, g_pre, w_gate, yf, od, og, yc, wf, wd, wg, wc, wout, g_post)


def _mlp_kernel(x_ref, shift_ref, scale_ref, gate_ref, g_ref, w1_ref, w2_ref, gpost_ref, o_ref, *, ff_chunk):
    x = x_ref[0]
    h = _modulated(x, g_ref[...], shift_ref[0], scale_ref[0]).astype(BF16)
    acc = None
    for c0 in range(0, w1_ref.shape[-1], ff_chunk):
        u = jnp.maximum(jnp.dot(h, w1_ref[:, c0:c0 + ff_chunk], preferred_element_type=F32), 0.0)
        part = jnp.dot((u * u).astype(BF16), w2_ref[c0:c0 + ff_chunk, :], preferred_element_type=F32)
        acc = part if acc is None else acc + part
    o_ref[0] = x + gate_ref[0] * _rms(acc, gpost_ref[...])


def _mlp(x, mod, g_pre, w1, w2, g_post, *, tm):
    b, l, d = x.shape
    tm = min(tm, l)
    tok = lambda w: pl.BlockSpec((1, tm, w), lambda bi, i: (bi, i, 0))
    modc = lambda k: pl.BlockSpec((1, 1, d), lambda bi, i: (bi, 0, k))
    const = lambda a: pl.BlockSpec(a.shape, lambda bi, i: (0,) * a.ndim)
    return pl.pallas_call(
        functools.partial(_mlp_kernel, ff_chunk=1024),
        grid=(b, l // tm),
        in_specs=[tok(d), modc(3), modc(4), modc(5), const(g_pre), const(w1), const(w2), const(g_post)],
        out_specs=tok(d),
        out_shape=jax.ShapeDtypeStruct((b, l, d), F32),
        compiler_params=_params(("arbitrary", "arbitrary")),
        name="mlp",
    )(x, mod, mod, mod, g_pre, w1, w2, g_post)


def _angles(i, j, period):
    return (np.outer(i, j) % period).astype(np.float64) * (2.0 * np.pi / period)


def _mxu_table(values):
    return jnp.asarray(values, F32).astype(BF16)


def _rope_tables(seq_len):
    rows = seq_len // GRID_W
    row = np.repeat(np.arange(rows, dtype=np.float64), GRID_W)
    col = np.tile(np.arange(GRID_W, dtype=np.float64), rows)
    inv_freq = ROPE_THETA ** (-np.arange(ROPE_FREQS, dtype=np.float64) / ROPE_FREQS)
    ang_r, ang_c = row[:, None] * inv_freq, col[:, None] * inv_freq
    cos64 = np.concatenate([np.cos(ang_r)] * 2 + [np.cos(ang_c)] * 2, axis=-1)
    sin64 = np.concatenate([-np.sin(ang_r), np.sin(ang_r), -np.sin(ang_c), np.sin(ang_c)], axis=-1)
    return (jnp.asarray(np.concatenate([cos64] * 2, axis=-1), F32),
            jnp.asarray(np.concatenate([sin64] * 2, axis=-1), F32))


def _seq_dft(l):
    th = _angles(np.arange(l), np.arange(l), l)
    return _mxu_table(np.cos(th)), _mxu_table(np.sin(th))


def _two_stage_dft(l, n2):
    n1 = l // n2
    r = BF16_SUBLANES
    i1, i2 = np.arange(n1), np.arange(n2)
    eye = np.eye(r)
    th_a = _angles(i1, i1, n1)
    m_a = np.concatenate([np.kron(np.cos(th_a), eye), np.kron(np.sin(th_a), eye)], axis=0)
    th_t = _angles(i1, i2, l).reshape(n1, n2 // r, r)
    th_t = np.transpose(th_t, (1, 0, 2)).reshape(n2 // r, n1 * r)
    lanes = lambda t: jnp.asarray(np.broadcast_to(t[:, :, None], t.shape + (2 * HEAD_DIM,)), F32)
    th_c = _angles(i2, i2, n2)
    perm = lambda w: _mxu_table(np.einsum('kb,ij->kijb', w, eye).reshape(n2 * r, r * n2))
    return _mxu_table(m_a), lanes(np.cos(th_t)), lanes(np.sin(th_t)), perm(np.cos(th_c)), perm(np.sin(th_c))


def _channel_dft(n_groups):
    g = FOURIER_GROUP_DIM
    th = _angles(np.arange(g), np.arange(g), g)
    eye = np.eye(n_groups)
    scale = float(g) ** -0.5
    return _mxu_table(np.concatenate([np.kron(eye, np.cos(th) * scale), np.kron(eye, np.sin(th) * scale)], axis=-1))


def _seq_fourier(l):
    n2 = FOURIER_STAGE_LEN
    if l % n2 == 0 and (l // n2) % BF16_SUBLANES == 0:
        return functools.partial(_fourier_two_stage, _two_stage_dft(l, n2))
    return functools.partial(_fourier_dense, _seq_dft(l))


def _mix_and_mlp(x, mod, lw, tok, kv_bufs, fourier, *, dims, lam_init, key_start, key_len, tm_mlp):
    qd, qg, xf, vc = tok
    od, og = _attention(qd, qg, kv_bufs, lw["lam_p"], lw["subln"], dims=dims, lam_init=lam_init, tq=256,
                        key_start=key_start, key_len=key_len)
    yf = fourier(xf)
    yc = _conv(vc, lw["conv_w"], lw["conv_b"], lw["conv_ln_g"], lw["conv_ln_b"], rows=512)
    x = _merge(x, mod, lw["g_pre_mix"], lw["w_gate"], yf, od, og, yc, lw["w_br_f"], lw["w_br_d"], lw["w_br_g"],
               lw["w_br_c"], lw["w_out"], lw["g_post_mix"], tm=512)
    return _mlp(x, mod, lw["g_pre_mlp"], lw["w_ff1"], lw["w_ff2"], lw["g_post_mlp"], tm=tm_mlp)


def kernel(x, c, ctx, c_ctx, w_mod, b_mod, g_pre_mix, g_post_mix, g_pre_mlp, g_post_mlp, w_in, q_norm, k_norm,
           diff_lambda, diff_subln, conv_dw, conv_dw_bias, conv_ln_g, conv_ln_b, w_br_fourier, w_br_diff,
           w_br_gqa, w_br_conv, w_out, w_ff1, w_ff2):
    b, l, d = x.shape
    lc = ctx.shape[1]
    lk = l + lc
    depth = w_in.shape[0]
    dims = _dims(d)
    assert b + 1 <= MOD_ROWS and l % GRID_W == 0 and l % lc == 0

    rope_x = _rope_tables(l)
    rope_none = (jnp.ones((lc, 2 * HEAD_DIM), F32), jnp.zeros((lc, 2 * HEAD_DIM), F32))
    fourier_x, fourier_c = _seq_fourier(l), _seq_fourier(lc)
    n_groups = dims["fourier"] // FOURIER_GROUP_DIM
    heads = np.arange(dims["diff_qk"]) // HEAD_DIM
    ones_bd = jnp.asarray((heads[:, None] == heads[None, :]).astype(BF16))
    dft_ch = _channel_dft(n_groups)

    c_rows = jnp.zeros((MOD_ROWS, d), F32).at[:b].set(c).at[b].set(c_ctx)
    mod_all = _modulation(c_rows, w_mod, b_mod)

    n_tok_cols = w_in.shape[-1] - N_BRANCHES * d

    xc = ctx
    for li in range(depth):
        lam_init = 0.8 - 0.6 * math.exp(-0.3 * li)
        row = lambda a: a[li].reshape(1, -1)
        lw = dict(
            w_gate=lax.optimization_barrier(w_in[li, :, n_tok_cols:]).astype(BF16),
            g_pre_mix=row(g_pre_mix), g_post_mix=row(g_post_mix), g_pre_mlp=row(g_pre_mlp),
            g_post_mlp=row(g_post_mlp),
            qn=jnp.tile(q_norm[li], dims["gqa_q"] // HEAD_DIM).reshape(1, -1),
            kn=jnp.tile(k_norm[li], dims["gqa_kv"] // HEAD_DIM).reshape(1, -1),
            lam_p=diff_lambda[li], subln=row(diff_subln),
            conv_w=conv_dw[li], conv_b=row(conv_dw_bias), conv_ln_g=row(conv_ln_g), conv_ln_b=row(conv_ln_b),
            w_br_f=w_br_fourier[li].astype(BF16), w_br_d=w_br_diff[li].astype(BF16),
            w_br_g=w_br_gqa[li].astype(BF16), w_br_c=w_br_conv[li].astype(BF16), w_out=w_out[li].astype(BF16),
            w_ff1=w_ff1[li].astype(BF16), w_ff2=w_ff2[li].astype(BF16))
        mod_x = mod_all[li, :b].reshape(b, 1, N_MOD * d)
        mod_c = jnp.broadcast_to(mod_all[li, b].reshape(1, 1, N_MOD * d), (b, 1, N_MOD * d))
        last = li == depth - 1

        inproj = functools.partial(_inproj, g_pre=lw["g_pre_mix"], w_in=w_in, ones_bd=ones_bd, qn=lw["qn"],
                                   kn=lw["kn"], dft_ch=dft_ch, dims=dims, layer=li, n_tok_cols=n_tok_cols)
        kv_bufs = (jnp.zeros((b, dims["diff_qk"], lk), BF16), jnp.zeros((b, dims["gqa_kv"], lk), BF16),
                   jnp.zeros((b, lk, dims["diff_v"]), BF16), jnp.zeros((b, lk, dims["gqa_kv"]), BF16))
        tok_c, kv_bufs = inproj(xc, mod_c, cos_t=rope_none[0], sin_t=rope_none[1], kv_bufs=kv_bufs, kv_only=last,
                                row_start=l, tm=lc)
        tok_x, kv_bufs = inproj(x, mod_x, cos_t=rope_x[0], sin_t=rope_x[1], kv_bufs=kv_bufs, kv_only=False,
                                row_start=0, tm=512)
        mix = functools.partial(_mix_and_mlp, lw=lw, kv_bufs=kv_bufs, dims=dims, lam_init=lam_init)
        x = mix(x, mod_x, tok=tok_x, fourier=fourier_x, key_start=0, key_len=lk, tm_mlp=512)
        if not last:
            xc = mix(xc, mod_c, tok=tok_c, fourier=fourier_c, key_start=l, key_len=lc, tm_mlp=lc)
    return x
```

```python
import functools
import math

import jax
import jax.numpy as jnp
import numpy as np
from jax import lax
from jax.experimental import pallas as pl
from jax.experimental.pallas import tpu as pltpu

F32 = jnp.float32
BF16 = jnp.bfloat16

GRID_W = 64
HEAD_DIM = 64
ROPE_FREQS = HEAD_DIM // 4
ROPE_THETA = 10000.0
EPS = 1e-6
ATTN_SCALE = HEAD_DIM ** -0.5
Q_PRESCALE = ATTN_SCALE * math.log2(math.e)
FOURIER_GROUP_DIM = 64
CONV_KERNEL = 31
CONV_HALO = 16
N_BRANCHES = 4
N_MOD = 6
MOD_ROWS = 8

V7X_VMEM_LIMIT_BYTES = 56 * 1024 * 1024
V7X_MXU_COLUMNS = 256
SUBLANES = 8
BF16_SUBLANES = 16
FOURIER_STAGE_LEN = 64


def _dims(d_model):
    fourier = 3 * d_model // 8
    diff_heads = d_model // 256
    diff_qk = diff_heads * 2 * HEAD_DIM
    diff_v = diff_heads * 2 * HEAD_DIM
    gqa_q = (d_model // 128) * HEAD_DIM
    gqa_kv = (d_model // 128 // 4) * HEAD_DIM
    conv = 3 * d_model // 8
    return dict(fourier=fourier, diff_heads=diff_heads, diff_qk=diff_qk, diff_v=diff_v,
                gqa_q=gqa_q, gqa_kv=gqa_kv, conv=conv)


def _params(semantics):
    return pltpu.CompilerParams(dimension_semantics=semantics, vmem_limit_bytes=V7X_VMEM_LIMIT_BYTES)


def _mod_kernel(c_ref, w_ref, b_ref, o_ref):
    c = c_ref[...]
    a = c * jax.nn.sigmoid(c)
    o_ref[...] = jnp.dot(a, w_ref[...], preferred_element_type=F32,
                         precision=lax.Precision.HIGHEST) + b_ref[...]


def _modulation(c_rows, w_mod, b_mod):
    depth, d, n = w_mod.shape
    tn = 2 * d
    return pl.pallas_call(
        _mod_kernel,
        grid=(depth, n // tn),
        in_specs=[pl.BlockSpec((MOD_ROWS, d), lambda l, j: (0, 0)),
                  pl.BlockSpec((None, d, tn), lambda l, j: (l, 0, j)),
                  pl.BlockSpec((None, 1, tn), lambda l, j: (l, 0, j))],
        out_specs=pl.BlockSpec((None, MOD_ROWS, tn), lambda l, j: (l, 0, j)),
        out_shape=jax.ShapeDtypeStruct((depth, MOD_ROWS, n), F32),
        compiler_params=_params(("arbitrary", "arbitrary")),
        name="modulation",
    )(c_rows, w_mod, b_mod.reshape(depth, 1, n))


def _modulated(x, g, shift, scale):
    h = x * lax.rsqrt(jnp.mean(x * x, axis=-1, keepdims=True) + EPS) * g
    return h * (1.0 + scale) + shift


def _rms(x, g):
    return x * lax.rsqrt(jnp.mean(x * x, axis=-1, keepdims=True) + EPS) * g


def _swap_halves(t):
    w = t.shape[-1]
    lane = lax.broadcasted_iota(jnp.int32, t.shape, 1)
    from_right = pltpu.roll(t, w - ROPE_FREQS, 1)
    from_left = pltpu.roll(t, ROPE_FREQS, 1)
    return jnp.where((lane % (2 * ROPE_FREQS)) < ROPE_FREQS, from_right, from_left)


def _tile_lanes(t, width):
    reps = width // t.shape[-1]
    return t if reps == 1 else jnp.concatenate([t] * reps, axis=-1)


def _head_norm(t, ones_bd, gain):
    w = t.shape[-1]
    t2 = t * t
    hi = t2.astype(BF16)
    lo = (t2 - hi.astype(F32)).astype(BF16)
    e = ones_bd[:w, :w]
    ss = (jnp.dot(hi, e, preferred_element_type=F32) + jnp.dot(lo, e, preferred_element_type=F32))
    return t * lax.rsqrt(ss * (1.0 / HEAD_DIM) + EPS) * gain


def _inproj_kernel(x_ref, shift_ref, scale_ref, g_ref, w_ref, ones_ref, cos_ref, sin_ref, qn_ref, kn_ref,
                   dft_ref, *rest, dims, kv_only, n_alias):
    out_refs = rest[n_alias:]
    nq, nkv, nv, nf, nc = dims["diff_qk"], dims["gqa_kv"], dims["diff_v"], dims["fourier"], dims["conv"]
    ngq = dims["gqa_q"]
    h = _modulated(x_ref[0], g_ref[...], shift_ref[0], scale_ref[0]).astype(BF16)

    def proj(start, width):
        return jnp.dot(h, w_ref[:, start:start + width].astype(BF16), preferred_element_type=F32)

    def rope(t):
        w = t.shape[-1]
        return t * _tile_lanes(cos_ref[...], w) + _swap_halves(t) * _tile_lanes(sin_ref[...], w)

    ones_bd = ones_ref[...]
    off = 0
    dk = proj(off, nq); off += nq
    dv = proj(off, nv); off += nv
    gk = proj(off, nkv); off += nkv
    gv = proj(off, nkv); off += nkv
    kdt_ref, kgt_ref, vd_ref, vg_ref = out_refs[-4:]
    kdt_ref[0] = rope(dk).T.astype(BF16)
    kgt_ref[0] = rope(_head_norm(gk, ones_bd, kn_ref[...])).T.astype(BF16)
    vd_ref[0] = dv.astype(BF16)
    vg_ref[0] = gv.astype(BF16)
    if not kv_only:
        qd_ref, qg_ref, xf_ref, vc_ref = out_refs[:4]
        uf = proj(off, nf).astype(BF16); off += nf
        xf_ref[0] = jnp.dot(uf, dft_ref[...], preferred_element_type=F32).astype(BF16)
        dq = proj(off, nq); off += nq
        qd_ref[0] = (rope(dq) * Q_PRESCALE).astype(BF16)
        gq = proj(off, ngq); off += ngq
        qg_ref[0] = (rope(_head_norm(gq, ones_bd, qn_ref[...])) * Q_PRESCALE).astype(BF16)
        glu = proj(off, 2 * nc)
        vc_ref[0] = glu[:, :nc] * jax.nn.sigmoid(glu[:, nc:])


def _inproj(x, mod, g_pre, w_in, ones_bd, cos_t, sin_t, qn, kn, dft_ch, kv_bufs, *, dims, layer, n_tok_cols,
            kv_only, row_start, tm):
    b, l, d = x.shape
    tm = min(tm, l)
    nq, nkv, nv, nf, nc, ngq = (dims["diff_qk"], dims["gqa_kv"], dims["diff_v"], dims["fourier"],
                                dims["conv"], dims["gqa_q"])
    blk0 = row_start // tm
    assert row_start % tm == 0
    tok = lambda w: pl.BlockSpec((1, tm, w), lambda bi, i: (bi, i, 0))
    const = lambda shape: pl.BlockSpec(shape, lambda bi, i: (0,) * len(shape))
    tok_outs = [] if kv_only else [((b, l, nq), BF16), ((b, l, ngq), BF16), ((b, l, 2 * nf), BF16), ((b, l, nc), F32)]
    kv_shapes = [(a.shape, a.dtype) for a in kv_bufs]
    kv_specs = [pl.BlockSpec((1, nq, tm), lambda bi, i: (bi, 0, blk0 + i)),
                pl.BlockSpec((1, nkv, tm), lambda bi, i: (bi, 0, blk0 + i)),
                pl.BlockSpec((1, tm, nv), lambda bi, i: (bi, blk0 + i, 0)),
                pl.BlockSpec((1, tm, nkv), lambda bi, i: (bi, blk0 + i, 0))]
    n_alias = len(kv_bufs)
    n_in = 11
    aliases = {n_in + k: len(tok_outs) + k for k in range(n_alias)}
    outs = pl.pallas_call(
        functools.partial(_inproj_kernel, dims=dims, kv_only=kv_only, n_alias=n_alias),
        grid=(b, l // tm),
        in_specs=[tok(d),
                  pl.BlockSpec((1, 1, d), lambda bi, i: (bi, 0, 0)),
                  pl.BlockSpec((1, 1, d), lambda bi, i: (bi, 0, 1)),
                  const((1, d)),
                  pl.BlockSpec((None, d, n_tok_cols), lambda bi, i: (layer, 0, 0)),
                  const(ones_bd.shape),
                  pl.BlockSpec((tm, 2 * HEAD_DIM), lambda bi, i: (i, 0)),
                  pl.BlockSpec((tm, 2 * HEAD_DIM), lambda bi, i: (i, 0)),
                  const(qn.shape), const(kn.shape), const(dft_ch.shape)]
                 + [pl.BlockSpec(memory_space=pl.ANY)] * n_alias,
        out_specs=[tok(s[-1]) for s, _ in tok_outs] + kv_specs,
        out_shape=[jax.ShapeDtypeStruct(s, dt) for s, dt in tok_outs + kv_shapes],
        input_output_aliases=aliases,
        compiler_params=_params(("arbitrary", "arbitrary")),
        name="inproj_kv" if kv_only else "inproj",
    )(x, mod, mod, g_pre, w_in, ones_bd, cos_t, sin_t, qn, kn, dft_ch, *kv_bufs)
    return outs[:len(tok_outs)], outs[len(tok_outs):]


def _conv_ln_silu(win_ref, w_ref, b_ref, g_ref, beta_ref, yc_ref, *, rows, chunk):
    first = CONV_HALO - CONV_KERNEL // 2
    for r0 in range(0, rows, chunk):
        acc = jnp.zeros((chunk, w_ref.shape[-1]), F32) + b_ref[...]
        for phase in range(SUBLANES):
            part = None
            for t in range(CONV_KERNEL):
                if (t + first) % SUBLANES != phase:
                    continue
                base = r0 + t + first - phase
                term = w_ref[t:t + 1, :] * win_ref[base:base + chunk + SUBLANES, :]
                part = term if part is None else part + term
            acc = acc + part[phase:phase + chunk, :]
        mu = jnp.mean(acc, axis=-1, keepdims=True)
        cen = acc - mu
        var = jnp.mean(cen * cen, axis=-1, keepdims=True)
        y = cen * lax.rsqrt(var + EPS) * g_ref[...] + beta_ref[...]
        yc_ref[r0:r0 + chunk, :] = (y * jax.nn.sigmoid(y)).astype(BF16)


def _softmax_pv(q, k_t, v_ext):
    s = jnp.dot(q, k_t, preferred_element_type=F32)
    e = jnp.exp2(s - jnp.max(s, axis=-1, keepdims=True))
    return jnp.dot(e.astype(BF16), v_ext, preferred_element_type=F32)


def _conv_kernel(vprev_ref, vcur_ref, vnext_ref, cw_ref, cb_ref, cg_ref, cbeta_ref, yc_ref, win_ref):
    i = pl.program_id(1)
    rows = vcur_ref.shape[1]
    win_ref[:CONV_HALO, :] = jnp.where(i > 0, vprev_ref[0], 0.0)
    win_ref[CONV_HALO:CONV_HALO + rows, :] = vcur_ref[0]
    win_ref[CONV_HALO + rows:, :] = jnp.where(i < pl.num_programs(1) - 1, vnext_ref[0], 0.0)
    _conv_ln_silu(win_ref, cw_ref, cb_ref, cg_ref, cbeta_ref, yc_ref.at[0], rows=rows, chunk=32)


def _conv(vc, conv_w, conv_b, conv_g, conv_beta, *, rows):
    b, l, c = vc.shape
    rows = min(rows, l)
    hb = rows // CONV_HALO
    n_hb = l // CONV_HALO
    const = lambda a: pl.BlockSpec(a.shape, lambda bi, i: (0,) * a.ndim)
    prev = pl.BlockSpec((1, CONV_HALO, c), lambda bi, i: (bi, jnp.maximum(i * hb - 1, 0), 0))
    nxt = pl.BlockSpec((1, CONV_HALO, c), lambda bi, i: (bi, jnp.minimum((i + 1) * hb, n_hb - 1), 0))
    tile = pl.BlockSpec((1, rows, c), lambda bi, i: (bi, i, 0))
    return pl.pallas_call(
        _conv_kernel,
        grid=(b, l // rows),
        in_specs=[prev, tile, nxt, const(conv_w), const(conv_b), const(conv_g), const(conv_beta)],
        out_specs=tile,
        out_shape=jax.ShapeDtypeStruct((b, l, c), BF16),
        scratch_shapes=[pltpu.VMEM((rows + 2 * CONV_HALO, c), F32)],
        compiler_params=_params(("arbitrary", "arbitrary")),
        name="conv",
    )(vc, vc, vc, conv_w, conv_b, conv_g, conv_beta)


def _attn_kernel(qd_ref, qg_ref, kdt_ref, vd_ref, kgt_ref, vg_ref, lam_ref, subln_ref, od_ref, og_ref, odl_ref,
                 vdx_ref, vgx_ref, e_ref, m0_ref, *, n_diff, n_gq, n_kv, lam_init):
    dh = HEAD_DIM
    dv = 2 * HEAD_DIM
    i = pl.program_id(1)
    tq = qd_ref.shape[1]
    last = n_diff - 1

    @pl.when(i == 0)
    def _():
        for hd in range(n_diff):
            vdx_ref[hd, :, :dv] = vd_ref[0, :, hd * dv:(hd + 1) * dv]
            vdx_ref[hd, :, dv:] = jnp.ones((vd_ref.shape[1], V7X_MXU_COLUMNS - dv), BF16)
        vgx_ref[:, :n_kv * dh] = vg_ref[0]
        vgx_ref[:, n_kv * dh:] = jnp.ones((vg_ref.shape[1], V7X_MXU_COLUMNS - n_kv * dh), BF16)
        e_ref[...] = jnp.ones(e_ref.shape, BF16)
        m0_ref[...] = jnp.zeros(m0_ref.shape, F32)

    lv = lam_ref[...]
    lam = (jnp.exp(jnp.sum(lv[0:1] * lv[1:2], axis=-1, keepdims=True))
           - jnp.exp(jnp.sum(lv[2:3] * lv[3:4], axis=-1, keepdims=True)) + lam_init)

    def diff_head_out(m0, pv1):
        o = m0 - lam * (pv1[:, :dv] / pv1[:, dv:])
        return (_rms(o, subln_ref[...]) * (1.0 - lam_init)).astype(BF16)

    def finish_last_diff_head(row0):
        pv1 = jnp.dot(e_ref[...], vdx_ref[last], preferred_element_type=F32)
        odl_ref[0, pl.ds(row0, tq), :] = diff_head_out(m0_ref[...], pv1)

    finish_last_diff_head(pl.multiple_of(jnp.maximum(i - 1, 0) * tq, tq))
    for hd in range(last):
        c = 2 * hd * dh
        pv0 = _softmax_pv(qd_ref[0, :, c:c + dh], kdt_ref[0, c:c + dh, :], vdx_ref[hd])
        pv1 = _softmax_pv(qd_ref[0, :, c + dh:c + 2 * dh], kdt_ref[0, c + dh:c + 2 * dh, :], vdx_ref[hd])
        od_ref[0, :, hd * dv:(hd + 1) * dv] = diff_head_out(pv0[:, :dv] / pv0[:, dv:], pv1)
    group = n_gq // n_kv
    lane = lax.broadcasted_iota(jnp.int32, (qg_ref.shape[1], dv), 1)
    for pair in range(n_gq // 2):
        kv = (2 * pair) // group
        halves = []
        for hq in (2 * pair, 2 * pair + 1):
            pv = _softmax_pv(qg_ref[0, :, hq * dh:(hq + 1) * dh], kgt_ref[0, kv * dh:(kv + 1) * dh, :], vgx_ref[...])
            halves.append(pv[:, :dv] / pv[:, dv:])
        left = halves[0] if kv % 2 == 0 else pltpu.roll(halves[0], dh, 1)
        right = pltpu.roll(halves[1], dh, 1) if kv % 2 == 0 else halves[1]
        og_ref[0, :, pair * dv:(pair + 1) * dv] = jnp.where(lane < dh, left, right).astype(BF16)

    c = 2 * last * dh
    pv0 = _softmax_pv(qd_ref[0, :, c:c + dh], kdt_ref[0, c:c + dh, :], vdx_ref[last])
    m0_ref[...] = pv0[:, :dv] / pv0[:, dv:]
    s = jnp.dot(qd_ref[0, :, c + dh:c + 2 * dh], kdt_ref[0, c + dh:c + 2 * dh, :], preferred_element_type=F32)
    e_ref[...] = jnp.exp2(s - jnp.max(s, axis=-1, keepdims=True)).astype(BF16)

    @pl.when(i == pl.num_programs(1) - 1)
    def _():
        finish_last_diff_head(pl.multiple_of(i * tq, tq))


def _attention(qd, qg, kv_bufs, lam_p, subln, *, dims, lam_init, tq, key_start, key_len):
    kd_t, kg_t, vd, vg = kv_bufs
    b, l, nq = qd.shape
    n_diff, n_gq, n_kv = dims["diff_heads"], dims["gqa_q"] // HEAD_DIM, dims["gqa_kv"] // HEAD_DIM
    assert n_kv * HEAD_DIM == 2 * HEAD_DIM and (n_gq // n_kv) % 2 == 0 and key_start % key_len == 0
    kb = key_start // key_len
    tq = min(tq, l)
    tok = lambda w: pl.BlockSpec((1, tq, w), lambda bi, i: (bi, i, 0))
    keys_t = lambda a: pl.BlockSpec((1, a.shape[1], key_len), lambda bi, i: (bi, 0, kb))
    vals = lambda a: pl.BlockSpec((1, key_len, a.shape[2]), lambda bi, i: (bi, kb, 0))
    const = lambda a: pl.BlockSpec(a.shape, lambda bi, i: (0,) * a.ndim)
    dv = 2 * HEAD_DIM
    od, og, od_last = pl.pallas_call(
        functools.partial(_attn_kernel, n_diff=n_diff, n_gq=n_gq, n_kv=n_kv, lam_init=lam_init),
        grid=(b, l // tq),
        in_specs=[tok(nq), tok(qg.shape[-1]), keys_t(kd_t), vals(vd), keys_t(kg_t), vals(vg),
                  const(lam_p), const(subln)],
        out_specs=[tok((n_diff - 1) * dv), tok(dims["gqa_q"]), pl.BlockSpec((1, l, dv), lambda bi, i: (bi, 0, 0))],
        out_shape=[jax.ShapeDtypeStruct((b, l, (n_diff - 1) * dv), BF16),
                   jax.ShapeDtypeStruct((b, l, dims["gqa_q"]), BF16),
                   jax.ShapeDtypeStruct((b, l, dv), BF16)],
        scratch_shapes=[pltpu.VMEM((n_diff, key_len, V7X_MXU_COLUMNS), BF16),
                        pltpu.VMEM((key_len, V7X_MXU_COLUMNS), BF16),
                        pltpu.VMEM((tq, key_len), BF16),
                        pltpu.VMEM((tq, dv), F32)],
        compiler_params=_params(("arbitrary", "arbitrary")),
        name="attention",
    )(qd, qg, kd_t, vd, kg_t, vg, lam_p, subln)
    return od, od_last, og


def _fourier_kernel(c_ref, s_ref, xf_ref, o_ref, *, nf, out_scale):
    y = (jnp.dot(c_ref[...], xf_ref[0, :, :nf], preferred_element_type=F32)
         - jnp.dot(s_ref[...], xf_ref[0, :, nf:], preferred_element_type=F32))
    o_ref[0] = (y * out_scale).astype(BF16)


def _fourier_dense(tables, xf, *, tr=512):
    dft_c, dft_s = tables
    b, l, nf2 = xf.shape
    nf = nf2 // 2
    tr = min(tr, l)
    return pl.pallas_call(
        functools.partial(_fourier_kernel, nf=nf, out_scale=float(l) ** -0.5),
        grid=(b, l // tr),
        in_specs=[pl.BlockSpec((tr, l), lambda bi, i: (i, 0)),
                  pl.BlockSpec((tr, l), lambda bi, i: (i, 0)),
                  pl.BlockSpec((1, l, nf2), lambda bi, i: (bi, 0, 0))],
        out_specs=pl.BlockSpec((1, tr, nf), lambda bi, i: (bi, i, 0)),
        out_shape=jax.ShapeDtypeStruct((b, l, nf), BF16),
        compiler_params=_params(("arbitrary", "arbitrary")),
        name="fourier",
    )(dft_c, dft_s, xf)


def _fft_a_kernel(z_ref, m_ref, cos_ref, sin_ref, o_ref, *, nf):
    n1, r = z_ref.shape[1], z_ref.shape[2]
    rows = n1 * r
    z = z_ref[0].reshape(rows, 2 * nf)
    t = jnp.dot(m_ref[...], z, preferred_element_type=F32)
    cz, sz = t[:rows], t[rows:]
    ar = cz[:, :nf] - sz[:, nf:]
    ai = -(sz[:, :nf] + cz[:, nf:])
    ct = _tile_lanes(cos_ref[0], nf)
    st = _tile_lanes(sin_ref[0], nf)
    o_ref[0, :, :, :nf] = (ar * ct + ai * st).reshape(n1, r, nf).astype(BF16)
    o_ref[0, :, :, nf:] = (ai * ct - ar * st).reshape(n1, r, nf).astype(BF16)


def _fft_c_kernel(b_ref, mc_ref, ms_ref, o_ref, *, nf, out_scale):
    r, n2 = b_ref.shape[1], b_ref.shape[2]
    blk = b_ref[0].reshape(r * n2, 2 * nf)
    y = (jnp.dot(mc_ref[...], blk[:, :nf], preferred_element_type=F32)
         + jnp.dot(ms_ref[...], blk[:, nf:], preferred_element_type=F32))
    o_ref[0] = (y * out_scale).reshape(n2, r, nf).astype(BF16)


def _fourier_two_stage(tables, xf):
    m_a, tw_cos, tw_sin, m_c_cos, m_c_sin = tables
    b, l, nf2 = xf.shape
    nf = nf2 // 2
    r = BF16_SUBLANES
    n2 = m_c_cos.shape[0] // r
    n1 = l // n2
    tile = lambda w: pl.BlockSpec((1, n1, r, w), lambda bi, s: (bi, 0, s, 0))
    const = lambda a: pl.BlockSpec(a.shape, lambda bi, s: (0,) * a.ndim)
    twiddle = pl.BlockSpec((1, n1 * r, 2 * HEAD_DIM), lambda bi, s: (s, 0, 0))
    staged = pl.pallas_call(
        functools.partial(_fft_a_kernel, nf=nf),
        grid=(b, n2 // r),
        in_specs=[tile(nf2), const(m_a), twiddle, twiddle],
        out_specs=tile(nf2),
        out_shape=jax.ShapeDtypeStruct((b, n1, n2, nf2), BF16),
        compiler_params=_params(("arbitrary", "arbitrary")),
        name="fourier_a",
    )(xf.reshape(b, n1, n2, nf2), m_a, tw_cos, tw_sin)
    out = pl.pallas_call(
        functools.partial(_fft_c_kernel, nf=nf, out_scale=float(l) ** -0.5),
        grid=(b, n1 // r),
        in_specs=[pl.BlockSpec((1, r, n2, nf2), lambda bi, s: (bi, s, 0, 0)), const(m_c_cos), const(m_c_sin)],
        out_specs=pl.BlockSpec((1, n2, r, nf), lambda bi, s: (bi, 0, s, 0)),
        out_shape=jax.ShapeDtypeStruct((b, n2, n1, nf), BF16),
        compiler_params=_params(("arbitrary", "arbitrary")),
        name="fourier_c",
    )(staged, m_c_cos, m_c_sin)
    return out.reshape(b, l, nf)


def _merge_kernel(x_ref, shift_ref, scale_ref, gate_ref, g_ref, wgate_ref, yf_ref, od_ref, odl_ref, og_ref, yc_ref,
                  wf_ref, wd_ref, wg_ref, wc_ref, wout_ref, gpost_ref, o_ref):
    x = x_ref[0]
    d = x.shape[-1]
    h = _modulated(x, g_ref[...], shift_ref[0], scale_ref[0]).astype(BF16)
    diff = jnp.concatenate([od_ref[0], odl_ref[0]], axis=-1)
    branches = ((yf_ref[0], wf_ref), (diff, wd_ref), (og_ref[0], wg_ref), (yc_ref[0], wc_ref))
    merged = None
    for k, (y, w_ref) in enumerate(branches):
        gate = jax.nn.sigmoid(jnp.dot(h, wgate_ref[:, k * d:(k + 1) * d], preferred_element_type=F32))
        term = gate * jnp.dot(y, w_ref[...], preferred_element_type=F32)
        merged = term if merged is None else merged + term
    mix = jnp.dot(merged.astype(BF16), wout_ref[...], preferred_element_type=F32)
    o_ref[0] = x + gate_ref[0] * _rms(mix, gpost_ref[...])


def _merge(x, mod, g_pre, w_gate, yf, od, od_last, og, yc, wf, wd, wg, wc, wout, g_post, *, tm):
    b, l, d = x.shape
    tm = min(tm, l)
    tok = lambda w: pl.BlockSpec((1, tm, w), lambda bi, i: (bi, i, 0))
    modc = lambda k: pl.BlockSpec((1, 1, d), lambda bi, i: (bi, 0, k))
    const = lambda a: pl.BlockSpec(a.shape, lambda bi, i: (0,) * a.ndim)
    return pl.pallas_call(
        _merge_kernel,
        grid=(b, l // tm),
        in_specs=[tok(d), modc(0), modc(1), modc(2), const(g_pre), const(w_gate),
                  tok(yf.shape[-1]), tok(od.shape[-1]), tok(od_last.shape[-1]), tok(og.shape[-1]), tok(yc.shape[-1]),
                  const(wf), const(wd), const(wg), const(wc), const(wout), const(g_post)],
        out_specs=tok(d),
        out_shape=jax.ShapeDtypeStruct((b, l, d), F32),
        compiler_params=_params(("arbitrary", "arbitrary")),
        name="merge",
    )(x, mod, mod, mod, g_pre, w_gate, yf, od, od_last, og, yc, wf, wd, wg, wc, wout, g_post)


def _mlp_kernel(x_ref, shift_ref, scale_ref, gate_ref, g_ref, w1_ref, w2_ref, gpost_ref, o_ref, *, ff_chunk):
    x = x_ref[0]
    h = _modulated(x, g_ref[...], shift_ref[0], scale_ref[0]).astype(BF16)
    acc = None
    for c0 in range(0, w1_ref.shape[-1], ff_chunk):
        u = jnp.maximum(jnp.dot(h, w1_ref[:, c0:c0 + ff_chunk], preferred_element_type=F32), 0.0)
        part = jnp.dot((u * u).astype(BF16), w2_ref[c0:c0 + ff_chunk, :], preferred_element_type=F32)
        acc = part if acc is None else acc + part
    o_ref[0] = x + gate_ref[0] * _rms(acc, gpost_ref[...])


def _mlp(x, mod, g_pre, w1, w2, g_post, *, tm):
    b, l, d = x.shape
    tm = min(tm, l)
    tok = lambda w: pl.BlockSpec((1, tm, w), lambda bi, i: (bi, i, 0))
    modc = lambda k: pl.BlockSpec((1, 1, d), lambda bi, i: (bi, 0, k))
    const = lambda a: pl.BlockSpec(a.shape, lambda bi, i: (0,) * a.ndim)
    return pl.pallas_call(
        functools.partial(_mlp_kernel, ff_chunk=1024),
        grid=(b, l // tm),
        in_specs=[tok(d), modc(3), modc(4), modc(5), const(g_pre), const(w1), const(w2), const(g_post)],
        out_specs=tok(d),
        out_shape=jax.ShapeDtypeStruct((b, l, d), F32),
        compiler_params=_params(("arbitrary", "arbitrary")),
        name="mlp",
    )(x, mod, mod, mod, g_pre, w1, w2, g_post)


def _angles(i, j, period):
    return (np.outer(i, j) % period).astype(np.float64) * (2.0 * np.pi / period)


def _mxu_table(values):
    return jnp.asarray(values, F32).astype(BF16)


def _rope_tables(seq_len):
    rows = seq_len // GRID_W
    row = np.repeat(np.arange(rows, dtype=np.float64), GRID_W)
    col = np.tile(np.arange(GRID_W, dtype=np.float64), rows)
    inv_freq = ROPE_THETA ** (-np.arange(ROPE_FREQS, dtype=np.float64) / ROPE_FREQS)
    ang_r, ang_c = row[:, None] * inv_freq, col[:, None] * inv_freq
    cos64 = np.concatenate([np.cos(ang_r)] * 2 + [np.cos(ang_c)] * 2, axis=-1)
    sin64 = np.concatenate([-np.sin(ang_r), np.sin(ang_r), -np.sin(ang_c), np.sin(ang_c)], axis=-1)
    return (jnp.asarray(np.concatenate([cos64] * 2, axis=-1), F32),
            jnp.asarray(np.concatenate([sin64] * 2, axis=-1), F32))


def _seq_dft(l):
    th = _angles(np.arange(l), np.arange(l), l)
    return _mxu_table(np.cos(th)), _mxu_table(np.sin(th))


def _two_stage_dft(l, n2):
    n1 = l // n2
    r = BF16_SUBLANES
    i1, i2 = np.arange(n1), np.arange(n2)
    eye = np.eye(r)
    th_a = _angles(i1, i1, n1)
    m_a = np.concatenate([np.kron(np.cos(th_a), eye), np.kron(np.sin(th_a), eye)], axis=0)
    th_t = _angles(i1, i2, l).reshape(n1, n2 // r, r)
    th_t = np.transpose(th_t, (1, 0, 2)).reshape(n2 // r, n1 * r)
    lanes = lambda t: jnp.asarray(np.broadcast_to(t[:, :, None], t.shape + (2 * HEAD_DIM,)), F32)
    th_c = _angles(i2, i2, n2)
    perm = lambda w: _mxu_table(np.einsum('kb,ij->kijb', w, eye).reshape(n2 * r, r * n2))
    return _mxu_table(m_a), lanes(np.cos(th_t)), lanes(np.sin(th_t)), perm(np.cos(th_c)), perm(np.sin(th_c))


def _channel_dft(n_groups):
    g = FOURIER_GROUP_DIM
    th = _angles(np.arange(g), np.arange(g), g)
    eye = np.eye(n_groups)
    scale = float(g) ** -0.5
    return _mxu_table(np.concatenate([np.kron(eye, np.cos(th) * scale), np.kron(eye, np.sin(th) * scale)], axis=-1))


def _seq_fourier(l):
    n2 = FOURIER_STAGE_LEN
    if l % n2 == 0 and (l // n2) % BF16_SUBLANES == 0:
        return functools.partial(_fourier_two_stage, _two_stage_dft(l, n2))
    return functools.partial(_fourier_dense, _seq_dft(l))


def _mix_and_mlp(x, mod, lw, tok, kv_bufs, fourier, *, dims, lam_init, key_start, key_len, tm_mlp):
    qd, qg, xf, vc = tok
    od, od_last, og = _attention(qd, qg, kv_bufs, lw["lam_p"], lw["subln"], dims=dims, lam_init=lam_init, tq=256,
                                 key_start=key_start, key_len=key_len)
    yf = fourier(xf)
    yc = _conv(vc, lw["conv_w"], lw["conv_b"], lw["conv_ln_g"], lw["conv_ln_b"], rows=512)
    x = _merge(x, mod, lw["g_pre_mix"], lw["w_gate"], yf, od, od_last, og, yc, lw["w_br_f"], lw["w_br_d"],
               lw["w_br_g"], lw["w_br_c"], lw["w_out"], lw["g_post_mix"], tm=512)
    return _mlp(x, mod, lw["g_pre_mlp"], lw["w_ff1"], lw["w_ff2"], lw["g_post_mlp"], tm=tm_mlp)


def kernel(x, c, ctx, c_ctx, w_mod, b_mod, g_pre_mix, g_post_mix, g_pre_mlp, g_post_mlp, w_in, q_norm, k_norm,
           diff_lambda, diff_subln, conv_dw, conv_dw_bias, conv_ln_g, conv_ln_b, w_br_fourier, w_br_diff,
           w_br_gqa, w_br_conv, w_out, w_ff1, w_ff2):
    b, l, d = x.shape
    lc = ctx.shape[1]
    lk = l + lc
    depth = w_in.shape[0]
    dims = _dims(d)
    assert b + 1 <= MOD_ROWS and l % GRID_W == 0 and l % lc == 0

    rope_x = _rope_tables(l)
    rope_none = (jnp.ones((lc, 2 * HEAD_DIM), F32), jnp.zeros((lc, 2 * HEAD_DIM), F32))
    fourier_x, fourier_c = _seq_fourier(l), _seq_fourier(lc)
    n_groups = dims["fourier"] // FOURIER_GROUP_DIM
    heads = np.arange(dims["diff_qk"]) // HEAD_DIM
    ones_bd = jnp.asarray((heads[:, None] == heads[None, :]).astype(BF16))
    dft_ch = _channel_dft(n_groups)

    c_rows = jnp.zeros((MOD_ROWS, d), F32).at[:b].set(c).at[b].set(c_ctx)
    mod_all = _modulation(c_rows, w_mod, b_mod)

    n_tok_cols = w_in.shape[-1] - N_BRANCHES * d

    xc = ctx
    for li in range(depth):
        lam_init = 0.8 - 0.6 * math.exp(-0.3 * li)
        row = lambda a: a[li].reshape(1, -1)
        lw = dict(
            w_gate=lax.optimization_barrier(w_in[li, :, n_tok_cols:]).astype(BF16),
            g_pre_mix=row(g_pre_mix), g_post_mix=row(g_post_mix), g_pre_mlp=row(g_pre_mlp),
            g_post_mlp=row(g_post_mlp),
            qn=jnp.tile(q_norm[li], dims["gqa_q"] // HEAD_DIM).reshape(1, -1),
            kn=jnp.tile(k_norm[li], dims["gqa_kv"] // HEAD_DIM).reshape(1, -1),
            lam_p=diff_lambda[li], subln=row(diff_subln),
            conv_w=conv_dw[li], conv_b=row(conv_dw_bias), conv_ln_g=row(conv_ln_g), conv_ln_b=row(conv_ln_b),
            w_br_f=w_br_fourier[li].astype(BF16), w_br_d=w_br_diff[li].astype(BF16),
            w_br_g=w_br_gqa[li].astype(BF16), w_br_c=w_br_conv[li].astype(BF16), w_out=w_out[li].astype(BF16),
            w_ff1=w_ff1[li].astype(BF16), w_ff2=w_ff2[li].astype(BF16))
        mod_x = mod_all[li, :b].reshape(b, 1, N_MOD * d)
        mod_c = jnp.broadcast_to(mod_all[li, b].reshape(1, 1, N_MOD * d), (b, 1, N_MOD * d))
        last = li == depth - 1

        inproj = functools.partial(_inproj, g_pre=lw["g_pre_mix"], w_in=w_in, ones_bd=ones_bd, qn=lw["qn"],
                                   kn=lw["kn"], dft_ch=dft_ch, dims=dims, layer=li, n_tok_cols=n_tok_cols)
        kv_bufs = (jnp.zeros((b, dims["diff_qk"], lk), BF16), jnp.zeros((b, dims["gqa_kv"], lk), BF16),
                   jnp.zeros((b, lk, dims["diff_v"]), BF16), jnp.zeros((b, lk, dims["gqa_kv"]), BF16))
        tok_c, kv_bufs = inproj(xc, mod_c, cos_t=rope_none[0], sin_t=rope_none[1], kv_bufs=kv_bufs, kv_only=last,
                                row_start=l, tm=lc)
        tok_x, kv_bufs = inproj(x, mod_x, cos_t=rope_x[0], sin_t=rope_x[1], kv_bufs=kv_bufs, kv_only=False,
                                row_start=0, tm=512)
        mix = functools.partial(_mix_and_mlp, lw=lw, kv_bufs=kv_bufs, dims=dims, lam_init=lam_init)
        x = mix(x, mod_x, tok=tok_x, fourier=fourier_x, key_start=0, key_len=lk, tm_mlp=512)
        if not last:
            xc = mix(xc, mod_c, tok=tok_c, fourier=fourier_c, key_start=l, key_len=lc, tm_mlp=lc)
    return x
```

```python
import functools
import math

import jax
import jax.numpy as jnp
import numpy as np
from jax import lax
from jax.experimental import pallas as pl
from jax.experimental.pallas import tpu as pltpu

F32 = jnp.float32
BF16 = jnp.bfloat16

GRID_W = 64
HEAD_DIM = 64
ROPE_FREQS = HEAD_DIM // 4
ROPE_THETA = 10000.0
EPS = 1e-6
ATTN_SCALE = HEAD_DIM ** -0.5
Q_PRESCALE = ATTN_SCALE * math.log2(math.e)
FOURIER_GROUP_DIM = 64
CONV_KERNEL = 31
CONV_HALO = 16
N_BRANCHES = 4
N_MOD = 6
MOD_ROWS = 8

V7X_VMEM_LIMIT_BYTES = 56 * 1024 * 1024
V7X_MXU_COLUMNS = 256
SUBLANES = 8
BF16_SUBLANES = 16
FOURIER_STAGE_LEN = 64


def _dims(d_model):
    fourier = 3 * d_model // 8
    diff_heads = d_model // 256
    diff_qk = diff_heads * 2 * HEAD_DIM
    diff_v = diff_heads * 2 * HEAD_DIM
    gqa_q = (d_model // 128) * HEAD_DIM
    gqa_kv = (d_model // 128 // 4) * HEAD_DIM
    conv = 3 * d_model // 8
    return dict(fourier=fourier, diff_heads=diff_heads, diff_qk=diff_qk, diff_v=diff_v,
                gqa_q=gqa_q, gqa_kv=gqa_kv, conv=conv)


def _params(semantics):
    return pltpu.CompilerParams(dimension_semantics=semantics, vmem_limit_bytes=V7X_VMEM_LIMIT_BYTES)


def _mod_kernel(c_ref, w_ref, b_ref, o_ref):
    c = c_ref[...]
    a = c * jax.nn.sigmoid(c)
    o_ref[...] = jnp.dot(a, w_ref[...], preferred_element_type=F32,
                         precision=lax.Precision.HIGHEST) + b_ref[...]


def _modulation(c_rows, w_mod, b_mod):
    depth, d, n = w_mod.shape
    tn = 2 * d
    return pl.pallas_call(
        _mod_kernel,
        grid=(depth, n // tn),
        in_specs=[pl.BlockSpec((MOD_ROWS, d), lambda l, j: (0, 0)),
                  pl.BlockSpec((None, d, tn), lambda l, j: (l, 0, j)),
                  pl.BlockSpec((None, 1, tn), lambda l, j: (l, 0, j))],
        out_specs=pl.BlockSpec((None, MOD_ROWS, tn), lambda l, j: (l, 0, j)),
        out_shape=jax.ShapeDtypeStruct((depth, MOD_ROWS, n), F32),
        compiler_params=_params(("arbitrary", "arbitrary")),
        name="modulation",
    )(c_rows, w_mod, b_mod.reshape(depth, 1, n))


def _modulated(x, g, shift, scale):
    h = x * lax.rsqrt(jnp.mean(x * x, axis=-1, keepdims=True) + EPS) * g
    return h * (1.0 + scale) + shift


def _rms(x, g):
    return x * lax.rsqrt(jnp.mean(x * x, axis=-1, keepdims=True) + EPS) * g


def _swap_halves(t):
    w = t.shape[-1]
    lane = lax.broadcasted_iota(jnp.int32, t.shape, 1)
    from_right = pltpu.roll(t, w - ROPE_FREQS, 1)
    from_left = pltpu.roll(t, ROPE_FREQS, 1)
    return jnp.where((lane % (2 * ROPE_FREQS)) < ROPE_FREQS, from_right, from_left)


def _tile_lanes(t, width):
    reps = width // t.shape[-1]
    return t if reps == 1 else jnp.concatenate([t] * reps, axis=-1)


def _head_norm(t, ones_bd, gain):
    w = t.shape[-1]
    t2 = t * t
    hi = t2.astype(BF16)
    lo = (t2 - hi.astype(F32)).astype(BF16)
    e = ones_bd[:w, :w]
    ss = (jnp.dot(hi, e, preferred_element_type=F32) + jnp.dot(lo, e, preferred_element_type=F32))
    return t * lax.rsqrt(ss * (1.0 / HEAD_DIM) + EPS) * gain


def _inproj_kernel(x_ref, shift_ref, scale_ref, g_ref, w_ref, ones_ref, cos_ref, sin_ref, qn_ref, kn_ref,
                   dft_ref, *rest, dims, kv_only, n_alias):
    out_refs = rest[n_alias:]
    nq, nkv, nv, nf, nc = dims["diff_qk"], dims["gqa_kv"], dims["diff_v"], dims["fourier"], dims["conv"]
    ngq = dims["gqa_q"]
    h = _modulated(x_ref[0], g_ref[...], shift_ref[0], scale_ref[0]).astype(BF16)

    def proj(start, width):
        return jnp.dot(h, w_ref[:, start:start + width].astype(BF16), preferred_element_type=F32)

    def rope(t):
        w = t.shape[-1]
        return t * _tile_lanes(cos_ref[...], w) + _swap_halves(t) * _tile_lanes(sin_ref[...], w)

    ones_bd = ones_ref[...]
    off = 0
    dk = proj(off, nq); off += nq
    dv = proj(off, nv); off += nv
    gk = proj(off, nkv); off += nkv
    gv = proj(off, nkv); off += nkv
    kdt_ref, kgt_ref, vd_ref, vg_ref = out_refs[-4:]
    kdt_ref[0] = rope(dk).T.astype(BF16)
    kgt_ref[0] = rope(_head_norm(gk, ones_bd, kn_ref[...])).T.astype(BF16)
    vd_ref[0] = dv.astype(BF16)
    vg_ref[0] = gv.astype(BF16)
    if not kv_only:
        qd_ref, qg_ref, xf_ref, vc_ref = out_refs[:4]
        uf = proj(off, nf).astype(BF16); off += nf
        xf_ref[0] = jnp.dot(uf, dft_ref[...], preferred_element_type=F32).astype(BF16)
        dq = proj(off, nq); off += nq
        qd_ref[0] = (rope(dq) * Q_PRESCALE).astype(BF16)
        gq = proj(off, ngq); off += ngq
        qg_ref[0] = (rope(_head_norm(gq, ones_bd, qn_ref[...])) * Q_PRESCALE).astype(BF16)
        glu = proj(off, 2 * nc)
        vc_ref[0] = glu[:, :nc] * jax.nn.sigmoid(glu[:, nc:])


def _inproj(x, mod, g_pre, w_in, ones_bd, cos_t, sin_t, qn, kn, dft_ch, kv_bufs, *, dims, layer, n_tok_cols,
            kv_only, row_start, tm):
    b, l, d = x.shape
    tm = min(tm, l)
    nq, nkv, nv, nf, nc, ngq = (dims["diff_qk"], dims["gqa_kv"], dims["diff_v"], dims["fourier"],
                                dims["conv"], dims["gqa_q"])
    blk0 = row_start // tm
    assert row_start % tm == 0
    tok = lambda w: pl.BlockSpec((1, tm, w), lambda bi, i: (bi, i, 0))
    const = lambda shape: pl.BlockSpec(shape, lambda bi, i: (0,) * len(shape))
    tok_outs = [] if kv_only else [((b, l, nq), BF16), ((b, l, ngq), BF16), ((b, l, 2 * nf), BF16), ((b, l, nc), F32)]
    kv_shapes = [(a.shape, a.dtype) for a in kv_bufs]
    kv_specs = [pl.BlockSpec((1, nq, tm), lambda bi, i: (bi, 0, blk0 + i)),
                pl.BlockSpec((1, nkv, tm), lambda bi, i: (bi, 0, blk0 + i)),
                pl.BlockSpec((1, tm, nv), lambda bi, i: (bi, blk0 + i, 0)),
                pl.BlockSpec((1, tm, nkv), lambda bi, i: (bi, blk0 + i, 0))]
    n_alias = len(kv_bufs)
    n_in = 11
    aliases = {n_in + k: len(tok_outs) + k for k in range(n_alias)}
    outs = pl.pallas_call(
        functools.partial(_inproj_kernel, dims=dims, kv_only=kv_only, n_alias=n_alias),
        grid=(b, l // tm),
        in_specs=[tok(d),
                  pl.BlockSpec((1, 1, d), lambda bi, i: (bi, 0, 0)),
                  pl.BlockSpec((1, 1, d), lambda bi, i: (bi, 0, 1)),
                  const((1, d)),
                  pl.BlockSpec((None, d, n_tok_cols), lambda bi, i: (layer, 0, 0)),
                  const(ones_bd.shape),
                  pl.BlockSpec((tm, 2 * HEAD_DIM), lambda bi, i: (i, 0)),
                  pl.BlockSpec((tm, 2 * HEAD_DIM), lambda bi, i: (i, 0)),
                  const(qn.shape), const(kn.shape), const(dft_ch.shape)]
                 + [pl.BlockSpec(memory_space=pl.ANY)] * n_alias,
        out_specs=[tok(s[-1]) for s, _ in tok_outs] + kv_specs,
        out_shape=[jax.ShapeDtypeStruct(s, dt) for s, dt in tok_outs + kv_shapes],
        input_output_aliases=aliases,
        compiler_params=_params(("arbitrary", "arbitrary")),
        name="inproj_kv" if kv_only else "inproj",
    )(x, mod, mod, g_pre, w_in, ones_bd, cos_t, sin_t, qn, kn, dft_ch, *kv_bufs)
    return outs[:len(tok_outs)], outs[len(tok_outs):]


def _conv_ln_silu(win_ref, w_ref, b_ref, g_ref, beta_ref, yc_ref, *, rows, chunk):
    first = CONV_HALO - CONV_KERNEL // 2
    for r0 in range(0, rows, chunk):
        acc = jnp.zeros((chunk, w_ref.shape[-1]), F32) + b_ref[...]
        for phase in range(SUBLANES):
            part = None
            for t in range(CONV_KERNEL):
                if (t + first) % SUBLANES != phase:
                    continue
                base = r0 + t + first - phase
                term = w_ref[t:t + 1, :] * win_ref[base:base + chunk + SUBLANES, :]
                part = term if part is None else part + term
            acc = acc + part[phase:phase + chunk, :]
        mu = jnp.mean(acc, axis=-1, keepdims=True)
        cen = acc - mu
        var = jnp.mean(cen * cen, axis=-1, keepdims=True)
        y = cen * lax.rsqrt(var + EPS) * g_ref[...] + beta_ref[...]
        yc_ref[r0:r0 + chunk, :] = (y * jax.nn.sigmoid(y)).astype(BF16)


def _softmax_pv(q, k_t, v_ext):
    s = jnp.dot(q, k_t, preferred_element_type=F32)
    e = jnp.exp2(s - jnp.max(s, axis=-1, keepdims=True))
    return jnp.dot(e.astype(BF16), v_ext, preferred_element_type=F32)


def _conv_kernel(vprev_ref, vcur_ref, vnext_ref, cw_ref, cb_ref, cg_ref, cbeta_ref, yc_ref, win_ref):
    i = pl.program_id(1)
    rows = vcur_ref.shape[1]
    win_ref[:CONV_HALO, :] = jnp.where(i > 0, vprev_ref[0], 0.0)
    win_ref[CONV_HALO:CONV_HALO + rows, :] = vcur_ref[0]
    win_ref[CONV_HALO + rows:, :] = jnp.where(i < pl.num_programs(1) - 1, vnext_ref[0], 0.0)
    _conv_ln_silu(win_ref, cw_ref, cb_ref, cg_ref, cbeta_ref, yc_ref.at[0], rows=rows, chunk=32)


def _conv(vc, conv_w, conv_b, conv_g, conv_beta, *, rows):
    b, l, c = vc.shape
    rows = min(rows, l)
    hb = rows // CONV_HALO
    n_hb = l // CONV_HALO
    const = lambda a: pl.BlockSpec(a.shape, lambda bi, i: (0,) * a.ndim)
    prev = pl.BlockSpec((1, CONV_HALO, c), lambda bi, i: (bi, jnp.maximum(i * hb - 1, 0), 0))
    nxt = pl.BlockSpec((1, CONV_HALO, c), lambda bi, i: (bi, jnp.minimum((i + 1) * hb, n_hb - 1), 0))
    tile = pl.BlockSpec((1, rows, c), lambda bi, i: (bi, i, 0))
    return pl.pallas_call(
        _conv_kernel,
        grid=(b, l // rows),
        in_specs=[prev, tile, nxt, const(conv_w), const(conv_b), const(conv_g), const(conv_beta)],
        out_specs=tile,
        out_shape=jax.ShapeDtypeStruct((b, l, c), BF16),
        scratch_shapes=[pltpu.VMEM((rows + 2 * CONV_HALO, c), F32)],
        compiler_params=_params(("arbitrary", "arbitrary")),
        name="conv",
    )(vc, vc, vc, conv_w, conv_b, conv_g, conv_beta)


def _attn_kernel(qd_ref, qg_ref, kdt_ref, vd_ref, kgt_ref, vg_ref, lam_ref, subln_ref, od_ref, og_ref, odl_ref,
                 vdx_ref, vgx_ref, s_ref, m0_ref, *, n_diff, n_gq, n_kv, lam_init):
    dh = HEAD_DIM
    dv = 2 * HEAD_DIM
    i = pl.program_id(1)
    tq = qd_ref.shape[1]
    last = n_diff - 1

    @pl.when(i == 0)
    def _():
        for hd in range(n_diff):
            vdx_ref[hd, :, :dv] = vd_ref[0, :, hd * dv:(hd + 1) * dv]
            vdx_ref[hd, :, dv:] = jnp.ones((vd_ref.shape[1], V7X_MXU_COLUMNS - dv), BF16)
        vgx_ref[:, :n_kv * dh] = vg_ref[0]
        vgx_ref[:, n_kv * dh:] = jnp.ones((vg_ref.shape[1], V7X_MXU_COLUMNS - n_kv * dh), BF16)
        s_ref[...] = jnp.zeros(s_ref.shape, F32)
        m0_ref[...] = jnp.zeros(m0_ref.shape, F32)

    lv = lam_ref[...]
    lam = (jnp.exp(jnp.sum(lv[0:1] * lv[1:2], axis=-1, keepdims=True))
           - jnp.exp(jnp.sum(lv[2:3] * lv[3:4], axis=-1, keepdims=True)) + lam_init)

    def diff_head_out(m0, pv1):
        o = m0 - lam * (pv1[:, :dv] / pv1[:, dv:])
        return (_rms(o, subln_ref[...]) * (1.0 - lam_init)).astype(BF16)

    def finish_last_diff_head(row0):
        s = s_ref[...]
        e = jnp.exp2(s - jnp.max(s, axis=-1, keepdims=True))
        pv1 = jnp.dot(e.astype(BF16), vdx_ref[last], preferred_element_type=F32)
        odl_ref[0, pl.ds(row0, tq), :] = diff_head_out(m0_ref[...], pv1)

    finish_last_diff_head(pl.multiple_of(jnp.maximum(i - 1, 0) * tq, tq))
    for hd in range(last):
        c = 2 * hd * dh
        pv0 = _softmax_pv(qd_ref[0, :, c:c + dh], kdt_ref[0, c:c + dh, :], vdx_ref[hd])
        pv1 = _softmax_pv(qd_ref[0, :, c + dh:c + 2 * dh], kdt_ref[0, c + dh:c + 2 * dh, :], vdx_ref[hd])
        od_ref[0, :, hd * dv:(hd + 1) * dv] = diff_head_out(pv0[:, :dv] / pv0[:, dv:], pv1)
    group = n_gq // n_kv
    lane = lax.broadcasted_iota(jnp.int32, (qg_ref.shape[1], dv), 1)
    for pair in range(n_gq // 2):
        kv = (2 * pair) // group
        halves = []
        for hq in (2 * pair, 2 * pair + 1):
            pv = _softmax_pv(qg_ref[0, :, hq * dh:(hq + 1) * dh], kgt_ref[0, kv * dh:(kv + 1) * dh, :], vgx_ref[...])
            halves.append(pv[:, :dv] / pv[:, dv:])
        left = halves[0] if kv % 2 == 0 else pltpu.roll(halves[0], dh, 1)
        right = pltpu.roll(halves[1], dh, 1) if kv % 2 == 0 else halves[1]
        og_ref[0, :, pair * dv:(pair + 1) * dv] = jnp.where(lane < dh, left, right).astype(BF16)

    c = 2 * last * dh
    pv0 = _softmax_pv(qd_ref[0, :, c:c + dh], kdt_ref[0, c:c + dh, :], vdx_ref[last])
    m0_ref[...] = pv0[:, :dv] / pv0[:, dv:]
    s_ref[...] = jnp.dot(qd_ref[0, :, c + dh:c + 2 * dh], kdt_ref[0, c + dh:c + 2 * dh, :],
                         preferred_element_type=F32)

    @pl.when(i == pl.num_programs(1) - 1)
    def _():
        finish_last_diff_head(pl.multiple_of(i * tq, tq))


def _attention(qd, qg, kv_bufs, lam_p, subln, *, dims, lam_init, tq, key_start, key_len):
    kd_t, kg_t, vd, vg = kv_bufs
    b, l, nq = qd.shape
    n_diff, n_gq, n_kv = dims["diff_heads"], dims["gqa_q"] // HEAD_DIM, dims["gqa_kv"] // HEAD_DIM
    assert n_kv * HEAD_DIM == 2 * HEAD_DIM and (n_gq // n_kv) % 2 == 0 and key_start % key_len == 0
    kb = key_start // key_len
    tq = min(tq, l)
    tok = lambda w: pl.BlockSpec((1, tq, w), lambda bi, i: (bi, i, 0))
    keys_t = lambda a: pl.BlockSpec((1, a.shape[1], key_len), lambda bi, i: (bi, 0, kb))
    vals = lambda a: pl.BlockSpec((1, key_len, a.shape[2]), lambda bi, i: (bi, kb, 0))
    const = lambda a: pl.BlockSpec(a.shape, lambda bi, i: (0,) * a.ndim)
    dv = 2 * HEAD_DIM
    od, og, od_last = pl.pallas_call(
        functools.partial(_attn_kernel, n_diff=n_diff, n_gq=n_gq, n_kv=n_kv, lam_init=lam_init),
        grid=(b, l // tq),
        in_specs=[tok(nq), tok(qg.shape[-1]), keys_t(kd_t), vals(vd), keys_t(kg_t), vals(vg),
                  const(lam_p), const(subln)],
        out_specs=[tok((n_diff - 1) * dv), tok(dims["gqa_q"]), pl.BlockSpec((1, l, dv), lambda bi, i: (bi, 0, 0))],
        out_shape=[jax.ShapeDtypeStruct((b, l, (n_diff - 1) * dv), BF16),
                   jax.ShapeDtypeStruct((b, l, dims["gqa_q"]), BF16),
                   jax.ShapeDtypeStruct((b, l, dv), BF16)],
        scratch_shapes=[pltpu.VMEM((n_diff, key_len, V7X_MXU_COLUMNS), BF16),
                        pltpu.VMEM((key_len, V7X_MXU_COLUMNS), BF16),
                        pltpu.VMEM((tq, key_len), F32),
                        pltpu.VMEM((tq, dv), F32)],
        compiler_params=_params(("arbitrary", "arbitrary")),
        name="attention",
    )(qd, qg, kd_t, vd, kg_t, vg, lam_p, subln)
    return od, od_last, og


def _fourier_kernel(c_ref, s_ref, xf_ref, o_ref, *, nf, out_scale):
    y = (jnp.dot(c_ref[...], xf_ref[0, :, :nf], preferred_element_type=F32)
         - jnp.dot(s_ref[...], xf_ref[0, :, nf:], preferred_element_type=F32))
    o_ref[0] = (y * out_scale).astype(BF16)


def _fourier_dense(tables, xf, *, tr=512):
    dft_c, dft_s = tables
    b, l, nf2 = xf.shape
    nf = nf2 // 2
    tr = min(tr, l)
    return pl.pallas_call(
        functools.partial(_fourier_kernel, nf=nf, out_scale=float(l) ** -0.5),
        grid=(b, l // tr),
        in_specs=[pl.BlockSpec((tr, l), lambda bi, i: (i, 0)),
                  pl.BlockSpec((tr, l), lambda bi, i: (i, 0)),
                  pl.BlockSpec((1, l, nf2), lambda bi, i: (bi, 0, 0))],
        out_specs=pl.BlockSpec((1, tr, nf), lambda bi, i: (bi, i, 0)),
        out_shape=jax.ShapeDtypeStruct((b, l, nf), BF16),
        compiler_params=_params(("arbitrary", "arbitrary")),
        name="fourier",
    )(dft_c, dft_s, xf)


def _fft_a_kernel(z_ref, m_ref, cos_ref, sin_ref, o_ref, *, nf):
    n1, r = z_ref.shape[1], z_ref.shape[2]
    rows = n1 * r
    z = z_ref[0].reshape(rows, 2 * nf)
    t = jnp.dot(m_ref[...], z, preferred_element_type=F32)
    cz, sz = t[:rows], t[rows:]
    ar = cz[:, :nf] - sz[:, nf:]
    ai = -(sz[:, :nf] + cz[:, nf:])
    ct = _tile_lanes(cos_ref[0], nf)
    st = _tile_lanes(sin_ref[0], nf)
    o_ref[0, :, :, :nf] = (ar * ct + ai * st).reshape(n1, r, nf).astype(BF16)
    o_ref[0, :, :, nf:] = (ai * ct - ar * st).reshape(n1, r, nf).astype(BF16)


def _fft_c_kernel(b_ref, mc_ref, ms_ref, o_ref, *, nf, out_scale):
    r, n2 = b_ref.shape[1], b_ref.shape[2]
    blk = b_ref[0].reshape(r * n2, 2 * nf)
    y = (jnp.dot(mc_ref[...], blk[:, :nf], preferred_element_type=F32)
         + jnp.dot(ms_ref[...], blk[:, nf:], preferred_element_type=F32))
    o_ref[0] = (y * out_scale).reshape(n2, r, nf).astype(BF16)


def _fourier_two_stage(tables, xf):
    m_a, tw_cos, tw_sin, m_c_cos, m_c_sin = tables
    b, l, nf2 = xf.shape
    nf = nf2 // 2
    r = BF16_SUBLANES
    n2 = m_c_cos.shape[0] // r
    n1 = l // n2
    tile = lambda w: pl.BlockSpec((1, n1, r, w), lambda bi, s: (bi, 0, s, 0))
    const = lambda a: pl.BlockSpec(a.shape, lambda bi, s: (0,) * a.ndim)
    twiddle = pl.BlockSpec((1, n1 * r, 2 * HEAD_DIM), lambda bi, s: (s, 0, 0))
    staged = pl.pallas_call(
        functools.partial(_fft_a_kernel, nf=nf),
        grid=(b, n2 // r),
        in_specs=[tile(nf2), const(m_a), twiddle, twiddle],
        out_specs=tile(nf2),
        out_shape=jax.ShapeDtypeStruct((b, n1, n2, nf2), BF16),
        compiler_params=_params(("arbitrary", "arbitrary")),
        name="fourier_a",
    )(xf.reshape(b, n1, n2, nf2), m_a, tw_cos, tw_sin)
    out = pl.pallas_call(
        functools.partial(_fft_c_kernel, nf=nf, out_scale=float(l) ** -0.5),
        grid=(b, n1 // r),
        in_specs=[pl.BlockSpec((1, r, n2, nf2), lambda bi, s: (bi, s, 0, 0)), const(m_c_cos), const(m_c_sin)],
        out_specs=pl.BlockSpec((1, n2, r, nf), lambda bi, s: (bi, 0, s, 0)),
        out_shape=jax.ShapeDtypeStruct((b, n2, n1, nf), BF16),
        compiler_params=_params(("arbitrary", "arbitrary")),
        name="fourier_c",
    )(staged, m_c_cos, m_c_sin)
    return out.reshape(b, l, nf)


def _merge_kernel(x_ref, shift_ref, scale_ref, gate_ref, g_ref, wgate_ref, yf_ref, od_ref, odl_ref, og_ref, yc_ref,
                  wf_ref, wd_ref, wg_ref, wc_ref, wout_ref, gpost_ref, o_ref):
    x = x_ref[0]
    d = x.shape[-1]
    h = _modulated(x, g_ref[...], shift_ref[0], scale_ref[0]).astype(BF16)
    diff = jnp.concatenate([od_ref[0], odl_ref[0]], axis=-1)
    branches = ((yf_ref[0], wf_ref), (diff, wd_ref), (og_ref[0], wg_ref), (yc_ref[0], wc_ref))
    merged = None
    for k, (y, w_ref) in enumerate(branches):
        gate = jax.nn.sigmoid(jnp.dot(h, wgate_ref[:, k * d:(k + 1) * d], preferred_element_type=F32))
        term = gate * jnp.dot(y, w_ref[...], preferred_element_type=F32)
        merged = term if merged is None else merged + term
    mix = jnp.dot(merged.astype(BF16), wout_ref[...], preferred_element_type=F32)
    o_ref[0] = x + gate_ref[0] * _rms(mix, gpost_ref[...])


def _merge(x, mod, g_pre, w_gate, yf, od, od_last, og, yc, wf, wd, wg, wc, wout, g_post, *, tm):
    b, l, d = x.shape
    tm = min(tm, l)
    tok = lambda w: pl.BlockSpec((1, tm, w), lambda bi, i: (bi, i, 0))
    modc = lambda k: pl.BlockSpec((1, 1, d), lambda bi, i: (bi, 0, k))
    const = lambda a: pl.BlockSpec(a.shape, lambda bi, i: (0,) * a.ndim)
    return pl.pallas_call(
        _merge_kernel,
        grid=(b, l // tm),
        in_specs=[tok(d), modc(0), modc(1), modc(2), const(g_pre), const(w_gate),
                  tok(yf.shape[-1]), tok(od.shape[-1]), tok(od_last.shape[-1]), tok(og.shape[-1]), tok(yc.shape[-1]),
                  const(wf), const(wd), const(wg), const(wc), const(wout), const(g_post)],
        out_specs=tok(d),
        out_shape=jax.ShapeDtypeStruct((b, l, d), F32),
        compiler_params=_params(("arbitrary", "arbitrary")),
        name="merge",
    )(x, mod, mod, mod, g_pre, w_gate, yf, od, od_last, og, yc, wf, wd, wg, wc, wout, g_post)


def _mlp_kernel(x_ref, shift_ref, scale_ref, gate_ref, g_ref, w1_ref, w2_ref, gpost_ref, o_ref, *, ff_chunk):
    x = x_ref[0]
    h = _modulated(x, g_ref[...], shift_ref[0], scale_ref[0]).astype(BF16)
    acc = None
    for c0 in range(0, w1_ref.shape[-1], ff_chunk):
        u = jnp.maximum(jnp.dot(h, w1_ref[:, c0:c0 + ff_chunk], preferred_element_type=F32), 0.0)
        part = jnp.dot((u * u).astype(BF16), w2_ref[c0:c0 + ff_chunk, :], preferred_element_type=F32)
        acc = part if acc is None else acc + part
    o_ref[0] = x + gate_ref[0] * _rms(acc, gpost_ref[...])


def _mlp(x, mod, g_pre, w1, w2, g_post, *, tm):
    b, l, d = x.shape
    tm = min(tm, l)
    tok = lambda w: pl.BlockSpec((1, tm, w), lambda bi, i: (bi, i, 0))
    modc = lambda k: pl.BlockSpec((1, 1, d), lambda bi, i: (bi, 0, k))
    const = lambda a: pl.BlockSpec(a.shape, lambda bi, i: (0,) * a.ndim)
    return pl.pallas_call(
        functools.partial(_mlp_kernel, ff_chunk=1024),
        grid=(b, l // tm),
        in_specs=[tok(d), modc(3), modc(4), modc(5), const(g_pre), const(w1), const(w2), const(g_post)],
        out_specs=tok(d),
        out_shape=jax.ShapeDtypeStruct((b, l, d), F32),
        compiler_params=_params(("arbitrary", "arbitrary")),
        name="mlp",
    )(x, mod, mod, mod, g_pre, w1, w2, g_post)


def _angles(i, j, period):
    return (np.outer(i, j) % period).astype(np.float64) * (2.0 * np.pi / period)


def _mxu_table(values):
    return jnp.asarray(values, F32).astype(BF16)


def _rope_tables(seq_len):
    rows = seq_len // GRID_W
    row = np.repeat(np.arange(rows, dtype=np.float64), GRID_W)
    col = np.tile(np.arange(GRID_W, dtype=np.float64), rows)
    inv_freq = ROPE_THETA ** (-np.arange(ROPE_FREQS, dtype=np.float64) / ROPE_FREQS)
    ang_r, ang_c = row[:, None] * inv_freq, col[:, None] * inv_freq
    cos64 = np.concatenate([np.cos(ang_r)] * 2 + [np.cos(ang_c)] * 2, axis=-1)
    sin64 = np.concatenate([-np.sin(ang_r), np.sin(ang_r), -np.sin(ang_c), np.sin(ang_c)], axis=-1)
    return (jnp.asarray(np.concatenate([cos64] * 2, axis=-1), F32),
            jnp.asarray(np.concatenate([sin64] * 2, axis=-1), F32))


def _seq_dft(l):
    th = _angles(np.arange(l), np.arange(l), l)
    return _mxu_table(np.cos(th)), _mxu_table(np.sin(th))


def _two_stage_dft(l, n2):
    n1 = l // n2
    r = BF16_SUBLANES
    i1, i2 = np.arange(n1), np.arange(n2)
    eye = np.eye(r)
    th_a = _angles(i1, i1, n1)
    m_a = np.concatenate([np.kron(np.cos(th_a), eye), np.kron(np.sin(th_a), eye)], axis=0)
    th_t = _angles(i1, i2, l).reshape(n1, n2 // r, r)
    th_t = np.transpose(th_t, (1, 0, 2)).reshape(n2 // r, n1 * r)
    lanes = lambda t: jnp.asarray(np.broadcast_to(t[:, :, None], t.shape + (2 * HEAD_DIM,)), F32)
    th_c = _angles(i2, i2, n2)
    perm = lambda w: _mxu_table(np.einsum('kb,ij->kijb', w, eye).reshape(n2 * r, r * n2))
    return _mxu_table(m_a), lanes(np.cos(th_t)), lanes(np.sin(th_t)), perm(np.cos(th_c)), perm(np.sin(th_c))


def _channel_dft(n_groups):
    g = FOURIER_GROUP_DIM
    th = _angles(np.arange(g), np.arange(g), g)
    eye = np.eye(n_groups)
    scale = float(g) ** -0.5
    return _mxu_table(np.concatenate([np.kron(eye, np.cos(th) * scale), np.kron(eye, np.sin(th) * scale)], axis=-1))


def _seq_fourier(l):
    n2 = FOURIER_STAGE_LEN
    if l % n2 == 0 and (l // n2) % BF16_SUBLANES == 0:
        return functools.partial(_fourier_two_stage, _two_stage_dft(l, n2))
    return functools.partial(_fourier_dense, _seq_dft(l))


def _mix_and_mlp(x, mod, lw, tok, kv_bufs, fourier, *, dims, lam_init, key_start, key_len, tm_mlp):
    qd, qg, xf, vc = tok
    od, od_last, og = _attention(qd, qg, kv_bufs, lw["lam_p"], lw["subln"], dims=dims, lam_init=lam_init, tq=256,
                                 key_start=key_start, key_len=key_len)
    yf = fourier(xf)
    yc = _conv(vc, lw["conv_w"], lw["conv_b"], lw["conv_ln_g"], lw["conv_ln_b"], rows=512)
    x = _merge(x, mod, lw["g_pre_mix"], lw["w_gate"], yf, od, od_last, og, yc, lw["w_br_f"], lw["w_br_d"],
               lw["w_br_g"], lw["w_br_c"], lw["w_out"], lw["g_post_mix"], tm=512)
    return _mlp(x, mod, lw["g_pre_mlp"], lw["w_ff1"], lw["w_ff2"], lw["g_post_mlp"], tm=tm_mlp)


def kernel(x, c, ctx, c_ctx, w_mod, b_mod, g_pre_mix, g_post_mix, g_pre_mlp, g_post_mlp, w_in, q_norm, k_norm,
           diff_lambda, diff_subln, conv_dw, conv_dw_bias, conv_ln_g, conv_ln_b, w_br_fourier, w_br_diff,
           w_br_gqa, w_br_conv, w_out, w_ff1, w_ff2):
    b, l, d = x.shape
    lc = ctx.shape[1]
    lk = l + lc
    depth = w_in.shape[0]
    dims = _dims(d)
    assert b + 1 <= MOD_ROWS and l % GRID_W == 0 and l % lc == 0

    rope_x = _rope_tables(l)
    rope_none = (jnp.ones((lc, 2 * HEAD_DIM), F32), jnp.zeros((lc, 2 * HEAD_DIM), F32))
    fourier_x, fourier_c = _seq_fourier(l), _seq_fourier(lc)
    n_groups = dims["fourier"] // FOURIER_GROUP_DIM
    heads = np.arange(dims["diff_qk"]) // HEAD_DIM
    ones_bd = jnp.asarray((heads[:, None] == heads[None, :]).astype(BF16))
    dft_ch = _channel_dft(n_groups)

    c_rows = jnp.zeros((MOD_ROWS, d), F32).at[:b].set(c).at[b].set(c_ctx)
    mod_all = _modulation(c_rows, w_mod, b_mod)

    n_tok_cols = w_in.shape[-1] - N_BRANCHES * d

    xc = ctx
    for li in range(depth):
        lam_init = 0.8 - 0.6 * math.exp(-0.3 * li)
        row = lambda a: a[li].reshape(1, -1)
        lw = dict(
            w_gate=lax.optimization_barrier(w_in[li, :, n_tok_cols:]).astype(BF16),
            g_pre_mix=row(g_pre_mix), g_post_mix=row(g_post_mix), g_pre_mlp=row(g_pre_mlp),
            g_post_mlp=row(g_post_mlp),
            qn=jnp.tile(q_norm[li], dims["gqa_q"] // HEAD_DIM).reshape(1, -1),
            kn=jnp.tile(k_norm[li], dims["gqa_kv"] // HEAD_DIM).reshape(1, -1),
            lam_p=diff_lambda[li], subln=row(diff_subln),
            conv_w=conv_dw[li], conv_b=row(conv_dw_bias), conv_ln_g=row(conv_ln_g), conv_ln_b=row(conv_ln_b),
            w_br_f=w_br_fourier[li].astype(BF16), w_br_d=w_br_diff[li].astype(BF16),
            w_br_g=w_br_gqa[li].astype(BF16), w_br_c=w_br_conv[li].astype(BF16), w_out=w_out[li].astype(BF16),
            w_ff1=w_ff1[li].astype(BF16), w_ff2=w_ff2[li].astype(BF16))
        mod_x = mod_all[li, :b].reshape(b, 1, N_MOD * d)
        mod_c = jnp.broadcast_to(mod_all[li, b].reshape(1, 1, N_MOD * d), (b, 1, N_MOD * d))
        last = li == depth - 1

        inproj = functools.partial(_inproj, g_pre=lw["g_pre_mix"], w_in=w_in, ones_bd=ones_bd, qn=lw["qn"],
                                   kn=lw["kn"], dft_ch=dft_ch, dims=dims, layer=li, n_tok_cols=n_tok_cols)
        kv_bufs = (jnp.zeros((b, dims["diff_qk"], lk), BF16), jnp.zeros((b, dims["gqa_kv"], lk), BF16),
                   jnp.zeros((b, lk, dims["diff_v"]), BF16), jnp.zeros((b, lk, dims["gqa_kv"]), BF16))
        tok_c, kv_bufs = inproj(xc, mod_c, cos_t=rope_none[0], sin_t=rope_none[1], kv_bufs=kv_bufs, kv_only=last,
                                row_start=l, tm=lc)
        tok_x, kv_bufs = inproj(x, mod_x, cos_t=rope_x[0], sin_t=rope_x[1], kv_bufs=kv_bufs, kv_only=False,
                                row_start=0, tm=512)
        mix = functools.partial(_mix_and_mlp, lw=lw, kv_bufs=kv_bufs, dims=dims, lam_init=lam_init)
        x = mix(x, mod_x, tok=tok_x, fourier=fourier_x, key_start=0, key_len=lk, tm_mlp=512)
        if not last:
            xc = mix(xc, mod_c, tok=tok_c, fourier=fourier_c, key_start=l, key_len=lc, tm_mlp=lc)
    return x
```

```python
import functools
import math

import jax
import jax.numpy as jnp
import numpy as np
from jax import lax
from jax.experimental import pallas as pl
from jax.experimental.pallas import tpu as pltpu

F32 = jnp.float32
BF16 = jnp.bfloat16

GRID_W = 64
HEAD_DIM = 64
ROPE_FREQS = HEAD_DIM // 4
ROPE_THETA = 10000.0
EPS = 1e-6
ATTN_SCALE = HEAD_DIM ** -0.5
Q_PRESCALE = ATTN_SCALE * math.log2(math.e)
FOURIER_GROUP_DIM = 64
CONV_KERNEL = 31
CONV_HALO = 16
N_BRANCHES = 4
N_MOD = 6
MOD_ROWS = 8

V7X_VMEM_LIMIT_BYTES = 56 * 1024 * 1024
V7X_MXU_COLUMNS = 256
SUBLANES = 8
BF16_SUBLANES = 16
FOURIER_STAGE_LEN = 64


def _dims(d_model):
    fourier = 3 * d_model // 8
    diff_heads = d_model // 256
    diff_qk = diff_heads * 2 * HEAD_DIM
    diff_v = diff_heads * 2 * HEAD_DIM
    gqa_q = (d_model // 128) * HEAD_DIM
    gqa_kv = (d_model // 128 // 4) * HEAD_DIM
    conv = 3 * d_model // 8
    return dict(fourier=fourier, diff_heads=diff_heads, diff_qk=diff_qk, diff_v=diff_v,
                gqa_q=gqa_q, gqa_kv=gqa_kv, conv=conv)


def _params(semantics):
    return pltpu.CompilerParams(dimension_semantics=semantics, vmem_limit_bytes=V7X_VMEM_LIMIT_BYTES)


def _mod_kernel(c_ref, w_ref, b_ref, o_ref):
    c = c_ref[...]
    a = c * jax.nn.sigmoid(c)
    o_ref[...] = jnp.dot(a, w_ref[...], preferred_element_type=F32,
                         precision=lax.Precision.HIGHEST) + b_ref[...]


def _modulation(c_rows, w_mod, b_mod):
    depth, d, n = w_mod.shape
    tn = 2 * d
    return pl.pallas_call(
        _mod_kernel,
        grid=(depth, n // tn),
        in_specs=[pl.BlockSpec((MOD_ROWS, d), lambda l, j: (0, 0)),
                  pl.BlockSpec((None, d, tn), lambda l, j: (l, 0, j)),
                  pl.BlockSpec((None, 1, tn), lambda l, j: (l, 0, j))],
        out_specs=pl.BlockSpec((None, MOD_ROWS, tn), lambda l, j: (l, 0, j)),
        out_shape=jax.ShapeDtypeStruct((depth, MOD_ROWS, n), F32),
        compiler_params=_params(("arbitrary", "arbitrary")),
        name="modulation",
    )(c_rows, w_mod, b_mod.reshape(depth, 1, n))


def _modulated(x, g, shift, scale):
    h = x * lax.rsqrt(jnp.mean(x * x, axis=-1, keepdims=True) + EPS) * g
    return h * (1.0 + scale) + shift


def _rms(x, g):
    return x * lax.rsqrt(jnp.mean(x * x, axis=-1, keepdims=True) + EPS) * g


def _swap_halves(t):
    w = t.shape[-1]
    lane = lax.broadcasted_iota(jnp.int32, t.shape, 1)
    from_right = pltpu.roll(t, w - ROPE_FREQS, 1)
    from_left = pltpu.roll(t, ROPE_FREQS, 1)
    return jnp.where((lane % (2 * ROPE_FREQS)) < ROPE_FREQS, from_right, from_left)


def _tile_lanes(t, width):
    reps = width // t.shape[-1]
    return t if reps == 1 else jnp.concatenate([t] * reps, axis=-1)


def _head_norm(t, ones_bd, gain):
    w = t.shape[-1]
    t2 = t * t
    hi = t2.astype(BF16)
    lo = (t2 - hi.astype(F32)).astype(BF16)
    e = ones_bd[:w, :w]
    ss = (jnp.dot(hi, e, preferred_element_type=F32) + jnp.dot(lo, e, preferred_element_type=F32))
    return t * lax.rsqrt(ss * (1.0 / HEAD_DIM) + EPS) * gain


def _inproj_kernel(x_ref, shift_ref, scale_ref, g_ref, w_ref, ones_ref, cos_ref, sin_ref, qn_ref, kn_ref,
                   dft_ref, *rest, dims, kv_only, n_alias):
    out_refs = rest[n_alias:]
    nq, nkv, nv, nf, nc = dims["diff_qk"], dims["gqa_kv"], dims["diff_v"], dims["fourier"], dims["conv"]
    ngq = dims["gqa_q"]
    h = _modulated(x_ref[0], g_ref[...], shift_ref[0], scale_ref[0]).astype(BF16)

    def proj(start, width):
        return jnp.dot(h, w_ref[:, start:start + width].astype(BF16), preferred_element_type=F32)

    def rope(t):
        w = t.shape[-1]
        return t * _tile_lanes(cos_ref[...], w) + _swap_halves(t) * _tile_lanes(sin_ref[...], w)

    ones_bd = ones_ref[...]
    off = 0
    dk = proj(off, nq); off += nq
    dv = proj(off, nv); off += nv
    gk = proj(off, nkv); off += nkv
    gv = proj(off, nkv); off += nkv
    kdt_ref, kgt_ref, vd_ref, vg_ref = out_refs[-4:]
    kdt_ref[0] = rope(dk).T.astype(BF16)
    kgt_ref[0] = rope(_head_norm(gk, ones_bd, kn_ref[...])).T.astype(BF16)
    vd_ref[0] = dv.astype(BF16)
    vg_ref[0] = gv.astype(BF16)
    if not kv_only:
        qd_ref, qg_ref, xf_ref, vc_ref = out_refs[:4]
        uf = proj(off, nf).astype(BF16); off += nf
        xf_ref[0] = jnp.dot(uf, dft_ref[...], preferred_element_type=F32).astype(BF16)
        dq = proj(off, nq); off += nq
        qd_ref[0] = (rope(dq) * Q_PRESCALE).astype(BF16)
        gq = proj(off, ngq); off += ngq
        qg_ref[0] = (rope(_head_norm(gq, ones_bd, qn_ref[...])) * Q_PRESCALE).astype(BF16)
        glu = proj(off, 2 * nc)
        vc_ref[0] = glu[:, :nc] * jax.nn.sigmoid(glu[:, nc:])


def _inproj(x, mod, g_pre, w_in, ones_bd, cos_t, sin_t, qn, kn, dft_ch, kv_bufs, *, dims, layer, n_tok_cols,
            kv_only, row_start, tm):
    b, l, d = x.shape
    tm = min(tm, l)
    nq, nkv, nv, nf, nc, ngq = (dims["diff_qk"], dims["gqa_kv"], dims["diff_v"], dims["fourier"],
                                dims["conv"], dims["gqa_q"])
    blk0 = row_start // tm
    assert row_start % tm == 0
    tok = lambda w: pl.BlockSpec((1, tm, w), lambda bi, i: (bi, i, 0))
    const = lambda shape: pl.BlockSpec(shape, lambda bi, i: (0,) * len(shape))
    tok_outs = [] if kv_only else [((b, l, nq), BF16), ((b, l, ngq), BF16), ((b, l, 2 * nf), BF16), ((b, l, nc), F32)]
    kv_shapes = [(a.shape, a.dtype) for a in kv_bufs]
    kv_specs = [pl.BlockSpec((1, nq, tm), lambda bi, i: (bi, 0, blk0 + i)),
                pl.BlockSpec((1, nkv, tm), lambda bi, i: (bi, 0, blk0 + i)),
                pl.BlockSpec((1, tm, nv), lambda bi, i: (bi, blk0 + i, 0)),
                pl.BlockSpec((1, tm, nkv), lambda bi, i: (bi, blk0 + i, 0))]
    n_alias = len(kv_bufs)
    n_in = 11
    aliases = {n_in + k: len(tok_outs) + k for k in range(n_alias)}
    outs = pl.pallas_call(
        functools.partial(_inproj_kernel, dims=dims, kv_only=kv_only, n_alias=n_alias),
        grid=(b, l // tm),
        in_specs=[tok(d),
                  pl.BlockSpec((1, 1, d), lambda bi, i: (bi, 0, 0)),
                  pl.BlockSpec((1, 1, d), lambda bi, i: (bi, 0, 1)),
                  const((1, d)),
                  pl.BlockSpec((None, d, n_tok_cols), lambda bi, i: (layer, 0, 0)),
                  const(ones_bd.shape),
                  pl.BlockSpec((tm, 2 * HEAD_DIM), lambda bi, i: (i, 0)),
                  pl.BlockSpec((tm, 2 * HEAD_DIM), lambda bi, i: (i, 0)),
                  const(qn.shape), const(kn.shape), const(dft_ch.shape)]
                 + [pl.BlockSpec(memory_space=pl.ANY)] * n_alias,
        out_specs=[tok(s[-1]) for s, _ in tok_outs] + kv_specs,
        out_shape=[jax.ShapeDtypeStruct(s, dt) for s, dt in tok_outs + kv_shapes],
        input_output_aliases=aliases,
        compiler_params=_params(("arbitrary", "arbitrary")),
        name="inproj_kv" if kv_only else "inproj",
    )(x, mod, mod, g_pre, w_in, ones_bd, cos_t, sin_t, qn, kn, dft_ch, *kv_bufs)
    return outs[:len(tok_outs)], outs[len(tok_outs):]


def _conv_ln_silu(win_ref, w_ref, b_ref, g_ref, beta_ref, yc_ref, *, rows):
    first = CONV_HALO - CONV_KERNEL // 2
    acc = jnp.zeros((rows, w_ref.shape[-1]), F32) + b_ref[...]
    for phase in range(SUBLANES):
        part = None
        for t in range(CONV_KERNEL):
            if (t + first) % SUBLANES != phase:
                continue
            base = t + first - phase
            term = w_ref[t:t + 1, :] * win_ref[base:base + rows + SUBLANES, :]
            part = term if part is None else part + term
        acc = acc + part[phase:phase + rows, :]
    mu = jnp.mean(acc, axis=-1, keepdims=True)
    cen = acc - mu
    var = jnp.mean(cen * cen, axis=-1, keepdims=True)
    y = cen * lax.rsqrt(var + EPS) * g_ref[...] + beta_ref[...]
    yc_ref[...] = (y * jax.nn.sigmoid(y)).astype(BF16)


def _conv_kernel(vprev_ref, vcur_ref, vnext_ref, cw_ref, cb_ref, cg_ref, cbeta_ref, yc_ref, win_ref):
    i = pl.program_id(1)
    rows = vcur_ref.shape[1]
    win_ref[:CONV_HALO, :] = jnp.where(i > 0, vprev_ref[0], 0.0)
    win_ref[CONV_HALO:CONV_HALO + rows, :] = vcur_ref[0]
    win_ref[CONV_HALO + rows:, :] = jnp.where(i < pl.num_programs(1) - 1, vnext_ref[0], 0.0)
    _conv_ln_silu(win_ref, cw_ref, cb_ref, cg_ref, cbeta_ref, yc_ref.at[0], rows=rows)


def _conv(vc, conv_w, conv_b, conv_g, conv_beta, *, rows):
    b, l, c = vc.shape
    rows = min(rows, l)
    hb = rows // CONV_HALO
    n_hb = l // CONV_HALO
    const = lambda a: pl.BlockSpec(a.shape, lambda bi, i: (0,) * a.ndim)
    prev = pl.BlockSpec((1, CONV_HALO, c), lambda bi, i: (bi, jnp.maximum(i * hb - 1, 0), 0))
    nxt = pl.BlockSpec((1, CONV_HALO, c), lambda bi, i: (bi, jnp.minimum((i + 1) * hb, n_hb - 1), 0))
    tile = pl.BlockSpec((1, rows, c), lambda bi, i: (bi, i, 0))
    return pl.pallas_call(
        _conv_kernel,
        grid=(b, l // rows),
        in_specs=[prev, tile, nxt, const(conv_w), const(conv_b), const(conv_g), const(conv_beta)],
        out_specs=tile,
        out_shape=jax.ShapeDtypeStruct((b, l, c), BF16),
        scratch_shapes=[pltpu.VMEM((rows + 2 * CONV_HALO, c), F32)],
        compiler_params=_params(("arbitrary", "arbitrary")),
        name="conv",
    )(vc, vc, vc, conv_w, conv_b, conv_g, conv_beta)


def _softmax_pv(q, k_t, v_ext):
    s = jnp.dot(q, k_t, preferred_element_type=F32)
    e = jnp.exp2(s - jnp.max(s, axis=-1, keepdims=True))
    return jnp.dot(e.astype(BF16), v_ext, preferred_element_type=F32)


def _attn_kernel(qd_ref, qg_ref, kdt_ref, vd_ref, kgt_ref, vg_ref, lam_ref, subln_ref, od_ref, og_ref, odl_ref,
                 vdx_ref, vgx_ref, s_ref, m0_ref, *, n_diff, n_gq, n_kv, lam_init):
    dh = HEAD_DIM
    dv = 2 * HEAD_DIM
    i = pl.program_id(1)
    tq = qd_ref.shape[1]
    last = n_diff - 1

    @pl.when(i == 0)
    def _():
        for hd in range(n_diff):
            vdx_ref[hd, :, :dv] = vd_ref[0, :, hd * dv:(hd + 1) * dv]
            vdx_ref[hd, :, dv:] = jnp.ones((vd_ref.shape[1], V7X_MXU_COLUMNS - dv), BF16)
        vgx_ref[:, :n_kv * dh] = vg_ref[0]
        vgx_ref[:, n_kv * dh:] = jnp.ones((vg_ref.shape[1], V7X_MXU_COLUMNS - n_kv * dh), BF16)
        s_ref[...] = jnp.zeros(s_ref.shape, F32)
        m0_ref[...] = jnp.zeros(m0_ref.shape, F32)

    lv = lam_ref[...]
    lam = (jnp.exp(jnp.sum(lv[0:1] * lv[1:2], axis=-1, keepdims=True))
           - jnp.exp(jnp.sum(lv[2:3] * lv[3:4], axis=-1, keepdims=True)) + lam_init)

    def diff_head_out(m0, pv1):
        o = m0 - lam * (pv1[:, :dv] / pv1[:, dv:])
        return (_rms(o, subln_ref[...]) * (1.0 - lam_init)).astype(BF16)

    def finish_last_diff_head(row0):
        s = s_ref[...]
        e = jnp.exp2(s - jnp.max(s, axis=-1, keepdims=True))
        pv1 = jnp.dot(e.astype(BF16), vdx_ref[last], preferred_element_type=F32)
        odl_ref[0, pl.ds(row0, tq), :] = diff_head_out(m0_ref[...], pv1)

    finish_last_diff_head(pl.multiple_of(jnp.maximum(i - 1, 0) * tq, tq))
    for hd in range(last):
        c = 2 * hd * dh
        pv0 = _softmax_pv(qd_ref[0, :, c:c + dh], kdt_ref[0, c:c + dh, :], vdx_ref[hd])
        pv1 = _softmax_pv(qd_ref[0, :, c + dh:c + 2 * dh], kdt_ref[0, c + dh:c + 2 * dh, :], vdx_ref[hd])
        od_ref[0, :, hd * dv:(hd + 1) * dv] = diff_head_out(pv0[:, :dv] / pv0[:, dv:], pv1)
    group = n_gq // n_kv
    lane = lax.broadcasted_iota(jnp.int32, (qg_ref.shape[1], dv), 1)
    for pair in range(n_gq // 2):
        kv = (2 * pair) // group
        halves = []
        for hq in (2 * pair, 2 * pair + 1):
            pv = _softmax_pv(qg_ref[0, :, hq * dh:(hq + 1) * dh], kgt_ref[0, kv * dh:(kv + 1) * dh, :], vgx_ref[...])
            halves.append(pv[:, :dv] / pv[:, dv:])
        left = halves[0] if kv % 2 == 0 else pltpu.roll(halves[0], dh, 1)
        right = pltpu.roll(halves[1], dh, 1) if kv % 2 == 0 else halves[1]
        og_ref[0, :, pair * dv:(pair + 1) * dv] = jnp.where(lane < dh, left, right).astype(BF16)

    c = 2 * last * dh
    pv0 = _softmax_pv(qd_ref[0, :, c:c + dh], kdt_ref[0, c:c + dh, :], vdx_ref[last])
    m0_ref[...] = pv0[:, :dv] / pv0[:, dv:]
    s_ref[...] = jnp.dot(qd_ref[0, :, c + dh:c + 2 * dh], kdt_ref[0, c + dh:c + 2 * dh, :],
                         preferred_element_type=F32)

    @pl.when(i == pl.num_programs(1) - 1)
    def _():
        finish_last_diff_head(pl.multiple_of(i * tq, tq))


def _attention(qd, qg, kv_bufs, lam_p, subln, *, dims, lam_init, tq, key_start, key_len):
    kd_t, kg_t, vd, vg = kv_bufs
    b, l, nq = qd.shape
    n_diff, n_gq, n_kv = dims["diff_heads"], dims["gqa_q"] // HEAD_DIM, dims["gqa_kv"] // HEAD_DIM
    assert n_kv * HEAD_DIM == 2 * HEAD_DIM and (n_gq // n_kv) % 2 == 0 and key_start % key_len == 0
    kb = key_start // key_len
    tq = min(tq, l)
    tok = lambda w: pl.BlockSpec((1, tq, w), lambda bi, i: (bi, i, 0))
    keys_t = lambda a: pl.BlockSpec((1, a.shape[1], key_len), lambda bi, i: (bi, 0, kb))
    vals = lambda a: pl.BlockSpec((1, key_len, a.shape[2]), lambda bi, i: (bi, kb, 0))
    const = lambda a: pl.BlockSpec(a.shape, lambda bi, i: (0,) * a.ndim)
    dv = 2 * HEAD_DIM
    od, og, od_last = pl.pallas_call(
        functools.partial(_attn_kernel, n_diff=n_diff, n_gq=n_gq, n_kv=n_kv, lam_init=lam_init),
        grid=(b, l // tq),
        in_specs=[tok(nq), tok(qg.shape[-1]), keys_t(kd_t), vals(vd), keys_t(kg_t), vals(vg),
                  const(lam_p), const(subln)],
        out_specs=[tok((n_diff - 1) * dv), tok(dims["gqa_q"]), pl.BlockSpec((1, l, dv), lambda bi, i: (bi, 0, 0))],
        out_shape=[jax.ShapeDtypeStruct((b, l, (n_diff - 1) * dv), BF16),
                   jax.ShapeDtypeStruct((b, l, dims["gqa_q"]), BF16),
                   jax.ShapeDtypeStruct((b, l, dv), BF16)],
        scratch_shapes=[pltpu.VMEM((n_diff, key_len, V7X_MXU_COLUMNS), BF16),
                        pltpu.VMEM((key_len, V7X_MXU_COLUMNS), BF16),
                        pltpu.VMEM((tq, key_len), F32),
                        pltpu.VMEM((tq, dv), F32)],
        compiler_params=_params(("arbitrary", "arbitrary")),
        name="attention",
    )(qd, qg, kd_t, vd, kg_t, vg, lam_p, subln)
    return od, od_last, og


def _fourier_kernel(c_ref, s_ref, xf_ref, o_ref, *, nf, out_scale):
    y = (jnp.dot(c_ref[...], xf_ref[0, :, :nf], preferred_element_type=F32)
         - jnp.dot(s_ref[...], xf_ref[0, :, nf:], preferred_element_type=F32))
    o_ref[0] = (y * out_scale).astype(BF16)


def _fourier_dense(tables, xf, *, tr=512):
    dft_c, dft_s = tables
    b, l, nf2 = xf.shape
    nf = nf2 // 2
    tr = min(tr, l)
    return pl.pallas_call(
        functools.partial(_fourier_kernel, nf=nf, out_scale=float(l) ** -0.5),
        grid=(b, l // tr),
        in_specs=[pl.BlockSpec((tr, l), lambda bi, i: (i, 0)),
                  pl.BlockSpec((tr, l), lambda bi, i: (i, 0)),
                  pl.BlockSpec((1, l, nf2), lambda bi, i: (bi, 0, 0))],
        out_specs=pl.BlockSpec((1, tr, nf), lambda bi, i: (bi, i, 0)),
        out_shape=jax.ShapeDtypeStruct((b, l, nf), BF16),
        compiler_params=_params(("arbitrary", "arbitrary")),
        name="fourier",
    )(dft_c, dft_s, xf)


def _fft_a_kernel(z_ref, m_ref, cos_ref, sin_ref, o_ref, *, nf):
    n1, r = z_ref.shape[1], z_ref.shape[2]
    rows = n1 * r
    z = z_ref[0].reshape(rows, 2 * nf)
    t = jnp.dot(m_ref[...], z, preferred_element_type=F32)
    cz, sz = t[:rows], t[rows:]
    ar = cz[:, :nf] - sz[:, nf:]
    ai = -(sz[:, :nf] + cz[:, nf:])
    ct = _tile_lanes(cos_ref[0], nf)
    st = _tile_lanes(sin_ref[0], nf)
    o_ref[0, :, :, :nf] = (ar * ct + ai * st).reshape(n1, r, nf).astype(BF16)
    o_ref[0, :, :, nf:] = (ai * ct - ar * st).reshape(n1, r, nf).astype(BF16)


def _fft_c_kernel(b_ref, mc_ref, ms_ref, o_ref, *, nf, out_scale):
    r, n2 = b_ref.shape[1], b_ref.shape[2]
    blk = b_ref[0].reshape(r * n2, 2 * nf)
    y = (jnp.dot(mc_ref[...], blk[:, :nf], preferred_element_type=F32)
         + jnp.dot(ms_ref[...], blk[:, nf:], preferred_element_type=F32))
    o_ref[0] = (y * out_scale).reshape(n2, r, nf).astype(BF16)


def _fourier_two_stage(tables, xf):
    m_a, tw_cos, tw_sin, m_c_cos, m_c_sin = tables
    b, l, nf2 = xf.shape
    nf = nf2 // 2
    r = BF16_SUBLANES
    n2 = m_c_cos.shape[0] // r
    n1 = l // n2
    tile = lambda w: pl.BlockSpec((1, n1, r, w), lambda bi, s: (bi, 0, s, 0))
    const = lambda a: pl.BlockSpec(a.shape, lambda bi, s: (0,) * a.ndim)
    twiddle = pl.BlockSpec((1, n1 * r, 2 * HEAD_DIM), lambda bi, s: (s, 0, 0))
    staged = pl.pallas_call(
        functools.partial(_fft_a_kernel, nf=nf),
        grid=(b, n2 // r),
        in_specs=[tile(nf2), const(m_a), twiddle, twiddle],
        out_specs=tile(nf2),
        out_shape=jax.ShapeDtypeStruct((b, n1, n2, nf2), BF16),
        compiler_params=_params(("arbitrary", "arbitrary")),
        name="fourier_a",
    )(xf.reshape(b, n1, n2, nf2), m_a, tw_cos, tw_sin)
    out = pl.pallas_call(
        functools.partial(_fft_c_kernel, nf=nf, out_scale=float(l) ** -0.5),
        grid=(b, n1 // r),
        in_specs=[pl.BlockSpec((1, r, n2, nf2), lambda bi, s: (bi, s, 0, 0)), const(m_c_cos), const(m_c_sin)],
        out_specs=pl.BlockSpec((1, n2, r, nf), lambda bi, s: (bi, 0, s, 0)),
        out_shape=jax.ShapeDtypeStruct((b, n2, n1, nf), BF16),
        compiler_params=_params(("arbitrary", "arbitrary")),
        name="fourier_c",
    )(staged, m_c_cos, m_c_sin)
    return out.reshape(b, l, nf)


def _merge_kernel(x_ref, shift_ref, scale_ref, gate_ref, g_ref, wgate_ref, yf_ref, od_ref, odl_ref, og_ref, yc_ref,
                  wf_ref, wd_ref, wg_ref, wc_ref, wout_ref, gpost_ref, o_ref):
    x = x_ref[0]
    d = x.shape[-1]
    h = _modulated(x, g_ref[...], shift_ref[0], scale_ref[0]).astype(BF16)
    diff = jnp.concatenate([od_ref[0], odl_ref[0]], axis=-1)
    branches = ((yf_ref[0], wf_ref), (diff, wd_ref), (og_ref[0], wg_ref), (yc_ref[0], wc_ref))
    merged = None
    for k, (y, w_ref) in enumerate(branches):
        gate = jax.nn.sigmoid(jnp.dot(h, wgate_ref[:, k * d:(k + 1) * d], preferred_element_type=F32))
        term = gate * jnp.dot(y, w_ref[...], preferred_element_type=F32)
        merged = term if merged is None else merged + term
    mix = jnp.dot(merged.astype(BF16), wout_ref[...], preferred_element_type=F32)
    o_ref[0] = x + gate_ref[0] * _rms(mix, gpost_ref[...])


def _merge(x, mod, g_pre, w_gate, yf, od, od_last, og, yc, wf, wd, wg, wc, wout, g_post, *, tm):
    b, l, d = x.shape
    tm = min(tm, l)
    tok = lambda w: pl.BlockSpec((1, tm, w), lambda bi, i: (bi, i, 0))
    modc = lambda k: pl.BlockSpec((1, 1, d), lambda bi, i: (bi, 0, k))
    const = lambda a: pl.BlockSpec(a.shape, lambda bi, i: (0,) * a.ndim)
    return pl.pallas_call(
        _merge_kernel,
        grid=(b, l // tm),
        in_specs=[tok(d), modc(0), modc(1), modc(2), const(g_pre), const(w_gate),
                  tok(yf.shape[-1]), tok(od.shape[-1]), tok(od_last.shape[-1]), tok(og.shape[-1]), tok(yc.shape[-1]),
                  const(wf), const(wd), const(wg), const(wc), const(wout), const(g_post)],
        out_specs=tok(d),
        out_shape=jax.ShapeDtypeStruct((b, l, d), F32),
        compiler_params=_params(("arbitrary", "arbitrary")),
        name="merge",
    )(x, mod, mod, mod, g_pre, w_gate, yf, od, od_last, og, yc, wf, wd, wg, wc, wout, g_post)


def _mlp_kernel(x_ref, shift_ref, scale_ref, gate_ref, g_ref, w1_ref, w2_ref, gpost_ref, o_ref, *, ff_chunk):
    x = x_ref[0]
    h = _modulated(x, g_ref[...], shift_ref[0], scale_ref[0]).astype(BF16)
    acc = None
    for c0 in range(0, w1_ref.shape[-1], ff_chunk):
        u = jnp.maximum(jnp.dot(h, w1_ref[:, c0:c0 + ff_chunk], preferred_element_type=F32), 0.0)
        part = jnp.dot((u * u).astype(BF16), w2_ref[c0:c0 + ff_chunk, :], preferred_element_type=F32)
        acc = part if acc is None else acc + part
    o_ref[0] = x + gate_ref[0] * _rms(acc, gpost_ref[...])


def _mlp(x, mod, g_pre, w1, w2, g_post, *, tm):
    b, l, d = x.shape
    tm = min(tm, l)
    tok = lambda w: pl.BlockSpec((1, tm, w), lambda bi, i: (bi, i, 0))
    modc = lambda k: pl.BlockSpec((1, 1, d), lambda bi, i: (bi, 0, k))
    const = lambda a: pl.BlockSpec(a.shape, lambda bi, i: (0,) * a.ndim)
    return pl.pallas_call(
        functools.partial(_mlp_kernel, ff_chunk=1024),
        grid=(b, l // tm),
        in_specs=[tok(d), modc(3), modc(4), modc(5), const(g_pre), const(w1), const(w2), const(g_post)],
        out_specs=tok(d),
        out_shape=jax.ShapeDtypeStruct((b, l, d), F32),
        compiler_params=_params(("arbitrary", "arbitrary")),
        name="mlp",
    )(x, mod, mod, mod, g_pre, w1, w2, g_post)


def _angles(i, j, period):
    return (np.outer(i, j) % period).astype(np.float64) * (2.0 * np.pi / period)


def _mxu_table(values):
    return jnp.asarray(values, F32).astype(BF16)


def _rope_tables(seq_len):
    rows = seq_len // GRID_W
    row = np.repeat(np.arange(rows, dtype=np.float64), GRID_W)
    col = np.tile(np.arange(GRID_W, dtype=np.float64), rows)
    inv_freq = ROPE_THETA ** (-np.arange(ROPE_FREQS, dtype=np.float64) / ROPE_FREQS)
    ang_r, ang_c = row[:, None] * inv_freq, col[:, None] * inv_freq
    cos64 = np.concatenate([np.cos(ang_r)] * 2 + [np.cos(ang_c)] * 2, axis=-1)
    sin64 = np.concatenate([-np.sin(ang_r), np.sin(ang_r), -np.sin(ang_c), np.sin(ang_c)], axis=-1)
    return (jnp.asarray(np.concatenate([cos64] * 2, axis=-1), F32),
            jnp.asarray(np.concatenate([sin64] * 2, axis=-1), F32))


def _seq_dft(l):
    th = _angles(np.arange(l), np.arange(l), l)
    return _mxu_table(np.cos(th)), _mxu_table(np.sin(th))


def _two_stage_dft(l, n2):
    n1 = l // n2
    r = BF16_SUBLANES
    i1, i2 = np.arange(n1), np.arange(n2)
    eye = np.eye(r)
    th_a = _angles(i1, i1, n1)
    m_a = np.concatenate([np.kron(np.cos(th_a), eye), np.kron(np.sin(th_a), eye)], axis=0)
    th_t = _angles(i1, i2, l).reshape(n1, n2 // r, r)
    th_t = np.transpose(th_t, (1, 0, 2)).reshape(n2 // r, n1 * r)
    lanes = lambda t: jnp.asarray(np.broadcast_to(t[:, :, None], t.shape + (2 * HEAD_DIM,)), F32)
    th_c = _angles(i2, i2, n2)
    perm = lambda w: _mxu_table(np.einsum('kb,ij->kijb', w, eye).reshape(n2 * r, r * n2))
    return _mxu_table(m_a), lanes(np.cos(th_t)), lanes(np.sin(th_t)), perm(np.cos(th_c)), perm(np.sin(th_c))


def _channel_dft(n_groups):
    g = FOURIER_GROUP_DIM
    th = _angles(np.arange(g), np.arange(g), g)
    eye = np.eye(n_groups)
    scale = float(g) ** -0.5
    return _mxu_table(np.concatenate([np.kron(eye, np.cos(th) * scale), np.kron(eye, np.sin(th) * scale)], axis=-1))


def _seq_fourier(l):
    n2 = FOURIER_STAGE_LEN
    if l % n2 == 0 and (l // n2) % BF16_SUBLANES == 0:
        return functools.partial(_fourier_two_stage, _two_stage_dft(l, n2))
    return functools.partial(_fourier_dense, _seq_dft(l))


def _mix_and_mlp(x, mod, lw, tok, kv_bufs, fourier, *, dims, lam_init, key_start, key_len, tm_mlp):
    qd, qg, xf, vc = tok
    od, od_last, og = _attention(qd, qg, kv_bufs, lw["lam_p"], lw["subln"], dims=dims, lam_init=lam_init, tq=256,
                                 key_start=key_start, key_len=key_len)
    yf = fourier(xf)
    yc = _conv(vc, lw["conv_w"], lw["conv_b"], lw["conv_ln_g"], lw["conv_ln_b"], rows=512)
    x = _merge(x, mod, lw["g_pre_mix"], lw["w_gate"], yf, od, od_last, og, yc, lw["w_br_f"], lw["w_br_d"],
               lw["w_br_g"], lw["w_br_c"], lw["w_out"], lw["g_post_mix"], tm=512)
    return _mlp(x, mod, lw["g_pre_mlp"], lw["w_ff1"], lw["w_ff2"], lw["g_post_mlp"], tm=tm_mlp)


def kernel(x, c, ctx, c_ctx, w_mod, b_mod, g_pre_mix, g_post_mix, g_pre_mlp, g_post_mlp, w_in, q_norm, k_norm,
           diff_lambda, diff_subln, conv_dw, conv_dw_bias, conv_ln_g, conv_ln_b, w_br_fourier, w_br_diff,
           w_br_gqa, w_br_conv, w_out, w_ff1, w_ff2):
    b, l, d = x.shape
    lc = ctx.shape[1]
    lk = l + lc
    depth = w_in.shape[0]
    dims = _dims(d)
    assert b + 1 <= MOD_ROWS and l % GRID_W == 0 and l % lc == 0

    rope_x = _rope_tables(l)
    rope_none = (jnp.ones((lc, 2 * HEAD_DIM), F32), jnp.zeros((lc, 2 * HEAD_DIM), F32))
    fourier_x, fourier_c = _seq_fourier(l), _seq_fourier(lc)
    n_groups = dims["fourier"] // FOURIER_GROUP_DIM
    heads = np.arange(dims["diff_qk"]) // HEAD_DIM
    ones_bd = jnp.asarray((heads[:, None] == heads[None, :]).astype(BF16))
    dft_ch = _channel_dft(n_groups)

    c_rows = jnp.zeros((MOD_ROWS, d), F32).at[:b].set(c).at[b].set(c_ctx)
    mod_all = _modulation(c_rows, w_mod, b_mod)

    n_tok_cols = w_in.shape[-1] - N_BRANCHES * d

    xc = ctx
    for li in range(depth):
        lam_init = 0.8 - 0.6 * math.exp(-0.3 * li)
        row = lambda a: a[li].reshape(1, -1)
        lw = dict(
            w_gate=lax.optimization_barrier(w_in[li, :, n_tok_cols:]).astype(BF16),
            g_pre_mix=row(g_pre_mix), g_post_mix=row(g_post_mix), g_pre_mlp=row(g_pre_mlp),
            g_post_mlp=row(g_post_mlp),
            qn=jnp.tile(q_norm[li], dims["gqa_q"] // HEAD_DIM).reshape(1, -1),
            kn=jnp.tile(k_norm[li], dims["gqa_kv"] // HEAD_DIM).reshape(1, -1),
            lam_p=diff_lambda[li], subln=row(diff_subln),
            conv_w=conv_dw[li], conv_b=row(conv_dw_bias), conv_ln_g=row(conv_ln_g), conv_ln_b=row(conv_ln_b),
            w_br_f=w_br_fourier[li].astype(BF16), w_br_d=w_br_diff[li].astype(BF16),
            w_br_g=w_br_gqa[li].astype(BF16), w_br_c=w_br_conv[li].astype(BF16), w_out=w_out[li].astype(BF16),
            w_ff1=w_ff1[li].astype(BF16), w_ff2=w_ff2[li].astype(BF16))
        mod_x = mod_all[li, :b].reshape(b, 1, N_MOD * d)
        mod_c = jnp.broadcast_to(mod_all[li, b].reshape(1, 1, N_MOD * d), (b, 1, N_MOD * d))
        last = li == depth - 1

        inproj = functools.partial(_inproj, g_pre=lw["g_pre_mix"], w_in=w_in, ones_bd=ones_bd, qn=lw["qn"],
                                   kn=lw["kn"], dft_ch=dft_ch, dims=dims, layer=li, n_tok_cols=n_tok_cols)
        kv_bufs = (jnp.zeros((b, dims["diff_qk"], lk), BF16), jnp.zeros((b, dims["gqa_kv"], lk), BF16),
                   jnp.zeros((b, lk, dims["diff_v"]), BF16), jnp.zeros((b, lk, dims["gqa_kv"]), BF16))
        tok_c, kv_bufs = inproj(xc, mod_c, cos_t=rope_none[0], sin_t=rope_none[1], kv_bufs=kv_bufs, kv_only=last,
                                row_start=l, tm=lc)
        tok_x, kv_bufs = inproj(x, mod_x, cos_t=rope_x[0], sin_t=rope_x[1], kv_bufs=kv_bufs, kv_only=False,
                                row_start=0, tm=512)
        mix = functools.partial(_mix_and_mlp, lw=lw, kv_bufs=kv_bufs, dims=dims, lam_init=lam_init)
        x = mix(x, mod_x, tok=tok_x, fourier=fourier_x, key_start=0, key_len=lk, tm_mlp=512)
        if not last:
            xc = mix(xc, mod_c, tok=tok_c, fourier=fourier_c, key_start=l, key_len=lc, tm_mlp=lc)
    return x
```

```python
import functools
import math

import jax
import jax.numpy as jnp
import numpy as np
from jax import lax
from jax.experimental import pallas as pl
from jax.experimental.pallas import tpu as pltpu

F32 = jnp.float32
BF16 = jnp.bfloat16

GRID_W = 64
HEAD_DIM = 64
ROPE_FREQS = HEAD_DIM // 4
ROPE_THETA = 10000.0
EPS = 1e-6
ATTN_SCALE = HEAD_DIM ** -0.5
Q_PRESCALE = ATTN_SCALE * math.log2(math.e)
FOURIER_GROUP_DIM = 64
CONV_KERNEL = 31
CONV_HALO = 16
N_BRANCHES = 4
N_MOD = 6
MOD_ROWS = 8

V7X_VMEM_LIMIT_BYTES = 56 * 1024 * 1024
V7X_MXU_COLUMNS = 256
SUBLANES = 8
BF16_SUBLANES = 16
FOURIER_STAGE_LEN = 64

TILE_ROWS = dict(inproj=512, attention=256, conv=512, merge=512, mlp=512)


def _dims(d_model):
    fourier = 3 * d_model // 8
    diff_heads = d_model // 256
    diff_qk = diff_heads * 2 * HEAD_DIM
    diff_v = diff_heads * 2 * HEAD_DIM
    gqa_q = (d_model // 128) * HEAD_DIM
    gqa_kv = (d_model // 128 // 4) * HEAD_DIM
    conv = 3 * d_model // 8
    return dict(fourier=fourier, diff_heads=diff_heads, diff_qk=diff_qk, diff_v=diff_v,
                gqa_q=gqa_q, gqa_kv=gqa_kv, conv=conv)


def _params(semantics):
    return pltpu.CompilerParams(dimension_semantics=semantics, vmem_limit_bytes=V7X_VMEM_LIMIT_BYTES)


def _mod_kernel(c_ref, w_ref, b_ref, o_ref):
    c = c_ref[...]
    a = c * jax.nn.sigmoid(c)
    o_ref[...] = jnp.dot(a, w_ref[...], preferred_element_type=F32,
                         precision=lax.Precision.HIGHEST) + b_ref[...]


def _modulation(c_rows, w_mod, b_mod):
    depth, d, n = w_mod.shape
    tn = 2 * d
    return pl.pallas_call(
        _mod_kernel,
        grid=(depth, n // tn),
        in_specs=[pl.BlockSpec((MOD_ROWS, d), lambda l, j: (0, 0)),
                  pl.BlockSpec((None, d, tn), lambda l, j: (l, 0, j)),
                  pl.BlockSpec((None, 1, tn), lambda l, j: (l, 0, j))],
        out_specs=pl.BlockSpec((None, MOD_ROWS, tn), lambda l, j: (l, 0, j)),
        out_shape=jax.ShapeDtypeStruct((depth, MOD_ROWS, n), F32),
        compiler_params=_params(("arbitrary", "arbitrary")),
        name="modulation",
    )(c_rows, w_mod, b_mod.reshape(depth, 1, n))


def _modulated(x, g, shift, scale):
    h = x * lax.rsqrt(jnp.mean(x * x, axis=-1, keepdims=True) + EPS) * g
    return h * (1.0 + scale) + shift


def _rms(x, g):
    return x * lax.rsqrt(jnp.mean(x * x, axis=-1, keepdims=True) + EPS) * g


def _swap_halves(t):
    w = t.shape[-1]
    lane = lax.broadcasted_iota(jnp.int32, t.shape, 1)
    from_right = pltpu.roll(t, w - ROPE_FREQS, 1)
    from_left = pltpu.roll(t, ROPE_FREQS, 1)
    return jnp.where((lane % (2 * ROPE_FREQS)) < ROPE_FREQS, from_right, from_left)


def _tile_lanes(t, width):
    reps = width // t.shape[-1]
    return t if reps == 1 else jnp.concatenate([t] * reps, axis=-1)


def _head_norm(t, ones_bd, gain):
    w = t.shape[-1]
    t2 = t * t
    hi = t2.astype(BF16)
    lo = (t2 - hi.astype(F32)).astype(BF16)
    e = ones_bd[:w, :w]
    ss = (jnp.dot(hi, e, preferred_element_type=F32) + jnp.dot(lo, e, preferred_element_type=F32))
    return t * lax.rsqrt(ss * (1.0 / HEAD_DIM) + EPS) * gain


def _inproj_kernel(x_ref, shift_ref, scale_ref, g_ref, w_ref, ones_ref, cos_ref, sin_ref, qn_ref, kn_ref,
                   dft_ref, *rest, dims, kv_only, n_alias):
    out_refs = rest[n_alias:]
    nq, nkv, nv, nf, nc = dims["diff_qk"], dims["gqa_kv"], dims["diff_v"], dims["fourier"], dims["conv"]
    ngq = dims["gqa_q"]
    h = _modulated(x_ref[0], g_ref[...], shift_ref[0], scale_ref[0]).astype(BF16)

    def proj(start, width):
        return jnp.dot(h, w_ref[:, start:start + width].astype(BF16), preferred_element_type=F32)

    def rope(t):
        w = t.shape[-1]
        return t * _tile_lanes(cos_ref[...], w) + _swap_halves(t) * _tile_lanes(sin_ref[...], w)

    ones_bd = ones_ref[...]
    off = 0
    dk = proj(off, nq); off += nq
    dv = proj(off, nv); off += nv
    gk = proj(off, nkv); off += nkv
    gv = proj(off, nkv); off += nkv
    kdt_ref, kgt_ref, vd_ref, vg_ref = out_refs[-4:]
    kdt_ref[0] = rope(dk).T.astype(BF16)
    kgt_ref[0] = rope(_head_norm(gk, ones_bd, kn_ref[...])).T.astype(BF16)
    vd_ref[0] = dv.astype(BF16)
    vg_ref[0] = gv.astype(BF16)
    if not kv_only:
        qd_ref, qg_ref, xf_ref, vc_ref = out_refs[:4]
        uf = proj(off, nf).astype(BF16); off += nf
        xf_ref[0] = jnp.dot(uf, dft_ref[...], preferred_element_type=F32).astype(BF16)
        dq = proj(off, nq); off += nq
        qd_ref[0] = (rope(dq) * Q_PRESCALE).astype(BF16)
        gq = proj(off, ngq); off += ngq
        qg_ref[0] = (rope(_head_norm(gq, ones_bd, qn_ref[...])) * Q_PRESCALE).astype(BF16)
        glu = proj(off, 2 * nc)
        vc_ref[0] = glu[:, :nc] * jax.nn.sigmoid(glu[:, nc:])


def _inproj(x, mod, g_pre, w_in, ones_bd, cos_t, sin_t, qn, kn, dft_ch, kv_bufs, *, dims, layer, n_tok_cols,
            kv_only, row_start, tm):
    b, l, d = x.shape
    tm = min(tm, l)
    nq, nkv, nv, nf, nc, ngq = (dims["diff_qk"], dims["gqa_kv"], dims["diff_v"], dims["fourier"],
                                dims["conv"], dims["gqa_q"])
    blk0 = row_start // tm
    assert row_start % tm == 0
    tok = lambda w: pl.BlockSpec((1, tm, w), lambda bi, i: (bi, i, 0))
    const = lambda shape: pl.BlockSpec(shape, lambda bi, i: (0,) * len(shape))
    tok_outs = [] if kv_only else [((b, l, nq), BF16), ((b, l, ngq), BF16), ((b, l, 2 * nf), BF16), ((b, l, nc), F32)]
    kv_shapes = [(a.shape, a.dtype) for a in kv_bufs]
    kv_specs = [pl.BlockSpec((1, nq, tm), lambda bi, i: (bi, 0, blk0 + i)),
                pl.BlockSpec((1, nkv, tm), lambda bi, i: (bi, 0, blk0 + i)),
                pl.BlockSpec((1, tm, nv), lambda bi, i: (bi, blk0 + i, 0)),
                pl.BlockSpec((1, tm, nkv), lambda bi, i: (bi, blk0 + i, 0))]
    n_alias = len(kv_bufs)
    n_in = 11
    aliases = {n_in + k: len(tok_outs) + k for k in range(n_alias)}
    outs = pl.pallas_call(
        functools.partial(_inproj_kernel, dims=dims, kv_only=kv_only, n_alias=n_alias),
        grid=(b, l // tm),
        in_specs=[tok(d),
                  pl.BlockSpec((1, 1, d), lambda bi, i: (bi, 0, 0)),
                  pl.BlockSpec((1, 1, d), lambda bi, i: (bi, 0, 1)),
                  const((1, d)),
                  pl.BlockSpec((None, d, n_tok_cols), lambda bi, i: (layer, 0, 0)),
                  const(ones_bd.shape),
                  pl.BlockSpec((tm, 2 * HEAD_DIM), lambda bi, i: (i, 0)),
                  pl.BlockSpec((tm, 2 * HEAD_DIM), lambda bi, i: (i, 0)),
                  const(qn.shape), const(kn.shape), const(dft_ch.shape)]
                 + [pl.BlockSpec(memory_space=pl.ANY)] * n_alias,
        out_specs=[tok(s[-1]) for s, _ in tok_outs] + kv_specs,
        out_shape=[jax.ShapeDtypeStruct(s, dt) for s, dt in tok_outs + kv_shapes],
        input_output_aliases=aliases,
        compiler_params=_params(("arbitrary", "arbitrary")),
        name="inproj_kv" if kv_only else "inproj",
    )(x, mod, mod, g_pre, w_in, ones_bd, cos_t, sin_t, qn, kn, dft_ch, *kv_bufs)
    return outs[:len(tok_outs)], outs[len(tok_outs):]


def _conv_ln_silu(win_ref, w_ref, b_ref, g_ref, beta_ref, yc_ref, *, rows):
    first = CONV_HALO - CONV_KERNEL // 2
    acc = jnp.zeros((rows, w_ref.shape[-1]), F32) + b_ref[...]
    for phase in range(SUBLANES):
        part = None
        for t in range(CONV_KERNEL):
            if (t + first) % SUBLANES != phase:
                continue
            base = t + first - phase
            term = w_ref[t:t + 1, :] * win_ref[base:base + rows + SUBLANES, :]
            part = term if part is None else part + term
        acc = acc + part[phase:phase + rows, :]
    mu = jnp.mean(acc, axis=-1, keepdims=True)
    cen = acc - mu
    var = jnp.mean(cen * cen, axis=-1, keepdims=True)
    y = cen * lax.rsqrt(var + EPS) * g_ref[...] + beta_ref[...]
    yc_ref[...] = (y * jax.nn.sigmoid(y)).astype(BF16)


def _conv_kernel(vprev_ref, vcur_ref, vnext_ref, cw_ref, cb_ref, cg_ref, cbeta_ref, yc_ref, win_ref):
    i = pl.program_id(1)
    rows = vcur_ref.shape[1]
    win_ref[:CONV_HALO, :] = jnp.where(i > 0, vprev_ref[0], 0.0)
    win_ref[CONV_HALO:CONV_HALO + rows, :] = vcur_ref[0]
    win_ref[CONV_HALO + rows:, :] = jnp.where(i < pl.num_programs(1) - 1, vnext_ref[0], 0.0)
    _conv_ln_silu(win_ref, cw_ref, cb_ref, cg_ref, cbeta_ref, yc_ref.at[0], rows=rows)


def _conv(vc, conv_w, conv_b, conv_g, conv_beta, *, rows):
    b, l, c = vc.shape
    rows = min(rows, l)
    hb = rows // CONV_HALO
    n_hb = l // CONV_HALO
    const = lambda a: pl.BlockSpec(a.shape, lambda bi, i: (0,) * a.ndim)
    prev = pl.BlockSpec((1, CONV_HALO, c), lambda bi, i: (bi, jnp.maximum(i * hb - 1, 0), 0))
    nxt = pl.BlockSpec((1, CONV_HALO, c), lambda bi, i: (bi, jnp.minimum((i + 1) * hb, n_hb - 1), 0))
    tile = pl.BlockSpec((1, rows, c), lambda bi, i: (bi, i, 0))
    return pl.pallas_call(
        _conv_kernel,
        grid=(b, l // rows),
        in_specs=[prev, tile, nxt, const(conv_w), const(conv_b), const(conv_g), const(conv_beta)],
        out_specs=tile,
        out_shape=jax.ShapeDtypeStruct((b, l, c), BF16),
        scratch_shapes=[pltpu.VMEM((rows + 2 * CONV_HALO, c), F32)],
        compiler_params=_params(("arbitrary", "arbitrary")),
        name="conv",
    )(vc, vc, vc, conv_w, conv_b, conv_g, conv_beta)


def _softmax_pv(q, k_t, v_ext):
    s = jnp.dot(q, k_t, preferred_element_type=F32)
    e = jnp.exp2(s - jnp.max(s, axis=-1, keepdims=True))
    return jnp.dot(e.astype(BF16), v_ext, preferred_element_type=F32)


def _attn_kernel(qd_ref, qg_ref, kdt_ref, vd_ref, kgt_ref, vg_ref, lam_ref, subln_ref, od_ref, og_ref, odl_ref,
                 vdx_ref, vgx_ref, s_ref, m0_ref, *, n_diff, n_gq, n_kv, lam_init):
    dh = HEAD_DIM
    dv = 2 * HEAD_DIM
    i = pl.program_id(1)
    tq = qd_ref.shape[1]
    last = n_diff - 1

    @pl.when(i == 0)
    def _():
        for hd in range(n_diff):
            vdx_ref[hd, :, :dv] = vd_ref[0, :, hd * dv:(hd + 1) * dv]
            vdx_ref[hd, :, dv:] = jnp.ones((vd_ref.shape[1], V7X_MXU_COLUMNS - dv), BF16)
        vgx_ref[:, :n_kv * dh] = vg_ref[0]
        vgx_ref[:, n_kv * dh:] = jnp.ones((vg_ref.shape[1], V7X_MXU_COLUMNS - n_kv * dh), BF16)
        s_ref[...] = jnp.zeros(s_ref.shape, F32)
        m0_ref[...] = jnp.zeros(m0_ref.shape, F32)

    lv = lam_ref[...]
    lam = (jnp.exp(jnp.sum(lv[0:1] * lv[1:2], axis=-1, keepdims=True))
           - jnp.exp(jnp.sum(lv[2:3] * lv[3:4], axis=-1, keepdims=True)) + lam_init)

    def diff_head_out(m0, pv1):
        o = m0 - lam * (pv1[:, :dv] / pv1[:, dv:])
        return (_rms(o, subln_ref[...]) * (1.0 - lam_init)).astype(BF16)

    def finish_last_diff_head(row0):
        s = s_ref[...]
        e = jnp.exp2(s - jnp.max(s, axis=-1, keepdims=True))
        pv1 = jnp.dot(e.astype(BF16), vdx_ref[last], preferred_element_type=F32)
        odl_ref[0, pl.ds(row0, tq), :] = diff_head_out(m0_ref[...], pv1)

    finish_last_diff_head(pl.multiple_of(jnp.maximum(i - 1, 0) * tq, tq))
    for hd in range(last):
        c = 2 * hd * dh
        pv0 = _softmax_pv(qd_ref[0, :, c:c + dh], kdt_ref[0, c:c + dh, :], vdx_ref[hd])
        pv1 = _softmax_pv(qd_ref[0, :, c + dh:c + 2 * dh], kdt_ref[0, c + dh:c + 2 * dh, :], vdx_ref[hd])
        od_ref[0, :, hd * dv:(hd + 1) * dv] = diff_head_out(pv0[:, :dv] / pv0[:, dv:], pv1)
    group = n_gq // n_kv
    lane = lax.broadcasted_iota(jnp.int32, (qg_ref.shape[1], dv), 1)
    for pair in range(n_gq // 2):
        kv = (2 * pair) // group
        halves = []
        for hq in (2 * pair, 2 * pair + 1):
            pv = _softmax_pv(qg_ref[0, :, hq * dh:(hq + 1) * dh], kgt_ref[0, kv * dh:(kv + 1) * dh, :], vgx_ref[...])
            halves.append(pv[:, :dv] / pv[:, dv:])
        left = halves[0] if kv % 2 == 0 else pltpu.roll(halves[0], dh, 1)
        right = pltpu.roll(halves[1], dh, 1) if kv % 2 == 0 else halves[1]
        og_ref[0, :, pair * dv:(pair + 1) * dv] = jnp.where(lane < dh, left, right).astype(BF16)

    c = 2 * last * dh
    pv0 = _softmax_pv(qd_ref[0, :, c:c + dh], kdt_ref[0, c:c + dh, :], vdx_ref[last])
    m0_ref[...] = pv0[:, :dv] / pv0[:, dv:]
    s_ref[...] = jnp.dot(qd_ref[0, :, c + dh:c + 2 * dh], kdt_ref[0, c + dh:c + 2 * dh, :],
                         preferred_element_type=F32)

    @pl.when(i == pl.num_programs(1) - 1)
    def _():
        finish_last_diff_head(pl.multiple_of(i * tq, tq))


def _attention(qd, qg, kv_bufs, lam_p, subln, *, dims, lam_init, tq, key_start, key_len):
    kd_t, kg_t, vd, vg = kv_bufs
    b, l, nq = qd.shape
    n_diff, n_gq, n_kv = dims["diff_heads"], dims["gqa_q"] // HEAD_DIM, dims["gqa_kv"] // HEAD_DIM
    assert n_kv * HEAD_DIM == 2 * HEAD_DIM and (n_gq // n_kv) % 2 == 0 and key_start % key_len == 0
    kb = key_start // key_len
    tq = min(tq, l)
    tok = lambda w: pl.BlockSpec((1, tq, w), lambda bi, i: (bi, i, 0))
    keys_t = lambda a: pl.BlockSpec((1, a.shape[1], key_len), lambda bi, i: (bi, 0, kb))
    vals = lambda a: pl.BlockSpec((1, key_len, a.shape[2]), lambda bi, i: (bi, kb, 0))
    const = lambda a: pl.BlockSpec(a.shape, lambda bi, i: (0,) * a.ndim)
    dv = 2 * HEAD_DIM
    od, og, od_last = pl.pallas_call(
        functools.partial(_attn_kernel, n_diff=n_diff, n_gq=n_gq, n_kv=n_kv, lam_init=lam_init),
        grid=(b, l // tq),
        in_specs=[tok(nq), tok(qg.shape[-1]), keys_t(kd_t), vals(vd), keys_t(kg_t), vals(vg),
                  const(lam_p), const(subln)],
        out_specs=[tok((n_diff - 1) * dv), tok(dims["gqa_q"]), pl.BlockSpec((1, l, dv), lambda bi, i: (bi, 0, 0))],
        out_shape=[jax.ShapeDtypeStruct((b, l, (n_diff - 1) * dv), BF16),
                   jax.ShapeDtypeStruct((b, l, dims["gqa_q"]), BF16),
                   jax.ShapeDtypeStruct((b, l, dv), BF16)],
        scratch_shapes=[pltpu.VMEM((n_diff, key_len, V7X_MXU_COLUMNS), BF16),
                        pltpu.VMEM((key_len, V7X_MXU_COLUMNS), BF16),
                        pltpu.VMEM((tq, key_len), F32),
                        pltpu.VMEM((tq, dv), F32)],
        compiler_params=_params(("arbitrary", "arbitrary")),
        name="attention",
    )(qd, qg, kd_t, vd, kg_t, vg, lam_p, subln)
    return od, od_last, og


def _fourier_kernel(c_ref, s_ref, xf_ref, o_ref, *, nf, out_scale):
    y = (jnp.dot(c_ref[...], xf_ref[0, :, :nf], preferred_element_type=F32)
         - jnp.dot(s_ref[...], xf_ref[0, :, nf:], preferred_element_type=F32))
    o_ref[0] = (y * out_scale).astype(BF16)


def _fourier_dense(tables, xf, *, tr=512):
    dft_c, dft_s = tables
    b, l, nf2 = xf.shape
    nf = nf2 // 2
    tr = min(tr, l)
    return pl.pallas_call(
        functools.partial(_fourier_kernel, nf=nf, out_scale=float(l) ** -0.5),
        grid=(b, l // tr),
        in_specs=[pl.BlockSpec((tr, l), lambda bi, i: (i, 0)),
                  pl.BlockSpec((tr, l), lambda bi, i: (i, 0)),
                  pl.BlockSpec((1, l, nf2), lambda bi, i: (bi, 0, 0))],
        out_specs=pl.BlockSpec((1, tr, nf), lambda bi, i: (bi, i, 0)),
        out_shape=jax.ShapeDtypeStruct((b, l, nf), BF16),
        compiler_params=_params(("arbitrary", "arbitrary")),
        name="fourier",
    )(dft_c, dft_s, xf)


def _fft_a_kernel(z_ref, m_ref, cos_ref, sin_ref, o_ref, *, nf):
    n1, r = z_ref.shape[1], z_ref.shape[2]
    rows = n1 * r
    z = z_ref[0].reshape(rows, 2 * nf)
    t = jnp.dot(m_ref[...], z, preferred_element_type=F32)
    cz, sz = t[:rows], t[rows:]
    ar = cz[:, :nf] - sz[:, nf:]
    ai = -(sz[:, :nf] + cz[:, nf:])
    ct = _tile_lanes(cos_ref[0], nf)
    st = _tile_lanes(sin_ref[0], nf)
    o_ref[0, :, :, :nf] = (ar * ct + ai * st).reshape(n1, r, nf).astype(BF16)
    o_ref[0, :, :, nf:] = (ai * ct - ar * st).reshape(n1, r, nf).astype(BF16)


def _fft_c_kernel(b_ref, mc_ref, ms_ref, o_ref, *, nf, out_scale):
    r, n2 = b_ref.shape[1], b_ref.shape[2]
    blk = b_ref[0].reshape(r * n2, 2 * nf)
    y = (jnp.dot(mc_ref[...], blk[:, :nf], preferred_element_type=F32)
         + jnp.dot(ms_ref[...], blk[:, nf:], preferred_element_type=F32))
    o_ref[0] = (y * out_scale).reshape(n2, r, nf).astype(BF16)


def _fourier_two_stage(tables, xf):
    m_a, tw_cos, tw_sin, m_c_cos, m_c_sin = tables
    b, l, nf2 = xf.shape
    nf = nf2 // 2
    r = BF16_SUBLANES
    n2 = m_c_cos.shape[0] // r
    n1 = l // n2
    tile = lambda w: pl.BlockSpec((1, n1, r, w), lambda bi, s: (bi, 0, s, 0))
    const = lambda a: pl.BlockSpec(a.shape, lambda bi, s: (0,) * a.ndim)
    twiddle = pl.BlockSpec((1, n1 * r, 2 * HEAD_DIM), lambda bi, s: (s, 0, 0))
    staged = pl.pallas_call(
        functools.partial(_fft_a_kernel, nf=nf),
        grid=(b, n2 // r),
        in_specs=[tile(nf2), const(m_a), twiddle, twiddle],
        out_specs=tile(nf2),
        out_shape=jax.ShapeDtypeStruct((b, n1, n2, nf2), BF16),
        compiler_params=_params(("arbitrary", "arbitrary")),
        name="fourier_a",
    )(xf.reshape(b, n1, n2, nf2), m_a, tw_cos, tw_sin)
    out = pl.pallas_call(
        functools.partial(_fft_c_kernel, nf=nf, out_scale=float(l) ** -0.5),
        grid=(b, n1 // r),
        in_specs=[pl.BlockSpec((1, r, n2, nf2), lambda bi, s: (bi, s, 0, 0)), const(m_c_cos), const(m_c_sin)],
        out_specs=pl.BlockSpec((1, n2, r, nf), lambda bi, s: (bi, 0, s, 0)),
        out_shape=jax.ShapeDtypeStruct((b, n2, n1, nf), BF16),
        compiler_params=_params(("arbitrary", "arbitrary")),
        name="fourier_c",
    )(staged, m_c_cos, m_c_sin)
    return out.reshape(b, l, nf)


def _merge_kernel(x_ref, shift_ref, scale_ref, gate_ref, g_ref, wgate_ref, yf_ref, od_ref, odl_ref, og_ref, yc_ref,
                  wf_ref, wd_ref, wg_ref, wc_ref, wout_ref, gpost_ref, o_ref):
    x = x_ref[0]
    d = x.shape[-1]
    h = _modulated(x, g_ref[...], shift_ref[0], scale_ref[0]).astype(BF16)
    diff = jnp.concatenate([od_ref[0], odl_ref[0]], axis=-1)
    branches = ((yf_ref[0], wf_ref), (diff, wd_ref), (og_ref[0], wg_ref), (yc_ref[0], wc_ref))
    merged = None
    for k, (y, w_ref) in enumerate(branches):
        gate = jax.nn.sigmoid(jnp.dot(h, wgate_ref[:, k * d:(k + 1) * d], preferred_element_type=F32))
        term = gate * jnp.dot(y, w_ref[...], preferred_element_type=F32)
        merged = term if merged is None else merged + term
    mix = jnp.dot(merged.astype(BF16), wout_ref[...], preferred_element_type=F32)
    o_ref[0] = x + gate_ref[0] * _rms(mix, gpost_ref[...])


def _merge(x, mod, g_pre, w_gate, yf, od, od_last, og, yc, wf, wd, wg, wc, wout, g_post, *, tm):
    b, l, d = x.shape
    tm = min(tm, l)
    tok = lambda w: pl.BlockSpec((1, tm, w), lambda bi, i: (bi, i, 0))
    modc = lambda k: pl.BlockSpec((1, 1, d), lambda bi, i: (bi, 0, k))
    const = lambda a: pl.BlockSpec(a.shape, lambda bi, i: (0,) * a.ndim)
    return pl.pallas_call(
        _merge_kernel,
        grid=(b, l // tm),
        in_specs=[tok(d), modc(0), modc(1), modc(2), const(g_pre), const(w_gate),
                  tok(yf.shape[-1]), tok(od.shape[-1]), tok(od_last.shape[-1]), tok(og.shape[-1]), tok(yc.shape[-1]),
                  const(wf), const(wd), const(wg), const(wc), const(wout), const(g_post)],
        out_specs=tok(d),
        out_shape=jax.ShapeDtypeStruct((b, l, d), F32),
        compiler_params=_params(("arbitrary", "arbitrary")),
        name="merge",
    )(x, mod, mod, mod, g_pre, w_gate, yf, od, od_last, og, yc, wf, wd, wg, wc, wout, g_post)


def _mlp_kernel(x_ref, shift_ref, scale_ref, gate_ref, g_ref, w1_ref, w2_ref, gpost_ref, o_ref, *, ff_chunk):
    x = x_ref[0]
    h = _modulated(x, g_ref[...], shift_ref[0], scale_ref[0]).astype(BF16)
    acc = None
    for c0 in range(0, w1_ref.shape[-1], ff_chunk):
        u = jnp.maximum(jnp.dot(h, w1_ref[:, c0:c0 + ff_chunk], preferred_element_type=F32), 0.0)
        part = jnp.dot((u * u).astype(BF16), w2_ref[c0:c0 + ff_chunk, :], preferred_element_type=F32)
        acc = part if acc is None else acc + part
    o_ref[0] = x + gate_ref[0] * _rms(acc, gpost_ref[...])


def _mlp(x, mod, g_pre, w1, w2, g_post, *, tm):
    b, l, d = x.shape
    tm = min(tm, l)
    tok = lambda w: pl.BlockSpec((1, tm, w), lambda bi, i: (bi, i, 0))
    modc = lambda k: pl.BlockSpec((1, 1, d), lambda bi, i: (bi, 0, k))
    const = lambda a: pl.BlockSpec(a.shape, lambda bi, i: (0,) * a.ndim)
    return pl.pallas_call(
        functools.partial(_mlp_kernel, ff_chunk=1024),
        grid=(b, l // tm),
        in_specs=[tok(d), modc(3), modc(4), modc(5), const(g_pre), const(w1), const(w2), const(g_post)],
        out_specs=tok(d),
        out_shape=jax.ShapeDtypeStruct((b, l, d), F32),
        compiler_params=_params(("arbitrary", "arbitrary")),
        name="mlp",
    )(x, mod, mod, mod, g_pre, w1, w2, g_post)


def _angles(i, j, period):
    return (np.outer(i, j) % period).astype(np.float64) * (2.0 * np.pi / period)


def _mxu_table(values):
    return jnp.asarray(values, F32).astype(BF16)


def _rope_tables(seq_len):
    rows = seq_len // GRID_W
    row = np.repeat(np.arange(rows, dtype=np.float64), GRID_W)
    col = np.tile(np.arange(GRID_W, dtype=np.float64), rows)
    inv_freq = ROPE_THETA ** (-np.arange(ROPE_FREQS, dtype=np.float64) / ROPE_FREQS)
    ang_r, ang_c = row[:, None] * inv_freq, col[:, None] * inv_freq
    cos64 = np.concatenate([np.cos(ang_r)] * 2 + [np.cos(ang_c)] * 2, axis=-1)
    sin64 = np.concatenate([-np.sin(ang_r), np.sin(ang_r), -np.sin(ang_c), np.sin(ang_c)], axis=-1)
    return (jnp.asarray(np.concatenate([cos64] * 2, axis=-1), F32),
            jnp.asarray(np.concatenate([sin64] * 2, axis=-1), F32))


def _seq_dft(l):
    th = _angles(np.arange(l), np.arange(l), l)
    return _mxu_table(np.cos(th)), _mxu_table(np.sin(th))


def _two_stage_dft(l, n2):
    n1 = l // n2
    r = BF16_SUBLANES
    i1, i2 = np.arange(n1), np.arange(n2)
    eye = np.eye(r)
    th_a = _angles(i1, i1, n1)
    m_a = np.concatenate([np.kron(np.cos(th_a), eye), np.kron(np.sin(th_a), eye)], axis=0)
    th_t = _angles(i1, i2, l).reshape(n1, n2 // r, r)
    th_t = np.transpose(th_t, (1, 0, 2)).reshape(n2 // r, n1 * r)
    lanes = lambda t: jnp.asarray(np.broadcast_to(t[:, :, None], t.shape + (2 * HEAD_DIM,)), F32)
    th_c = _angles(i2, i2, n2)
    perm = lambda w: _mxu_table(np.einsum('kb,ij->kijb', w, eye).reshape(n2 * r, r * n2))
    return _mxu_table(m_a), lanes(np.cos(th_t)), lanes(np.sin(th_t)), perm(np.cos(th_c)), perm(np.sin(th_c))


def _channel_dft(n_groups):
    g = FOURIER_GROUP_DIM
    th = _angles(np.arange(g), np.arange(g), g)
    eye = np.eye(n_groups)
    scale = float(g) ** -0.5
    return _mxu_table(np.concatenate([np.kron(eye, np.cos(th) * scale), np.kron(eye, np.sin(th) * scale)], axis=-1))


def _seq_fourier(l):
    n2 = FOURIER_STAGE_LEN
    if l % n2 == 0 and (l // n2) % BF16_SUBLANES == 0:
        return functools.partial(_fourier_two_stage, _two_stage_dft(l, n2))
    return functools.partial(_fourier_dense, _seq_dft(l))


def _mix_and_mlp(x, mod, lw, tok, kv_bufs, fourier, *, dims, lam_init, key_start, key_len):
    qd, qg, xf, vc = tok
    od, od_last, og = _attention(qd, qg, kv_bufs, lw["lam_p"], lw["subln"], dims=dims, lam_init=lam_init,
                                 tq=TILE_ROWS["attention"], key_start=key_start, key_len=key_len)
    yf = fourier(xf)
    yc = _conv(vc, lw["conv_w"], lw["conv_b"], lw["conv_ln_g"], lw["conv_ln_b"], rows=TILE_ROWS["conv"])
    x = _merge(x, mod, lw["g_pre_mix"], lw["w_gate"], yf, od, od_last, og, yc, lw["w_br_f"], lw["w_br_d"],
               lw["w_br_g"], lw["w_br_c"], lw["w_out"], lw["g_post_mix"], tm=TILE_ROWS["merge"])
    return _mlp(x, mod, lw["g_pre_mlp"], lw["w_ff1"], lw["w_ff2"], lw["g_post_mlp"], tm=TILE_ROWS["mlp"])


def kernel(x, c, ctx, c_ctx, w_mod, b_mod, g_pre_mix, g_post_mix, g_pre_mlp, g_post_mlp, w_in, q_norm, k_norm,
           diff_lambda, diff_subln, conv_dw, conv_dw_bias, conv_ln_g, conv_ln_b, w_br_fourier, w_br_diff,
           w_br_gqa, w_br_conv, w_out, w_ff1, w_ff2):
    b, l, d = x.shape
    lc = ctx.shape[1]
    lk = l + lc
    depth = w_in.shape[0]
    dims = _dims(d)
    assert b + 1 <= MOD_ROWS and l % GRID_W == 0 and l % lc == 0

    rope_x = _rope_tables(l)
    rope_none = (jnp.ones((lc, 2 * HEAD_DIM), F32), jnp.zeros((lc, 2 * HEAD_DIM), F32))
    fourier_x, fourier_c = _seq_fourier(l), _seq_fourier(lc)
    n_groups = dims["fourier"] // FOURIER_GROUP_DIM
    heads = np.arange(dims["diff_qk"]) // HEAD_DIM
    ones_bd = jnp.asarray((heads[:, None] == heads[None, :]).astype(BF16))
    dft_ch = _channel_dft(n_groups)

    c_rows = jnp.zeros((MOD_ROWS, d), F32).at[:b].set(c).at[b].set(c_ctx)
    mod_all = _modulation(c_rows, w_mod, b_mod)

    n_tok_cols = w_in.shape[-1] - N_BRANCHES * d

    kv_bufs = (jnp.zeros((b, dims["diff_qk"], lk), BF16), jnp.zeros((b, dims["gqa_kv"], lk), BF16),
               jnp.zeros((b, lk, dims["diff_v"]), BF16), jnp.zeros((b, lk, dims["gqa_kv"]), BF16))
    xc = ctx
    for li in range(depth):
        lam_init = 0.8 - 0.6 * math.exp(-0.3 * li)
        row = lambda a: a[li].reshape(1, -1)
        lw = dict(
            w_gate=lax.optimization_barrier(w_in[li, :, n_tok_cols:]).astype(BF16),
            g_pre_mix=row(g_pre_mix), g_post_mix=row(g_post_mix), g_pre_mlp=row(g_pre_mlp),
            g_post_mlp=row(g_post_mlp),
            qn=jnp.tile(q_norm[li], dims["gqa_q"] // HEAD_DIM).reshape(1, -1),
            kn=jnp.tile(k_norm[li], dims["gqa_kv"] // HEAD_DIM).reshape(1, -1),
            lam_p=diff_lambda[li], subln=row(diff_subln),
            conv_w=conv_dw[li], conv_b=row(conv_dw_bias), conv_ln_g=row(conv_ln_g), conv_ln_b=row(conv_ln_b),
            w_br_f=w_br_fourier[li].astype(BF16), w_br_d=w_br_diff[li].astype(BF16),
            w_br_g=w_br_gqa[li].astype(BF16), w_br_c=w_br_conv[li].astype(BF16), w_out=w_out[li].astype(BF16),
            w_ff1=w_ff1[li].astype(BF16), w_ff2=w_ff2[li].astype(BF16))
        mod_x = mod_all[li, :b].reshape(b, 1, N_MOD * d)
        mod_c = jnp.broadcast_to(mod_all[li, b].reshape(1, 1, N_MOD * d), (b, 1, N_MOD * d))
        last = li == depth - 1

        inproj = functools.partial(_inproj, g_pre=lw["g_pre_mix"], w_in=w_in, ones_bd=ones_bd, qn=lw["qn"],
                                   kn=lw["kn"], dft_ch=dft_ch, dims=dims, layer=li, n_tok_cols=n_tok_cols)
        tok_c, kv_bufs = inproj(xc, mod_c, cos_t=rope_none[0], sin_t=rope_none[1], kv_bufs=kv_bufs, kv_only=last,
                                row_start=l, tm=TILE_ROWS["inproj"])
        tok_x, kv_bufs = inproj(x, mod_x, cos_t=rope_x[0], sin_t=rope_x[1], kv_bufs=kv_bufs, kv_only=False,
                                row_start=0, tm=TILE_ROWS["inproj"])
        mix = functools.partial(_mix_and_mlp, lw=lw, kv_bufs=kv_bufs, dims=dims, lam_init=lam_init)
        x = mix(x, mod_x, tok=tok_x, fourier=fourier_x, key_start=0, key_len=lk)
        if not last:
            xc = mix(xc, mod_c, tok=tok_c, fourier=fourier_c, key_start=l, key_len=lc)
    return x
```

```python
import functools
import math

import jax
import jax.numpy as jnp
import numpy as np
from jax import lax
from jax.experimental import pallas as pl
from jax.experimental.pallas import tpu as pltpu

F32 = jnp.float32
BF16 = jnp.bfloat16

GRID_W = 64
HEAD_DIM = 64
ROPE_FREQS = HEAD_DIM // 4
ROPE_THETA = 10000.0
EPS = 1e-6
ATTN_SCALE = HEAD_DIM ** -0.5
Q_PRESCALE = ATTN_SCALE * math.log2(math.e)
FOURIER_GROUP_DIM = 64
CONV_KERNEL = 31
CONV_HALO = 16
N_BRANCHES = 4
N_MOD = 6
MOD_ROWS = 8

V7X_VMEM_LIMIT_BYTES = 56 * 1024 * 1024
V7X_MXU_COLUMNS = 256
SUBLANES = 8
BF16_SUBLANES = 16
FOURIER_STAGE_LEN = 64

TILE_ROWS = dict(inproj=512, attention=256, conv=512, merge=512, mlp=512)


def _dims(d_model):
    fourier = 3 * d_model // 8
    diff_heads = d_model // 256
    diff_qk = diff_heads * 2 * HEAD_DIM
    diff_v = diff_heads * 2 * HEAD_DIM
    gqa_q = (d_model // 128) * HEAD_DIM
    gqa_kv = (d_model // 128 // 4) * HEAD_DIM
    conv = 3 * d_model // 8
    return dict(fourier=fourier, diff_heads=diff_heads, diff_qk=diff_qk, diff_v=diff_v,
                gqa_q=gqa_q, gqa_kv=gqa_kv, conv=conv)


def _params(semantics):
    return pltpu.CompilerParams(dimension_semantics=semantics, vmem_limit_bytes=V7X_VMEM_LIMIT_BYTES)


def _mod_kernel(c_ref, w_ref, b_ref, o_ref):
    c = c_ref[...]
    a = c * jax.nn.sigmoid(c)
    o_ref[...] = jnp.dot(a, w_ref[...], preferred_element_type=F32,
                         precision=lax.Precision.HIGHEST) + b_ref[...]


def _modulation(c_rows, w_mod, b_mod):
    depth, d, n = w_mod.shape
    tn = 2 * d
    return pl.pallas_call(
        _mod_kernel,
        grid=(depth, n // tn),
        in_specs=[pl.BlockSpec((MOD_ROWS, d), lambda l, j: (0, 0)),
                  pl.BlockSpec((None, d, tn), lambda l, j: (l, 0, j)),
                  pl.BlockSpec((None, 1, tn), lambda l, j: (l, 0, j))],
        out_specs=pl.BlockSpec((None, MOD_ROWS, tn), lambda l, j: (l, 0, j)),
        out_shape=jax.ShapeDtypeStruct((depth, MOD_ROWS, n), F32),
        compiler_params=_params(("arbitrary", "arbitrary")),
        name="modulation",
    )(c_rows, w_mod, b_mod.reshape(depth, 1, n))


def _modulated(x, g, shift, scale):
    h = x * lax.rsqrt(jnp.mean(x * x, axis=-1, keepdims=True) + EPS) * g
    return h * (1.0 + scale) + shift


def _rms(x, g):
    return x * lax.rsqrt(jnp.mean(x * x, axis=-1, keepdims=True) + EPS) * g


def _swap_halves(t):
    w = t.shape[-1]
    lane = lax.broadcasted_iota(jnp.int32, t.shape, 1)
    from_right = pltpu.roll(t, w - ROPE_FREQS, 1)
    from_left = pltpu.roll(t, ROPE_FREQS, 1)
    return jnp.where((lane % (2 * ROPE_FREQS)) < ROPE_FREQS, from_right, from_left)


def _tile_lanes(t, width):
    reps = width // t.shape[-1]
    return t if reps == 1 else jnp.concatenate([t] * reps, axis=-1)


def _head_norm(t, ones_bd, gain):
    w = t.shape[-1]
    t2 = t * t
    hi = t2.astype(BF16)
    lo = (t2 - hi.astype(F32)).astype(BF16)
    e = ones_bd[:w, :w]
    ss = (jnp.dot(hi, e, preferred_element_type=F32) + jnp.dot(lo, e, preferred_element_type=F32))
    return t * lax.rsqrt(ss * (1.0 / HEAD_DIM) + EPS) * gain


def _inproj_kernel(x_ref, shift_ref, scale_ref, g_ref, w_ref, ones_ref, cos_ref, sin_ref, qn_ref, kn_ref,
                   dft_ref, *rest, dims, kv_only, n_alias):
    out_refs = rest[n_alias:]
    nq, nkv, nv, nf, nc = dims["diff_qk"], dims["gqa_kv"], dims["diff_v"], dims["fourier"], dims["conv"]
    ngq = dims["gqa_q"]
    h = _modulated(x_ref[0], g_ref[...], shift_ref[0], scale_ref[0]).astype(BF16)

    def proj(start, width):
        return jnp.dot(h, w_ref[:, start:start + width].astype(BF16), preferred_element_type=F32)

    def rope(t):
        w = t.shape[-1]
        return t * _tile_lanes(cos_ref[...], w) + _swap_halves(t) * _tile_lanes(sin_ref[...], w)

    ones_bd = ones_ref[...]
    off = 0
    dk = proj(off, nq); off += nq
    dv = proj(off, nv); off += nv
    gk = proj(off, nkv); off += nkv
    gv = proj(off, nkv); off += nkv
    kdt_ref, kgt_ref, vd_ref, vg_ref = out_refs[-4:]
    kdt_ref[0] = rope(dk).T.astype(BF16)
    kgt_ref[0] = rope(_head_norm(gk, ones_bd, kn_ref[...])).T.astype(BF16)
    vd_ref[0] = dv.astype(BF16)
    vg_ref[0] = gv.astype(BF16)
    if not kv_only:
        qd_ref, qg_ref, xf_ref, vc_ref = out_refs[:4]
        uf = proj(off, nf).astype(BF16); off += nf
        xf_ref[0] = jnp.dot(uf, dft_ref[...], preferred_element_type=F32).astype(BF16)
        dq = proj(off, nq); off += nq
        qd_ref[0] = (rope(dq) * Q_PRESCALE).astype(BF16)
        gq = proj(off, ngq); off += ngq
        qg_ref[0] = (rope(_head_norm(gq, ones_bd, qn_ref[...])) * Q_PRESCALE).astype(BF16)
        glu = proj(off, 2 * nc)
        vc_ref[0] = glu[:, :nc] * jax.nn.sigmoid(glu[:, nc:])


def _inproj(x, mod, g_pre, w_in, ones_bd, cos_t, sin_t, qn, kn, dft_ch, kv_bufs, *, dims, layer, n_tok_cols,
            kv_only, row_start, tm):
    b, l, d = x.shape
    tm = min(tm, l)
    nq, nkv, nv, nf, nc, ngq = (dims["diff_qk"], dims["gqa_kv"], dims["diff_v"], dims["fourier"],
                                dims["conv"], dims["gqa_q"])
    blk0 = row_start // tm
    assert row_start % tm == 0
    tok = lambda w: pl.BlockSpec((1, tm, w), lambda bi, i: (bi, i, 0))
    const = lambda shape: pl.BlockSpec(shape, lambda bi, i: (0,) * len(shape))
    tok_outs = [] if kv_only else [((b, l, nq), BF16), ((b, l, ngq), BF16), ((b, l, 2 * nf), BF16), ((b, l, nc), F32)]
    kv_shapes = [(a.shape, a.dtype) for a in kv_bufs]
    kv_specs = [pl.BlockSpec((1, nq, tm), lambda bi, i: (bi, 0, blk0 + i)),
                pl.BlockSpec((1, nkv, tm), lambda bi, i: (bi, 0, blk0 + i)),
                pl.BlockSpec((1, tm, nv), lambda bi, i: (bi, blk0 + i, 0)),
                pl.BlockSpec((1, tm, nkv), lambda bi, i: (bi, blk0 + i, 0))]
    n_alias = len(kv_bufs)
    n_in = 11
    aliases = {n_in + k: len(tok_outs) + k for k in range(n_alias)}
    outs = pl.pallas_call(
        functools.partial(_inproj_kernel, dims=dims, kv_only=kv_only, n_alias=n_alias),
        grid=(b, l // tm),
        in_specs=[tok(d),
                  pl.BlockSpec((1, 1, d), lambda bi, i: (bi, 0, 0)),
                  pl.BlockSpec((1, 1, d), lambda bi, i: (bi, 0, 1)),
                  const((1, d)),
                  pl.BlockSpec((None, d, n_tok_cols), lambda bi, i: (layer, 0, 0)),
                  const(ones_bd.shape),
                  pl.BlockSpec((tm, 2 * HEAD_DIM), lambda bi, i: (i, 0)),
                  pl.BlockSpec((tm, 2 * HEAD_DIM), lambda bi, i: (i, 0)),
                  const(qn.shape), const(kn.shape), const(dft_ch.shape)]
                 + [pl.BlockSpec(memory_space=pl.ANY)] * n_alias,
        out_specs=[tok(s[-1]) for s, _ in tok_outs] + kv_specs,
        out_shape=[jax.ShapeDtypeStruct(s, dt) for s, dt in tok_outs + kv_shapes],
        input_output_aliases=aliases,
        compiler_params=_params(("arbitrary", "arbitrary")),
        name="inproj_kv" if kv_only else "inproj",
    )(x, mod, mod, g_pre, w_in, ones_bd, cos_t, sin_t, qn, kn, dft_ch, *kv_bufs)
    return outs[:len(tok_outs)], outs[len(tok_outs):]


def _conv_ln_silu(win_ref, w_ref, b_ref, g_ref, beta_ref, yc_ref, *, rows):
    first = CONV_HALO - CONV_KERNEL // 2
    acc = jnp.zeros((rows, w_ref.shape[-1]), F32) + b_ref[...]
    for phase in range(SUBLANES):
        part = None
        for t in range(CONV_KERNEL):
            if (t + first) % SUBLANES != phase:
                continue
            base = t + first - phase
            term = w_ref[t:t + 1, :] * win_ref[base:base + rows + SUBLANES, :]
            part = term if part is None else part + term
        acc = acc + part[phase:phase + rows, :]
    mu = jnp.mean(acc, axis=-1, keepdims=True)
    cen = acc - mu
    var = jnp.mean(cen * cen, axis=-1, keepdims=True)
    y = cen * lax.rsqrt(var + EPS) * g_ref[...] + beta_ref[...]
    yc_ref[...] = (y * jax.nn.sigmoid(y)).astype(BF16)


def _conv_kernel(vprev_ref, vcur_ref, vnext_ref, cw_ref, cb_ref, cg_ref, cbeta_ref, yc_ref, win_ref):
    i = pl.program_id(1)
    rows = vcur_ref.shape[1]
    win_ref[:CONV_HALO, :] = jnp.where(i > 0, vprev_ref[0], 0.0)
    win_ref[CONV_HALO:CONV_HALO + rows, :] = vcur_ref[0]
    win_ref[CONV_HALO + rows:, :] = jnp.where(i < pl.num_programs(1) - 1, vnext_ref[0], 0.0)
    _conv_ln_silu(win_ref, cw_ref, cb_ref, cg_ref, cbeta_ref, yc_ref.at[0], rows=rows)


def _conv(vc, conv_w, conv_b, conv_g, conv_beta, *, rows):
    b, l, c = vc.shape
    rows = min(rows, l)
    hb = rows // CONV_HALO
    n_hb = l // CONV_HALO
    const = lambda a: pl.BlockSpec(a.shape, lambda bi, i: (0,) * a.ndim)
    prev = pl.BlockSpec((1, CONV_HALO, c), lambda bi, i: (bi, jnp.maximum(i * hb - 1, 0), 0))
    nxt = pl.BlockSpec((1, CONV_HALO, c), lambda bi, i: (bi, jnp.minimum((i + 1) * hb, n_hb - 1), 0))
    tile = pl.BlockSpec((1, rows, c), lambda bi, i: (bi, i, 0))
    return pl.pallas_call(
        _conv_kernel,
        grid=(b, l // rows),
        in_specs=[prev, tile, nxt, const(conv_w), const(conv_b), const(conv_g), const(conv_beta)],
        out_specs=tile,
        out_shape=jax.ShapeDtypeStruct((b, l, c), BF16),
        scratch_shapes=[pltpu.VMEM((rows + 2 * CONV_HALO, c), F32)],
        compiler_params=_params(("arbitrary", "arbitrary")),
        name="conv",
    )(vc, vc, vc, conv_w, conv_b, conv_g, conv_beta)


def _softmax_pv(q, k_t, v_ext):
    s = jnp.dot(q, k_t, preferred_element_type=F32)
    e = jnp.exp2(s - jnp.max(s, axis=-1, keepdims=True))
    return jnp.dot(e.astype(BF16), v_ext, preferred_element_type=F32)


def _softmax_terms(q, k_t):
    s = jnp.dot(q, k_t, preferred_element_type=F32)
    e = jnp.exp2(s - jnp.max(s, axis=-1, keepdims=True))
    return e, jnp.sum(e, axis=-1, keepdims=True)


def _attn_kernel(qd_ref, qg_ref, kdt_ref, vd_ref, kgt_ref, vg_ref, lam_ref, subln_ref, od_ref, og_ref, odl_ref,
                 vdx_ref, vgx_ref, s_ref, m0_ref, *, n_diff, n_gq, n_kv, lam_init):
    dh = HEAD_DIM
    dv = 2 * HEAD_DIM
    i = pl.program_id(1)
    tq = qd_ref.shape[1]
    last = n_diff - 1

    @pl.when(i == 0)
    def _():
        for hd in range(n_diff):
            vdx_ref[hd, :, :dv] = vd_ref[0, :, hd * dv:(hd + 1) * dv]
            vdx_ref[hd, :, dv:] = jnp.ones((vd_ref.shape[1], V7X_MXU_COLUMNS - dv), BF16)
        vgx_ref[:, :n_kv * dh] = vg_ref[0]
        vgx_ref[:, n_kv * dh:] = jnp.ones((vg_ref.shape[1], V7X_MXU_COLUMNS - n_kv * dh), BF16)
        s_ref[...] = jnp.zeros(s_ref.shape, F32)
        m0_ref[...] = jnp.zeros(m0_ref.shape, F32)

    lv = lam_ref[...]
    lam = (jnp.exp(jnp.sum(lv[0:1] * lv[1:2], axis=-1, keepdims=True))
           - jnp.exp(jnp.sum(lv[2:3] * lv[3:4], axis=-1, keepdims=True)) + lam_init)

    def diff_head_out(m0, pv1):
        o = m0 - lam * (pv1[:, :dv] / pv1[:, dv:])
        return (_rms(o, subln_ref[...]) * (1.0 - lam_init)).astype(BF16)

    def finish_last_diff_head(row0):
        s = s_ref[...]
        e = jnp.exp2(s - jnp.max(s, axis=-1, keepdims=True))
        pv1 = jnp.dot(e.astype(BF16), vdx_ref[last], preferred_element_type=F32)
        odl_ref[0, pl.ds(row0, tq), :] = diff_head_out(m0_ref[...], pv1)

    finish_last_diff_head(pl.multiple_of(jnp.maximum(i - 1, 0) * tq, tq))
    for hd in range(last):
        c = 2 * hd * dh
        e0, l0 = _softmax_terms(qd_ref[0, :, c:c + dh], kdt_ref[0, c:c + dh, :])
        e1, l1 = _softmax_terms(qd_ref[0, :, c + dh:c + 2 * dh], kdt_ref[0, c + dh:c + 2 * dh, :])
        a = e0 * (1.0 / l0) - e1 * (lam / l1)
        pv = jnp.dot(a.astype(BF16), vdx_ref[hd], preferred_element_type=F32)
        od_ref[0, :, hd * dv:(hd + 1) * dv] = (_rms(pv[:, :dv], subln_ref[...]) * (1.0 - lam_init)).astype(BF16)
    group = n_gq // n_kv
    lane = lax.broadcasted_iota(jnp.int32, (qg_ref.shape[1], dv), 1)
    for pair in range(n_gq // 2):
        kv = (2 * pair) // group
        halves = []
        for hq in (2 * pair, 2 * pair + 1):
            pv = _softmax_pv(qg_ref[0, :, hq * dh:(hq + 1) * dh], kgt_ref[0, kv * dh:(kv + 1) * dh, :], vgx_ref[...])
            halves.append(pv[:, :dv] / pv[:, dv:])
        left = halves[0] if kv % 2 == 0 else pltpu.roll(halves[0], dh, 1)
        right = pltpu.roll(halves[1], dh, 1) if kv % 2 == 0 else halves[1]
        og_ref[0, :, pair * dv:(pair + 1) * dv] = jnp.where(lane < dh, left, right).astype(BF16)

    c = 2 * last * dh
    pv0 = _softmax_pv(qd_ref[0, :, c:c + dh], kdt_ref[0, c:c + dh, :], vdx_ref[last])
    m0_ref[...] = pv0[:, :dv] / pv0[:, dv:]
    s_ref[...] = jnp.dot(qd_ref[0, :, c + dh:c + 2 * dh], kdt_ref[0, c + dh:c + 2 * dh, :],
                         preferred_element_type=F32)

    @pl.when(i == pl.num_programs(1) - 1)
    def _():
        finish_last_diff_head(pl.multiple_of(i * tq, tq))


def _attention(qd, qg, kv_bufs, lam_p, subln, *, dims, lam_init, tq, key_start, key_len):
    kd_t, kg_t, vd, vg = kv_bufs
    b, l, nq = qd.shape
    n_diff, n_gq, n_kv = dims["diff_heads"], dims["gqa_q"] // HEAD_DIM, dims["gqa_kv"] // HEAD_DIM
    assert n_kv * HEAD_DIM == 2 * HEAD_DIM and (n_gq // n_kv) % 2 == 0 and key_start % key_len == 0
    kb = key_start // key_len
    tq = min(tq, l)
    tok = lambda w: pl.BlockSpec((1, tq, w), lambda bi, i: (bi, i, 0))
    keys_t = lambda a: pl.BlockSpec((1, a.shape[1], key_len), lambda bi, i: (bi, 0, kb))
    vals = lambda a: pl.BlockSpec((1, key_len, a.shape[2]), lambda bi, i: (bi, kb, 0))
    const = lambda a: pl.BlockSpec(a.shape, lambda bi, i: (0,) * a.ndim)
    dv = 2 * HEAD_DIM
    od, og, od_last = pl.pallas_call(
        functools.partial(_attn_kernel, n_diff=n_diff, n_gq=n_gq, n_kv=n_kv, lam_init=lam_init),
        grid=(b, l // tq),
        in_specs=[tok(nq), tok(qg.shape[-1]), keys_t(kd_t), vals(vd), keys_t(kg_t), vals(vg),
                  const(lam_p), const(subln)],
        out_specs=[tok((n_diff - 1) * dv), tok(dims["gqa_q"]), pl.BlockSpec((1, l, dv), lambda bi, i: (bi, 0, 0))],
        out_shape=[jax.ShapeDtypeStruct((b, l, (n_diff - 1) * dv), BF16),
                   jax.ShapeDtypeStruct((b, l, dims["gqa_q"]), BF16),
                   jax.ShapeDtypeStruct((b, l, dv), BF16)],
        scratch_shapes=[pltpu.VMEM((n_diff, key_len, V7X_MXU_COLUMNS), BF16),
                        pltpu.VMEM((key_len, V7X_MXU_COLUMNS), BF16),
                        pltpu.VMEM((tq, key_len), F32),
                        pltpu.VMEM((tq, dv), F32)],
        compiler_params=_params(("arbitrary", "arbitrary")),
        name="attention",
    )(qd, qg, kd_t, vd, kg_t, vg, lam_p, subln)
    return od, od_last, og


def _fourier_kernel(c_ref, s_ref, xf_ref, o_ref, *, nf, out_scale):
    y = (jnp.dot(c_ref[...], xf_ref[0, :, :nf], preferred_element_type=F32)
         - jnp.dot(s_ref[...], xf_ref[0, :, nf:], preferred_element_type=F32))
    o_ref[0] = (y * out_scale).astype(BF16)


def _fourier_dense(tables, xf, *, tr=512):
    dft_c, dft_s = tables
    b, l, nf2 = xf.shape
    nf = nf2 // 2
    tr = min(tr, l)
    return pl.pallas_call(
        functools.partial(_fourier_kernel, nf=nf, out_scale=float(l) ** -0.5),
        grid=(b, l // tr),
        in_specs=[pl.BlockSpec((tr, l), lambda bi, i: (i, 0)),
                  pl.BlockSpec((tr, l), lambda bi, i: (i, 0)),
                  pl.BlockSpec((1, l, nf2), lambda bi, i: (bi, 0, 0))],
        out_specs=pl.BlockSpec((1, tr, nf), lambda bi, i: (bi, i, 0)),
        out_shape=jax.ShapeDtypeStruct((b, l, nf), BF16),
        compiler_params=_params(("arbitrary", "arbitrary")),
        name="fourier",
    )(dft_c, dft_s, xf)


def _fft_a_kernel(z_ref, m_ref, cos_ref, sin_ref, o_ref, *, nf):
    n1, r = z_ref.shape[1], z_ref.shape[2]
    rows = n1 * r
    z = z_ref[0].reshape(rows, 2 * nf)
    t = jnp.dot(m_ref[...], z, preferred_element_type=F32)
    cz, sz = t[:rows], t[rows:]
    ar = cz[:, :nf] - sz[:, nf:]
    ai = -(sz[:, :nf] + cz[:, nf:])
    ct = _tile_lanes(cos_ref[0], nf)
    st = _tile_lanes(sin_ref[0], nf)
    o_ref[0, :, :, :nf] = (ar * ct + ai * st).reshape(n1, r, nf).astype(BF16)
    o_ref[0, :, :, nf:] = (ai * ct - ar * st).reshape(n1, r, nf).astype(BF16)


def _fft_c_kernel(b_ref, mc_ref, ms_ref, o_ref, *, nf, out_scale):
    r, n2 = b_ref.shape[1], b_ref.shape[2]
    blk = b_ref[0].reshape(r * n2, 2 * nf)
    y = (jnp.dot(mc_ref[...], blk[:, :nf], preferred_element_type=F32)
         + jnp.dot(ms_ref[...], blk[:, nf:], preferred_element_type=F32))
    o_ref[0] = (y * out_scale).reshape(n2, r, nf).astype(BF16)


def _fourier_two_stage(tables, xf):
    m_a, tw_cos, tw_sin, m_c_cos, m_c_sin = tables
    b, l, nf2 = xf.shape
    nf = nf2 // 2
    r = BF16_SUBLANES
    n2 = m_c_cos.shape[0] // r
    n1 = l // n2
    tile = lambda w: pl.BlockSpec((1, n1, r, w), lambda bi, s: (bi, 0, s, 0))
    const = lambda a: pl.BlockSpec(a.shape, lambda bi, s: (0,) * a.ndim)
    twiddle = pl.BlockSpec((1, n1 * r, 2 * HEAD_DIM), lambda bi, s: (s, 0, 0))
    staged = pl.pallas_call(
        functools.partial(_fft_a_kernel, nf=nf),
        grid=(b, n2 // r),
        in_specs=[tile(nf2), const(m_a), twiddle, twiddle],
        out_specs=tile(nf2),
        out_shape=jax.ShapeDtypeStruct((b, n1, n2, nf2), BF16),
        compiler_params=_params(("arbitrary", "arbitrary")),
        name="fourier_a",
    )(xf.reshape(b, n1, n2, nf2), m_a, tw_cos, tw_sin)
    out = pl.pallas_call(
        functools.partial(_fft_c_kernel, nf=nf, out_scale=float(l) ** -0.5),
        grid=(b, n1 // r),
        in_specs=[pl.BlockSpec((1, r, n2, nf2), lambda bi, s: (bi, s, 0, 0)), const(m_c_cos), const(m_c_sin)],
        out_specs=pl.BlockSpec((1, n2, r, nf), lambda bi, s: (bi, 0, s, 0)),
        out_shape=jax.ShapeDtypeStruct((b, n2, n1, nf), BF16),
        compiler_params=_params(("arbitrary", "arbitrary")),
        name="fourier_c",
    )(staged, m_c_cos, m_c_sin)
    return out.reshape(b, l, nf)


def _merge_kernel(x_ref, shift_ref, scale_ref, gate_ref, g_ref, wgate_ref, yf_ref, od_ref, odl_ref, og_ref, yc_ref,
                  wf_ref, wd_ref, wg_ref, wc_ref, wout_ref, gpost_ref, o_ref):
    x = x_ref[0]
    d = x.shape[-1]
    h = _modulated(x, g_ref[...], shift_ref[0], scale_ref[0]).astype(BF16)
    diff = jnp.concatenate([od_ref[0], odl_ref[0]], axis=-1)
    branches = ((yf_ref[0], wf_ref), (diff, wd_ref), (og_ref[0], wg_ref), (yc_ref[0], wc_ref))
    merged = None
    for k, (y, w_ref) in enumerate(branches):
        gate = jax.nn.sigmoid(jnp.dot(h, wgate_ref[:, k * d:(k + 1) * d], preferred_element_type=F32))
        term = gate * jnp.dot(y, w_ref[...], preferred_element_type=F32)
        merged = term if merged is None else merged + term
    mix = jnp.dot(merged.astype(BF16), wout_ref[...], preferred_element_type=F32)
    o_ref[0] = x + gate_ref[0] * _rms(mix, gpost_ref[...])


def _merge(x, mod, g_pre, w_gate, yf, od, od_last, og, yc, wf, wd, wg, wc, wout, g_post, *, tm):
    b, l, d = x.shape
    tm = min(tm, l)
    tok = lambda w: pl.BlockSpec((1, tm, w), lambda bi, i: (bi, i, 0))
    modc = lambda k: pl.BlockSpec((1, 1, d), lambda bi, i: (bi, 0, k))
    const = lambda a: pl.BlockSpec(a.shape, lambda bi, i: (0,) * a.ndim)
    return pl.pallas_call(
        _merge_kernel,
        grid=(b, l // tm),
        in_specs=[tok(d), modc(0), modc(1), modc(2), const(g_pre), const(w_gate),
                  tok(yf.shape[-1]), tok(od.shape[-1]), tok(od_last.shape[-1]), tok(og.shape[-1]), tok(yc.shape[-1]),
                  const(wf), const(wd), const(wg), const(wc), const(wout), const(g_post)],
        out_specs=tok(d),
        out_shape=jax.ShapeDtypeStruct((b, l, d), F32),
        compiler_params=_params(("arbitrary", "arbitrary")),
        name="merge",
    )(x, mod, mod, mod, g_pre, w_gate, yf, od, od_last, og, yc, wf, wd, wg, wc, wout, g_post)


def _mlp_kernel(x_ref, shift_ref, scale_ref, gate_ref, g_ref, w1_ref, w2_ref, gpost_ref, o_ref, *, ff_chunk):
    x = x_ref[0]
    h = _modulated(x, g_ref[...], shift_ref[0], scale_ref[0]).astype(BF16)
    acc = None
    for c0 in range(0, w1_ref.shape[-1], ff_chunk):
        u = jnp.maximum(jnp.dot(h, w1_ref[:, c0:c0 + ff_chunk], preferred_element_type=F32), 0.0)
        part = jnp.dot((u * u).astype(BF16), w2_ref[c0:c0 + ff_chunk, :], preferred_element_type=F32)
        acc = part if acc is None else acc + part
    o_ref[0] = x + gate_ref[0] * _rms(acc, gpost_ref[...])


def _mlp(x, mod, g_pre, w1, w2, g_post, *, tm):
    b, l, d = x.shape
    tm = min(tm, l)
    tok = lambda w: pl.BlockSpec((1, tm, w), lambda bi, i: (bi, i, 0))
    modc = lambda k: pl.BlockSpec((1, 1, d), lambda bi, i: (bi, 0, k))
    const = lambda a: pl.BlockSpec(a.shape, lambda bi, i: (0,) * a.ndim)
    return pl.pallas_call(
        functools.partial(_mlp_kernel, ff_chunk=1024),
        grid=(b, l // tm),
        in_specs=[tok(d), modc(3), modc(4), modc(5), const(g_pre), const(w1), const(w2), const(g_post)],
        out_specs=tok(d),
        out_shape=jax.ShapeDtypeStruct((b, l, d), F32),
        compiler_params=_params(("arbitrary", "arbitrary")),
        name="mlp",
    )(x, mod, mod, mod, g_pre, w1, w2, g_post)


def _angles(i, j, period):
    return (np.outer(i, j) % period).astype(np.float64) * (2.0 * np.pi / period)


def _mxu_table(values):
    return jnp.asarray(values, F32).astype(BF16)


def _rope_tables(seq_len):
    rows = seq_len // GRID_W
    row = np.repeat(np.arange(rows, dtype=np.float64), GRID_W)
    col = np.tile(np.arange(GRID_W, dtype=np.float64), rows)
    inv_freq = ROPE_THETA ** (-np.arange(ROPE_FREQS, dtype=np.float64) / ROPE_FREQS)
    ang_r, ang_c = row[:, None] * inv_freq, col[:, None] * inv_freq
    cos64 = np.concatenate([np.cos(ang_r)] * 2 + [np.cos(ang_c)] * 2, axis=-1)
    sin64 = np.concatenate([-np.sin(ang_r), np.sin(ang_r), -np.sin(ang_c), np.sin(ang_c)], axis=-1)
    return (jnp.asarray(np.concatenate([cos64] * 2, axis=-1), F32),
            jnp.asarray(np.concatenate([sin64] * 2, axis=-1), F32))


def _seq_dft(l):
    th = _angles(np.arange(l), np.arange(l), l)
    return _mxu_table(np.cos(th)), _mxu_table(np.sin(th))


def _two_stage_dft(l, n2):
    n1 = l // n2
    r = BF16_SUBLANES
    i1, i2 = np.arange(n1), np.arange(n2)
    eye = np.eye(r)
    th_a = _angles(i1, i1, n1)
    m_a = np.concatenate([np.kron(np.cos(th_a), eye), np.kron(np.sin(th_a), eye)], axis=0)
    th_t = _angles(i1, i2, l).reshape(n1, n2 // r, r)
    th_t = np.transpose(th_t, (1, 0, 2)).reshape(n2 // r, n1 * r)
    lanes = lambda t: jnp.asarray(np.broadcast_to(t[:, :, None], t.shape + (2 * HEAD_DIM,)), F32)
    th_c = _angles(i2, i2, n2)
    perm = lambda w: _mxu_table(np.einsum('kb,ij->kijb', w, eye).reshape(n2 * r, r * n2))
    return _mxu_table(m_a), lanes(np.cos(th_t)), lanes(np.sin(th_t)), perm(np.cos(th_c)), perm(np.sin(th_c))


def _channel_dft(n_groups):
    g = FOURIER_GROUP_DIM
    th = _angles(np.arange(g), np.arange(g), g)
    eye = np.eye(n_groups)
    scale = float(g) ** -0.5
    return _mxu_table(np.concatenate([np.kron(eye, np.cos(th) * scale), np.kron(eye, np.sin(th) * scale)], axis=-1))


def _seq_fourier(l):
    n2 = FOURIER_STAGE_LEN
    if l % n2 == 0 and (l // n2) % BF16_SUBLANES == 0:
        return functools.partial(_fourier_two_stage, _two_stage_dft(l, n2))
    return functools.partial(_fourier_dense, _seq_dft(l))


def _mix_and_mlp(x, mod, lw, tok, kv_bufs, fourier, *, dims, lam_init, key_start, key_len):
    qd, qg, xf, vc = tok
    od, od_last, og = _attention(qd, qg, kv_bufs, lw["lam_p"], lw["subln"], dims=dims, lam_init=lam_init,
                                 tq=TILE_ROWS["attention"], key_start=key_start, key_len=key_len)
    yf = fourier(xf)
    yc = _conv(vc, lw["conv_w"], lw["conv_b"], lw["conv_ln_g"], lw["conv_ln_b"], rows=TILE_ROWS["conv"])
    x = _merge(x, mod, lw["g_pre_mix"], lw["w_gate"], yf, od, od_last, og, yc, lw["w_br_f"], lw["w_br_d"],
               lw["w_br_g"], lw["w_br_c"], lw["w_out"], lw["g_post_mix"], tm=TILE_ROWS["merge"])
    return _mlp(x, mod, lw["g_pre_mlp"], lw["w_ff1"], lw["w_ff2"], lw["g_post_mlp"], tm=TILE_ROWS["mlp"])


def kernel(x, c, ctx, c_ctx, w_mod, b_mod, g_pre_mix, g_post_mix, g_pre_mlp, g_post_mlp, w_in, q_norm, k_norm,
           diff_lambda, diff_subln, conv_dw, conv_dw_bias, conv_ln_g, conv_ln_b, w_br_fourier, w_br_diff,
           w_br_gqa, w_br_conv, w_out, w_ff1, w_ff2):
    b, l, d = x.shape
    lc = ctx.shape[1]
    lk = l + lc
    depth = w_in.shape[0]
    dims = _dims(d)
    assert b + 1 <= MOD_ROWS and l % GRID_W == 0 and l % lc == 0

    rope_x = _rope_tables(l)
    rope_none = (jnp.ones((lc, 2 * HEAD_DIM), F32), jnp.zeros((lc, 2 * HEAD_DIM), F32))
    fourier_x, fourier_c = _seq_fourier(l), _seq_fourier(lc)
    n_groups = dims["fourier"] // FOURIER_GROUP_DIM
    heads = np.arange(dims["diff_qk"]) // HEAD_DIM
    ones_bd = jnp.asarray((heads[:, None] == heads[None, :]).astype(BF16))
    dft_ch = _channel_dft(n_groups)

    c_rows = jnp.zeros((MOD_ROWS, d), F32).at[:b].set(c).at[b].set(c_ctx)
    mod_all = _modulation(c_rows, w_mod, b_mod)

    n_tok_cols = w_in.shape[-1] - N_BRANCHES * d

    kv_bufs = (jnp.zeros((b, dims["diff_qk"], lk), BF16), jnp.zeros((b, dims["gqa_kv"], lk), BF16),
               jnp.zeros((b, lk, dims["diff_v"]), BF16), jnp.zeros((b, lk, dims["gqa_kv"]), BF16))
    xc = ctx
    for li in range(depth):
        lam_init = 0.8 - 0.6 * math.exp(-0.3 * li)
        row = lambda a: a[li].reshape(1, -1)
        lw = dict(
            w_gate=lax.optimization_barrier(w_in[li, :, n_tok_cols:]).astype(BF16),
            g_pre_mix=row(g_pre_mix), g_post_mix=row(g_post_mix), g_pre_mlp=row(g_pre_mlp),
            g_post_mlp=row(g_post_mlp),
            qn=jnp.tile(q_norm[li], dims["gqa_q"] // HEAD_DIM).reshape(1, -1),
            kn=jnp.tile(k_norm[li], dims["gqa_kv"] // HEAD_DIM).reshape(1, -1),
            lam_p=diff_lambda[li], subln=row(diff_subln),
            conv_w=conv_dw[li], conv_b=row(conv_dw_bias), conv_ln_g=row(conv_ln_g), conv_ln_b=row(conv_ln_b),
            w_br_f=w_br_fourier[li].astype(BF16), w_br_d=w_br_diff[li].astype(BF16),
            w_br_g=w_br_gqa[li].astype(BF16), w_br_c=w_br_conv[li].astype(BF16), w_out=w_out[li].astype(BF16),
            w_ff1=w_ff1[li].astype(BF16), w_ff2=w_ff2[li].astype(BF16))
        mod_x = mod_all[li, :b].reshape(b, 1, N_MOD * d)
        mod_c = jnp.broadcast_to(mod_all[li, b].reshape(1, 1, N_MOD * d), (b, 1, N_MOD * d))
        last = li == depth - 1

        inproj = functools.partial(_inproj, g_pre=lw["g_pre_mix"], w_in=w_in, ones_bd=ones_bd, qn=lw["qn"],
                                   kn=lw["kn"], dft_ch=dft_ch, dims=dims, layer=li, n_tok_cols=n_tok_cols)
        tok_c, kv_bufs = inproj(xc, mod_c, cos_t=rope_none[0], sin_t=rope_none[1], kv_bufs=kv_bufs, kv_only=last,
                                row_start=l, tm=TILE_ROWS["inproj"])
        tok_x, kv_bufs = inproj(x, mod_x, cos_t=rope_x[0], sin_t=rope_x[1], kv_bufs=kv_bufs, kv_only=False,
                                row_start=0, tm=TILE_ROWS["inproj"])
        mix = functools.partial(_mix_and_mlp, lw=lw, kv_bufs=kv_bufs, dims=dims, lam_init=lam_init)
        x = mix(x, mod_x, tok=tok_x, fourier=fourier_x, key_start=0, key_len=lk)
        if not last:
            xc = mix(xc, mod_c, tok=tok_c, fourier=fourier_c, key_start=l, key_len=lc)
    return x
```

```python
import functools
import math

import jax
import jax.numpy as jnp
import numpy as np
from jax import lax
from jax.experimental import pallas as pl
from jax.experimental.pallas import tpu as pltpu

F32 = jnp.float32
BF16 = jnp.bfloat16

GRID_W = 64
HEAD_DIM = 64
ROPE_FREQS = HEAD_DIM // 4
ROPE_THETA = 10000.0
EPS = 1e-6
ATTN_SCALE = HEAD_DIM ** -0.5
Q_PRESCALE = ATTN_SCALE * math.log2(math.e)
FOURIER_GROUP_DIM = 64
CONV_KERNEL = 31
CONV_HALO = 16
N_BRANCHES = 4
N_MOD = 6
MOD_ROWS = 8

V7X_VMEM_LIMIT_BYTES = 56 * 1024 * 1024
V7X_MXU_COLUMNS = 256
SUBLANES = 8
BF16_SUBLANES = 16
FOURIER_STAGE_LEN = 64

TILE_ROWS = dict(inproj=512, attention=256, conv=512, merge=512, mlp=512)


def _dims(d_model):
    fourier = 3 * d_model // 8
    diff_heads = d_model // 256
    diff_qk = diff_heads * 2 * HEAD_DIM
    diff_v = diff_heads * 2 * HEAD_DIM
    gqa_q = (d_model // 128) * HEAD_DIM
    gqa_kv = (d_model // 128 // 4) * HEAD_DIM
    conv = 3 * d_model // 8
    return dict(fourier=fourier, diff_heads=diff_heads, diff_qk=diff_qk, diff_v=diff_v,
                gqa_q=gqa_q, gqa_kv=gqa_kv, conv=conv)


def _params(semantics):
    return pltpu.CompilerParams(dimension_semantics=semantics, vmem_limit_bytes=V7X_VMEM_LIMIT_BYTES)


def _mod_kernel(c_ref, w_ref, b_ref, o_ref):
    c = c_ref[...]
    a = c * jax.nn.sigmoid(c)
    w = w_ref[...]
    a_hi = a.astype(BF16)
    a_lo = (a - a_hi.astype(F32)).astype(BF16)
    w_hi = w.astype(BF16)
    w_lo = (w - w_hi.astype(F32)).astype(BF16)
    dot = functools.partial(jnp.dot, preferred_element_type=F32)
    o_ref[...] = dot(a_hi, w_hi) + (dot(a_lo, w_hi) + dot(a_hi, w_lo)) + b_ref[...]


def _modulation(c_rows, w_mod, b_mod):
    depth, d, n = w_mod.shape
    tn = 2 * d
    return pl.pallas_call(
        _mod_kernel,
        grid=(depth, n // tn),
        in_specs=[pl.BlockSpec((MOD_ROWS, d), lambda l, j: (0, 0)),
                  pl.BlockSpec((None, d, tn), lambda l, j: (l, 0, j)),
                  pl.BlockSpec((None, 1, tn), lambda l, j: (l, 0, j))],
        out_specs=pl.BlockSpec((None, MOD_ROWS, tn), lambda l, j: (l, 0, j)),
        out_shape=jax.ShapeDtypeStruct((depth, MOD_ROWS, n), F32),
        compiler_params=_params(("arbitrary", "arbitrary")),
        name="modulation",
    )(c_rows, w_mod, b_mod.reshape(depth, 1, n))


def _modulated(x, g, shift, scale):
    h = x * lax.rsqrt(jnp.mean(x * x, axis=-1, keepdims=True) + EPS) * g
    return h * (1.0 + scale) + shift


def _rms(x, g):
    return x * lax.rsqrt(jnp.mean(x * x, axis=-1, keepdims=True) + EPS) * g


def _swap_halves(t):
    w = t.shape[-1]
    lane = lax.broadcasted_iota(jnp.int32, t.shape, 1)
    from_right = pltpu.roll(t, w - ROPE_FREQS, 1)
    from_left = pltpu.roll(t, ROPE_FREQS, 1)
    return jnp.where((lane % (2 * ROPE_FREQS)) < ROPE_FREQS, from_right, from_left)


def _tile_lanes(t, width):
    reps = width // t.shape[-1]
    return t if reps == 1 else jnp.concatenate([t] * reps, axis=-1)


def _head_norm(t, ones_bd, gain):
    w = t.shape[-1]
    t2 = t * t
    hi = t2.astype(BF16)
    lo = (t2 - hi.astype(F32)).astype(BF16)
    e = ones_bd[:w, :w]
    ss = (jnp.dot(hi, e, preferred_element_type=F32) + jnp.dot(lo, e, preferred_element_type=F32))
    return t * lax.rsqrt(ss * (1.0 / HEAD_DIM) + EPS) * gain


def _inproj_kernel(x_ref, shift_ref, scale_ref, g_ref, w_ref, ones_ref, cos_ref, sin_ref, qn_ref, kn_ref,
                   dft_ref, *rest, dims, kv_only, n_alias):
    out_refs = rest[n_alias:]
    nq, nkv, nv, nf, nc = dims["diff_qk"], dims["gqa_kv"], dims["diff_v"], dims["fourier"], dims["conv"]
    ngq = dims["gqa_q"]
    h = _modulated(x_ref[0], g_ref[...], shift_ref[0], scale_ref[0]).astype(BF16)

    def proj(start, width):
        return jnp.dot(h, w_ref[:, start:start + width].astype(BF16), preferred_element_type=F32)

    def rope(t):
        w = t.shape[-1]
        return t * _tile_lanes(cos_ref[...], w) + _swap_halves(t) * _tile_lanes(sin_ref[...], w)

    ones_bd = ones_ref[...]
    off = 0
    dk = proj(off, nq); off += nq
    dv = proj(off, nv); off += nv
    gk = proj(off, nkv); off += nkv
    gv = proj(off, nkv); off += nkv
    kdt_ref, kgt_ref, vd_ref, vg_ref = out_refs[-4:]
    kdt_ref[0] = rope(dk).T.astype(BF16)
    kgt_ref[0] = rope(_head_norm(gk, ones_bd, kn_ref[...])).T.astype(BF16)
    vd_ref[0] = dv.astype(BF16)
    vg_ref[0] = gv.astype(BF16)
    if not kv_only:
        qd_ref, qg_ref, xf_ref, vc_ref = out_refs[:4]
        uf = proj(off, nf).astype(BF16); off += nf
        xf_ref[0] = jnp.dot(uf, dft_ref[...], preferred_element_type=F32).astype(BF16)
        dq = proj(off, nq); off += nq
        qd_ref[0] = (rope(dq) * Q_PRESCALE).astype(BF16)
        gq = proj(off, ngq); off += ngq
        qg_ref[0] = (rope(_head_norm(gq, ones_bd, qn_ref[...])) * Q_PRESCALE).astype(BF16)
        glu = proj(off, 2 * nc)
        vc_ref[0] = glu[:, :nc] * jax.nn.sigmoid(glu[:, nc:])


def _inproj(x, mod, g_pre, w_in, ones_bd, cos_t, sin_t, qn, kn, dft_ch, kv_bufs, *, dims, layer, n_tok_cols,
            kv_only, row_start, tm):
    b, l, d = x.shape
    tm = min(tm, l)
    nq, nkv, nv, nf, nc, ngq = (dims["diff_qk"], dims["gqa_kv"], dims["diff_v"], dims["fourier"],
                                dims["conv"], dims["gqa_q"])
    blk0 = row_start // tm
    assert row_start % tm == 0
    tok = lambda w: pl.BlockSpec((1, tm, w), lambda bi, i: (bi, i, 0))
    const = lambda shape: pl.BlockSpec(shape, lambda bi, i: (0,) * len(shape))
    tok_outs = [] if kv_only else [((b, l, nq), BF16), ((b, l, ngq), BF16), ((b, l, 2 * nf), BF16), ((b, l, nc), F32)]
    kv_shapes = [(a.shape, a.dtype) for a in kv_bufs]
    kv_specs = [pl.BlockSpec((1, nq, tm), lambda bi, i: (bi, 0, blk0 + i)),
                pl.BlockSpec((1, nkv, tm), lambda bi, i: (bi, 0, blk0 + i)),
                pl.BlockSpec((1, tm, nv), lambda bi, i: (bi, blk0 + i, 0)),
                pl.BlockSpec((1, tm, nkv), lambda bi, i: (bi, blk0 + i, 0))]
    n_alias = len(kv_bufs)
    n_in = 11
    aliases = {n_in + k: len(tok_outs) + k for k in range(n_alias)}
    outs = pl.pallas_call(
        functools.partial(_inproj_kernel, dims=dims, kv_only=kv_only, n_alias=n_alias),
        grid=(b, l // tm),
        in_specs=[tok(d),
                  pl.BlockSpec((1, 1, d), lambda bi, i: (bi, 0, 0)),
                  pl.BlockSpec((1, 1, d), lambda bi, i: (bi, 0, 1)),
                  const((1, d)),
                  pl.BlockSpec((None, d, n_tok_cols), lambda bi, i: (layer, 0, 0)),
                  const(ones_bd.shape),
                  pl.BlockSpec((tm, 2 * HEAD_DIM), lambda bi, i: (i, 0)),
                  pl.BlockSpec((tm, 2 * HEAD_DIM), lambda bi, i: (i, 0)),
                  const(qn.shape), const(kn.shape), const(dft_ch.shape)]
                 + [pl.BlockSpec(memory_space=pl.ANY)] * n_alias,
        out_specs=[tok(s[-1]) for s, _ in tok_outs] + kv_specs,
        out_shape=[jax.ShapeDtypeStruct(s, dt) for s, dt in tok_outs + kv_shapes],
        input_output_aliases=aliases,
        compiler_params=_params(("arbitrary", "arbitrary")),
        name="inproj_kv" if kv_only else "inproj",
    )(x, mod, mod, g_pre, w_in, ones_bd, cos_t, sin_t, qn, kn, dft_ch, *kv_bufs)
    return outs[:len(tok_outs)], outs[len(tok_outs):]


def _conv_ln_silu(win_ref, w_ref, b_ref, g_ref, beta_ref, yc_ref, *, rows):
    first = CONV_HALO - CONV_KERNEL // 2
    acc = jnp.zeros((rows, w_ref.shape[-1]), F32) + b_ref[...]
    for phase in range(SUBLANES):
        part = None
        for t in range(CONV_KERNEL):
            if (t + first) % SUBLANES != phase:
                continue
            base = t + first - phase
            term = w_ref[t:t + 1, :] * win_ref[base:base + rows + SUBLANES, :]
            part = term if part is None else part + term
        acc = acc + part[phase:phase + rows, :]
    mu = jnp.mean(acc, axis=-1, keepdims=True)
    cen = acc - mu
    var = jnp.mean(cen * cen, axis=-1, keepdims=True)
    y = cen * lax.rsqrt(var + EPS) * g_ref[...] + beta_ref[...]
    yc_ref[...] = (y * jax.nn.sigmoid(y)).astype(BF16)


def _conv_kernel(vprev_ref, vcur_ref, vnext_ref, cw_ref, cb_ref, cg_ref, cbeta_ref, yc_ref, win_ref):
    i = pl.program_id(1)
    rows = vcur_ref.shape[1]
    win_ref[:CONV_HALO, :] = jnp.where(i > 0, vprev_ref[0], 0.0)
    win_ref[CONV_HALO:CONV_HALO + rows, :] = vcur_ref[0]
    win_ref[CONV_HALO + rows:, :] = jnp.where(i < pl.num_programs(1) - 1, vnext_ref[0], 0.0)
    _conv_ln_silu(win_ref, cw_ref, cb_ref, cg_ref, cbeta_ref, yc_ref.at[0], rows=rows)


def _conv(vc, conv_w, conv_b, conv_g, conv_beta, *, rows):
    b, l, c = vc.shape
    rows = min(rows, l)
    hb = rows // CONV_HALO
    n_hb = l // CONV_HALO
    const = lambda a: pl.BlockSpec(a.shape, lambda bi, i: (0,) * a.ndim)
    prev = pl.BlockSpec((1, CONV_HALO, c), lambda bi, i: (bi, jnp.maximum(i * hb - 1, 0), 0))
    nxt = pl.BlockSpec((1, CONV_HALO, c), lambda bi, i: (bi, jnp.minimum((i + 1) * hb, n_hb - 1), 0))
    tile = pl.BlockSpec((1, rows, c), lambda bi, i: (bi, i, 0))
    return pl.pallas_call(
        _conv_kernel,
        grid=(b, l // rows),
        in_specs=[prev, tile, nxt, const(conv_w), const(conv_b), const(conv_g), const(conv_beta)],
        out_specs=tile,
        out_shape=jax.ShapeDtypeStruct((b, l, c), BF16),
        scratch_shapes=[pltpu.VMEM((rows + 2 * CONV_HALO, c), F32)],
        compiler_params=_params(("arbitrary", "arbitrary")),
        name="conv",
    )(vc, vc, vc, conv_w, conv_b, conv_g, conv_beta)


def _softmax_pv(q, k_t, v_ext):
    s = jnp.dot(q, k_t, preferred_element_type=F32)
    e = jnp.exp2(s - jnp.max(s, axis=-1, keepdims=True))
    return jnp.dot(e.astype(BF16), v_ext, preferred_element_type=F32)


def _attn_kernel(qd_ref, qg_ref, kdt_ref, vd_ref, kgt_ref, vg_ref, lam_ref, subln_ref, od_ref, og_ref, odl_ref,
                 vdx_ref, vgx_ref, s_ref, m0_ref, *, n_diff, n_gq, n_kv, lam_init):
    dh = HEAD_DIM
    dv = 2 * HEAD_DIM
    i = pl.program_id(1)
    tq = qd_ref.shape[1]
    last = n_diff - 1

    @pl.when(i == 0)
    def _():
        for hd in range(n_diff):
            vdx_ref[hd, :, :dv] = vd_ref[0, :, hd * dv:(hd + 1) * dv]
            vdx_ref[hd, :, dv:] = jnp.ones((vd_ref.shape[1], V7X_MXU_COLUMNS - dv), BF16)
        vgx_ref[:, :n_kv * dh] = vg_ref[0]
        vgx_ref[:, n_kv * dh:] = jnp.ones((vg_ref.shape[1], V7X_MXU_COLUMNS - n_kv * dh), BF16)
        s_ref[...] = jnp.zeros(s_ref.shape, F32)
        m0_ref[...] = jnp.zeros(m0_ref.shape, F32)

    lv = lam_ref[...]
    lam = (jnp.exp(jnp.sum(lv[0:1] * lv[1:2], axis=-1, keepdims=True))
           - jnp.exp(jnp.sum(lv[2:3] * lv[3:4], axis=-1, keepdims=True)) + lam_init)

    def diff_head_out(m0, pv1):
        o = m0 - lam * (pv1[:, :dv] / pv1[:, dv:])
        return (_rms(o, subln_ref[...]) * (1.0 - lam_init)).astype(BF16)

    def finish_last_diff_head(row0):
        s = s_ref[...]
        e = jnp.exp2(s - jnp.max(s, axis=-1, keepdims=True))
        pv1 = jnp.dot(e.astype(BF16), vdx_ref[last], preferred_element_type=F32)
        odl_ref[0, pl.ds(row0, tq), :] = diff_head_out(m0_ref[...], pv1)

    finish_last_diff_head(pl.multiple_of(jnp.maximum(i - 1, 0) * tq, tq))
    for hd in range(last):
        c = 2 * hd * dh
        pv0 = _softmax_pv(qd_ref[0, :, c:c + dh], kdt_ref[0, c:c + dh, :], vdx_ref[hd])
        pv1 = _softmax_pv(qd_ref[0, :, c + dh:c + 2 * dh], kdt_ref[0, c + dh:c + 2 * dh, :], vdx_ref[hd])
        od_ref[0, :, hd * dv:(hd + 1) * dv] = diff_head_out(pv0[:, :dv] / pv0[:, dv:], pv1)
    group = n_gq // n_kv
    lane = lax.broadcasted_iota(jnp.int32, (qg_ref.shape[1], dv), 1)
    for pair in range(n_gq // 2):
        kv = (2 * pair) // group
        halves = []
        for hq in (2 * pair, 2 * pair + 1):
            pv = _softmax_pv(qg_ref[0, :, hq * dh:(hq + 1) * dh], kgt_ref[0, kv * dh:(kv + 1) * dh, :], vgx_ref[...])
            halves.append(pv[:, :dv] / pv[:, dv:])
        left = halves[0] if kv % 2 == 0 else pltpu.roll(halves[0], dh, 1)
        right = pltpu.roll(halves[1], dh, 1) if kv % 2 == 0 else halves[1]
        og_ref[0, :, pair * dv:(pair + 1) * dv] = jnp.where(lane < dh, left, right).astype(BF16)

    c = 2 * last * dh
    pv0 = _softmax_pv(qd_ref[0, :, c:c + dh], kdt_ref[0, c:c + dh, :], vdx_ref[last])
    m0_ref[...] = pv0[:, :dv] / pv0[:, dv:]
    s_ref[...] = jnp.dot(qd_ref[0, :, c + dh:c + 2 * dh], kdt_ref[0, c + dh:c + 2 * dh, :],
                         preferred_element_type=F32)

    @pl.when(i == pl.num_programs(1) - 1)
    def _():
        finish_last_diff_head(pl.multiple_of(i * tq, tq))


def _attention(qd, qg, kv_bufs, lam_p, subln, *, dims, lam_init, tq, key_start, key_len):
    kd_t, kg_t, vd, vg = kv_bufs
    b, l, nq = qd.shape
    n_diff, n_gq, n_kv = dims["diff_heads"], dims["gqa_q"] // HEAD_DIM, dims["gqa_kv"] // HEAD_DIM
    assert n_kv * HEAD_DIM == 2 * HEAD_DIM and (n_gq // n_kv) % 2 == 0 and key_start % key_len == 0
    kb = key_start // key_len
    tq = min(tq, l)
    tok = lambda w: pl.BlockSpec((1, tq, w), lambda bi, i: (bi, i, 0))
    keys_t = lambda a: pl.BlockSpec((1, a.shape[1], key_len), lambda bi, i: (bi, 0, kb))
    vals = lambda a: pl.BlockSpec((1, key_len, a.shape[2]), lambda bi, i: (bi, kb, 0))
    const = lambda a: pl.BlockSpec(a.shape, lambda bi, i: (0,) * a.ndim)
    dv = 2 * HEAD_DIM
    od, og, od_last = pl.pallas_call(
        functools.partial(_attn_kernel, n_diff=n_diff, n_gq=n_gq, n_kv=n_kv, lam_init=lam_init),
        grid=(b, l // tq),
        in_specs=[tok(nq), tok(qg.shape[-1]), keys_t(kd_t), vals(vd), keys_t(kg_t), vals(vg),
                  const(lam_p), const(subln)],
        out_specs=[tok((n_diff - 1) * dv), tok(dims["gqa_q"]), pl.BlockSpec((1, l, dv), lambda bi, i: (bi, 0, 0))],
        out_shape=[jax.ShapeDtypeStruct((b, l, (n_diff - 1) * dv), BF16),
                   jax.ShapeDtypeStruct((b, l, dims["gqa_q"]), BF16),
                   jax.ShapeDtypeStruct((b, l, dv), BF16)],
        scratch_shapes=[pltpu.VMEM((n_diff, key_len, V7X_MXU_COLUMNS), BF16),
                        pltpu.VMEM((key_len, V7X_MXU_COLUMNS), BF16),
                        pltpu.VMEM((tq, key_len), F32),
                        pltpu.VMEM((tq, dv), F32)],
        compiler_params=_params(("arbitrary", "arbitrary")),
        name="attention",
    )(qd, qg, kd_t, vd, kg_t, vg, lam_p, subln)
    return od, od_last, og


def _fourier_kernel(c_ref, s_ref, xf_ref, o_ref, *, nf, out_scale):
    y = (jnp.dot(c_ref[...], xf_ref[0, :, :nf], preferred_element_type=F32)
         - jnp.dot(s_ref[...], xf_ref[0, :, nf:], preferred_element_type=F32))
    o_ref[0] = (y * out_scale).astype(BF16)


def _fourier_dense(tables, xf, *, tr=512):
    dft_c, dft_s = tables
    b, l, nf2 = xf.shape
    nf = nf2 // 2
    tr = min(tr, l)
    return pl.pallas_call(
        functools.partial(_fourier_kernel, nf=nf, out_scale=float(l) ** -0.5),
        grid=(b, l // tr),
        in_specs=[pl.BlockSpec((tr, l), lambda bi, i: (i, 0)),
                  pl.BlockSpec((tr, l), lambda bi, i: (i, 0)),
                  pl.BlockSpec((1, l, nf2), lambda bi, i: (bi, 0, 0))],
        out_specs=pl.BlockSpec((1, tr, nf), lambda bi, i: (bi, i, 0)),
        out_shape=jax.ShapeDtypeStruct((b, l, nf), BF16),
        compiler_params=_params(("arbitrary", "arbitrary")),
        name="fourier",
    )(dft_c, dft_s, xf)


def _fft_a_kernel(z_ref, m_ref, cos_ref, sin_ref, o_ref, *, nf):
    n1, r = z_ref.shape[1], z_ref.shape[2]
    rows = n1 * r
    z = z_ref[0].reshape(rows, 2 * nf)
    t = jnp.dot(m_ref[...], z, preferred_element_type=F32)
    cz, sz = t[:rows], t[rows:]
    ar = cz[:, :nf] - sz[:, nf:]
    ai = -(sz[:, :nf] + cz[:, nf:])
    ct = _tile_lanes(cos_ref[0], nf)
    st = _tile_lanes(sin_ref[0], nf)
    o_ref[0, :, :, :nf] = (ar * ct + ai * st).reshape(n1, r, nf).astype(BF16)
    o_ref[0, :, :, nf:] = (ai * ct - ar * st).reshape(n1, r, nf).astype(BF16)


def _fft_c_kernel(b_ref, mc_ref, ms_ref, o_ref, *, nf, out_scale):
    r, n2 = b_ref.shape[1], b_ref.shape[2]
    blk = b_ref[0].reshape(r * n2, 2 * nf)
    y = (jnp.dot(mc_ref[...], blk[:, :nf], preferred_element_type=F32)
         + jnp.dot(ms_ref[...], blk[:, nf:], preferred_element_type=F32))
    o_ref[0] = (y * out_scale).reshape(n2, r, nf).astype(BF16)


def _fourier_two_stage(tables, xf):
    m_a, tw_cos, tw_sin, m_c_cos, m_c_sin = tables
    b, l, nf2 = xf.shape
    nf = nf2 // 2
    r = BF16_SUBLANES
    n2 = m_c_cos.shape[0] // r
    n1 = l // n2
    tile = lambda w: pl.BlockSpec((1, n1, r, w), lambda bi, s: (bi, 0, s, 0))
    const = lambda a: pl.BlockSpec(a.shape, lambda bi, s: (0,) * a.ndim)
    twiddle = pl.BlockSpec((1, n1 * r, 2 * HEAD_DIM), lambda bi, s: (s, 0, 0))
    staged = pl.pallas_call(
        functools.partial(_fft_a_kernel, nf=nf),
        grid=(b, n2 // r),
        in_specs=[tile(nf2), const(m_a), twiddle, twiddle],
        out_specs=tile(nf2),
        out_shape=jax.ShapeDtypeStruct((b, n1, n2, nf2), BF16),
        compiler_params=_params(("arbitrary", "arbitrary")),
        name="fourier_a",
    )(xf.reshape(b, n1, n2, nf2), m_a, tw_cos, tw_sin)
    out = pl.pallas_call(
        functools.partial(_fft_c_kernel, nf=nf, out_scale=float(l) ** -0.5),
        grid=(b, n1 // r),
        in_specs=[pl.BlockSpec((1, r, n2, nf2), lambda bi, s: (bi, s, 0, 0)), const(m_c_cos), const(m_c_sin)],
        out_specs=pl.BlockSpec((1, n2, r, nf), lambda bi, s: (bi, 0, s, 0)),
        out_shape=jax.ShapeDtypeStruct((b, n2, n1, nf), BF16),
        compiler_params=_params(("arbitrary", "arbitrary")),
        name="fourier_c",
    )(staged, m_c_cos, m_c_sin)
    return out.reshape(b, l, nf)


def _merge_kernel(x_ref, shift_ref, scale_ref, gate_ref, g_ref, wgate_ref, yf_ref, od_ref, odl_ref, og_ref, yc_ref,
                  wf_ref, wd_ref, wg_ref, wc_ref, wout_ref, gpost_ref, o_ref):
    x = x_ref[0]
    d = x.shape[-1]
    h = _modulated(x, g_ref[...], shift_ref[0], scale_ref[0]).astype(BF16)
    diff = jnp.concatenate([od_ref[0], odl_ref[0]], axis=-1)
    branches = ((yf_ref[0], wf_ref), (diff, wd_ref), (og_ref[0], wg_ref), (yc_ref[0], wc_ref))
    merged = None
    for k, (y, w_ref) in enumerate(branches):
        gate = jax.nn.sigmoid(jnp.dot(h, wgate_ref[:, k * d:(k + 1) * d], preferred_element_type=F32))
        term = gate * jnp.dot(y, w_ref[...], preferred_element_type=F32)
        merged = term if merged is None else merged + term
    mix = jnp.dot(merged.astype(BF16), wout_ref[...], preferred_element_type=F32)
    o_ref[0] = x + gate_ref[0] * _rms(mix, gpost_ref[...])


def _merge(x, mod, g_pre, w_gate, yf, od, od_last, og, yc, wf, wd, wg, wc, wout, g_post, *, tm):
    b, l, d = x.shape
    tm = min(tm, l)
    tok = lambda w: pl.BlockSpec((1, tm, w), lambda bi, i: (bi, i, 0))
    modc = lambda k: pl.BlockSpec((1, 1, d), lambda bi, i: (bi, 0, k))
    const = lambda a: pl.BlockSpec(a.shape, lambda bi, i: (0,) * a.ndim)
    return pl.pallas_call(
        _merge_kernel,
        grid=(b, l // tm),
        in_specs=[tok(d), modc(0), modc(1), modc(2), const(g_pre), const(w_gate),
                  tok(yf.shape[-1]), tok(od.shape[-1]), tok(od_last.shape[-1]), tok(og.shape[-1]), tok(yc.shape[-1]),
                  const(wf), const(wd), const(wg), const(wc), const(wout), const(g_post)],
        out_specs=tok(d),
        out_shape=jax.ShapeDtypeStruct((b, l, d), F32),
        compiler_params=_params(("arbitrary", "arbitrary")),
        name="merge",
    )(x, mod, mod, mod, g_pre, w_gate, yf, od, od_last, og, yc, wf, wd, wg, wc, wout, g_post)


def _mlp_kernel(x_ref, shift_ref, scale_ref, gate_ref, g_ref, w1_ref, w2_ref, gpost_ref, o_ref, *, ff_chunk):
    x = x_ref[0]
    h = _modulated(x, g_ref[...], shift_ref[0], scale_ref[0]).astype(BF16)
    acc = None
    for c0 in range(0, w1_ref.shape[-1], ff_chunk):
        u = jnp.maximum(jnp.dot(h, w1_ref[:, c0:c0 + ff_chunk], preferred_element_type=F32), 0.0)
        part = jnp.dot((u * u).astype(BF16), w2_ref[c0:c0 + ff_chunk, :], preferred_element_type=F32)
        acc = part if acc is None else acc + part
    o_ref[0] = x + gate_ref[0] * _rms(acc, gpost_ref[...])


def _mlp(x, mod, g_pre, w1, w2, g_post, *, tm):
    b, l, d = x.shape
    tm = min(tm, l)
    tok = lambda w: pl.BlockSpec((1, tm, w), lambda bi, i: (bi, i, 0))
    modc = lambda k: pl.BlockSpec((1, 1, d), lambda bi, i: (bi, 0, k))
    const = lambda a: pl.BlockSpec(a.shape, lambda bi, i: (0,) * a.ndim)
    return pl.pallas_call(
        functools.partial(_mlp_kernel, ff_chunk=1024),
        grid=(b, l // tm),
        in_specs=[tok(d), modc(3), modc(4), modc(5), const(g_pre), const(w1), const(w2), const(g_post)],
        out_specs=tok(d),
        out_shape=jax.ShapeDtypeStruct((b, l, d), F32),
        compiler_params=_params(("arbitrary", "arbitrary")),
        name="mlp",
    )(x, mod, mod, mod, g_pre, w1, w2, g_post)


def _angles(i, j, period):
    return (np.outer(i, j) % period).astype(np.float64) * (2.0 * np.pi / period)


def _mxu_table(values):
    return jnp.asarray(values, F32).astype(BF16)


def _rope_tables(seq_len):
    rows = seq_len // GRID_W
    row = np.repeat(np.arange(rows, dtype=np.float64), GRID_W)
    col = np.tile(np.arange(GRID_W, dtype=np.float64), rows)
    inv_freq = ROPE_THETA ** (-np.arange(ROPE_FREQS, dtype=np.float64) / ROPE_FREQS)
    ang_r, ang_c = row[:, None] * inv_freq, col[:, None] * inv_freq
    cos64 = np.concatenate([np.cos(ang_r)] * 2 + [np.cos(ang_c)] * 2, axis=-1)
    sin64 = np.concatenate([-np.sin(ang_r), np.sin(ang_r), -np.sin(ang_c), np.sin(ang_c)], axis=-1)
    return (jnp.asarray(np.concatenate([cos64] * 2, axis=-1), F32),
            jnp.asarray(np.concatenate([sin64] * 2, axis=-1), F32))


def _seq_dft(l):
    th = _angles(np.arange(l), np.arange(l), l)
    return _mxu_table(np.cos(th)), _mxu_table(np.sin(th))


def _two_stage_dft(l, n2):
    n1 = l // n2
    r = BF16_SUBLANES
    i1, i2 = np.arange(n1), np.arange(n2)
    eye = np.eye(r)
    th_a = _angles(i1, i1, n1)
    m_a = np.concatenate([np.kron(np.cos(th_a), eye), np.kron(np.sin(th_a), eye)], axis=0)
    th_t = _angles(i1, i2, l).reshape(n1, n2 // r, r)
    th_t = np.transpose(th_t, (1, 0, 2)).reshape(n2 // r, n1 * r)
    lanes = lambda t: jnp.asarray(np.broadcast_to(t[:, :, None], t.shape + (2 * HEAD_DIM,)), F32)
    th_c = _angles(i2, i2, n2)
    perm = lambda w: _mxu_table(np.einsum('kb,ij->kijb', w, eye).reshape(n2 * r, r * n2))
    return _mxu_table(m_a), lanes(np.cos(th_t)), lanes(np.sin(th_t)), perm(np.cos(th_c)), perm(np.sin(th_c))


def _channel_dft(n_groups):
    g = FOURIER_GROUP_DIM
    th = _angles(np.arange(g), np.arange(g), g)
    eye = np.eye(n_groups)
    scale = float(g) ** -0.5
    return _mxu_table(np.concatenate([np.kron(eye, np.cos(th) * scale), np.kron(eye, np.sin(th) * scale)], axis=-1))


def _seq_fourier(l):
    n2 = FOURIER_STAGE_LEN
    if l % n2 == 0 and (l // n2) % BF16_SUBLANES == 0:
        return functools.partial(_fourier_two_stage, _two_stage_dft(l, n2))
    return functools.partial(_fourier_dense, _seq_dft(l))


def _mix_and_mlp(x, mod, lw, tok, kv_bufs, fourier, *, dims, lam_init, key_start, key_len):
    qd, qg, xf, vc = tok
    od, od_last, og = _attention(qd, qg, kv_bufs, lw["lam_p"], lw["subln"], dims=dims, lam_init=lam_init,
                                 tq=TILE_ROWS["attention"], key_start=key_start, key_len=key_len)
    yf = fourier(xf)
    yc = _conv(vc, lw["conv_w"], lw["conv_b"], lw["conv_ln_g"], lw["conv_ln_b"], rows=TILE_ROWS["conv"])
    x = _merge(x, mod, lw["g_pre_mix"], lw["w_gate"], yf, od, od_last, og, yc, lw["w_br_f"], lw["w_br_d"],
               lw["w_br_g"], lw["w_br_c"], lw["w_out"], lw["g_post_mix"], tm=TILE_ROWS["merge"])
    return _mlp(x, mod, lw["g_pre_mlp"], lw["w_ff1"], lw["w_ff2"], lw["g_post_mlp"], tm=TILE_ROWS["mlp"])


def kernel(x, c, ctx, c_ctx, w_mod, b_mod, g_pre_mix, g_post_mix, g_pre_mlp, g_post_mlp, w_in, q_norm, k_norm,
           diff_lambda, diff_subln, conv_dw, conv_dw_bias, conv_ln_g, conv_ln_b, w_br_fourier, w_br_diff,
           w_br_gqa, w_br_conv, w_out, w_ff1, w_ff2):
    b, l, d = x.shape
    lc = ctx.shape[1]
    lk = l + lc
    depth = w_in.shape[0]
    dims = _dims(d)
    assert b + 1 <= MOD_ROWS and l % GRID_W == 0 and l % lc == 0

    rope_x = _rope_tables(l)
    rope_none = (jnp.ones((lc, 2 * HEAD_DIM), F32), jnp.zeros((lc, 2 * HEAD_DIM), F32))
    fourier_x, fourier_c = _seq_fourier(l), _seq_fourier(lc)
    n_groups = dims["fourier"] // FOURIER_GROUP_DIM
    heads = np.arange(dims["diff_qk"]) // HEAD_DIM
    ones_bd = jnp.asarray((heads[:, None] == heads[None, :]).astype(BF16))
    dft_ch = _channel_dft(n_groups)

    c_rows = jnp.zeros((MOD_ROWS, d), F32).at[:b].set(c).at[b].set(c_ctx)
    mod_all = _modulation(c_rows, w_mod, b_mod)

    n_tok_cols = w_in.shape[-1] - N_BRANCHES * d

    kv_bufs = (jnp.zeros((b, dims["diff_qk"], lk), BF16), jnp.zeros((b, dims["gqa_kv"], lk), BF16),
               jnp.zeros((b, lk, dims["diff_v"]), BF16), jnp.zeros((b, lk, dims["gqa_kv"]), BF16))
    xc = ctx
    for li in range(depth):
        lam_init = 0.8 - 0.6 * math.exp(-0.3 * li)
        row = lambda a: a[li].reshape(1, -1)
        lw = dict(
            w_gate=lax.optimization_barrier(w_in[li, :, n_tok_cols:]).astype(BF16),
            g_pre_mix=row(g_pre_mix), g_post_mix=row(g_post_mix), g_pre_mlp=row(g_pre_mlp),
            g_post_mlp=row(g_post_mlp),
            qn=jnp.tile(q_norm[li], dims["gqa_q"] // HEAD_DIM).reshape(1, -1),
            kn=jnp.tile(k_norm[li], dims["gqa_kv"] // HEAD_DIM).reshape(1, -1),
            lam_p=diff_lambda[li], subln=row(diff_subln),
            conv_w=conv_dw[li], conv_b=row(conv_dw_bias), conv_ln_g=row(conv_ln_g), conv_ln_b=row(conv_ln_b),
            w_br_f=w_br_fourier[li].astype(BF16), w_br_d=w_br_diff[li].astype(BF16),
            w_br_g=w_br_gqa[li].astype(BF16), w_br_c=w_br_conv[li].astype(BF16), w_out=w_out[li].astype(BF16),
            w_ff1=w_ff1[li].astype(BF16), w_ff2=w_ff2[li].astype(BF16))
        mod_x = mod_all[li, :b].reshape(b, 1, N_MOD * d)
        mod_c = jnp.broadcast_to(mod_all[li, b].reshape(1, 1, N_MOD * d), (b, 1, N_MOD * d))
        last = li == depth - 1

        inproj = functools.partial(_inproj, g_pre=lw["g_pre_mix"], w_in=w_in, ones_bd=ones_bd, qn=lw["qn"],
                                   kn=lw["kn"], dft_ch=dft_ch, dims=dims, layer=li, n_tok_cols=n_tok_cols)
        tok_c, kv_bufs = inproj(xc, mod_c, cos_t=rope_none[0], sin_t=rope_none[1], kv_bufs=kv_bufs, kv_only=last,
                                row_start=l, tm=TILE_ROWS["inproj"])
        tok_x, kv_bufs = inproj(x, mod_x, cos_t=rope_x[0], sin_t=rope_x[1], kv_bufs=kv_bufs, kv_only=False,
                                row_start=0, tm=TILE_ROWS["inproj"])
        mix = functools.partial(_mix_and_mlp, lw=lw, kv_bufs=kv_bufs, dims=dims, lam_init=lam_init)
        x = mix(x, mod_x, tok=tok_x, fourier=fourier_x, key_start=0, key_len=lk)
        if not last:
            xc = mix(xc, mod_c, tok=tok_c, fourier=fourier_c, key_start=l, key_len=lc)
    return x
```
